```python
import jax, jax.numpy as jnp
from jax import lax
import numpy as np

D_MODEL = 1024
BATCH = 8
SEQ = 4096
DEPTH = 4

N_MIXERS = 3
SB_HEADS = 16
SB_HEAD_DIM = D_MODEL // SB_HEADS
SB_Q_BLOCK = 128
CONV_WIDTH = 3
GLA_HEADS = 4
GLA_DK = D_MODEL // 2
GLA_DV = D_MODEL
GLA_DK_HEAD = GLA_DK // GLA_HEADS
GLA_DV_HEAD = GLA_DV // GLA_HEADS
GLA_GATE_RANK = 16
GLA_GATE_NORMALIZER = 16.0
GLA_CHUNK = 64
D_FF = 4 * D_MODEL
RMS_EPS = 1e-6

kernel_name = "hybrid_sb_conv_gla_sqrelu_sandwich"


def rms_norm(x, gain):
    xf = x.astype(jnp.float32)
    y = xf * lax.rsqrt(jnp.mean(xf * xf, axis=-1, keepdims=True) + RMS_EPS) * gain.astype(jnp.float32)
    return y.astype(x.dtype)


def stick_breaking_mixer(xn, w_qkv, w_o):
    b, s, _ = xn.shape
    qkv = (xn @ w_qkv).reshape(b, s, 3, SB_HEADS, SB_HEAD_DIM)
    q, k, v = [qkv[:, :, i].transpose(0, 2, 1, 3).astype(jnp.float32) for i in range(3)]
    scale = SB_HEAD_DIM ** -0.5
    n_blocks = s // SB_Q_BLOCK
    q_blocks = q.reshape(b, SB_HEADS, n_blocks, SB_Q_BLOCK, SB_HEAD_DIM).transpose(2, 0, 1, 3, 4)
    key_pos = jnp.arange(s)

    def one_block(args):
        qb, start = args
        z = jnp.einsum('bhqd,bhkd->bhqk', qb, k) * scale
        q_pos = start + jnp.arange(SB_Q_BLOCK)
        mask = key_pos[None, :] < q_pos[:, None]
        log_beta = jax.nn.log_sigmoid(z)
        log_1m_beta = jnp.where(mask, jax.nn.log_sigmoid(-z), 0.0)
        suffix = lax.cumsum(log_1m_beta, axis=3, reverse=True) - log_1m_beta
        weights = jnp.where(mask, jnp.exp(log_beta + suffix), 0.0)
        return jnp.einsum('bhqk,bhkd->bhqd', weights, v)

    o = lax.map(one_block, (q_blocks, jnp.arange(n_blocks) * SB_Q_BLOCK))
    o = o.transpose(1, 0, 3, 2, 4).reshape(b, s, D_MODEL).astype(xn.dtype)
    return o @ w_o


def short_conv_mixer(xn, w_in, conv_w, w_out):
    s = xn.shape[1]
    bcu = xn @ w_in
    b_gate, c_gate, u = jnp.split(bcu, 3, axis=-1)
    h = c_gate * u
    hp = jnp.pad(h, ((0, 0), (CONV_WIDTH - 1, 0), (0, 0)))
    conv = sum(conv_w[i] * hp[:, i:i + s] for i in range(CONV_WIDTH))
    return (b_gate * conv) @ w_out


def gla_mixer(xn, w_in, w_gate_up, b_gate, head_norm, w_o):
    b, s, _ = xn.shape
    n_chunks = s // GLA_CHUNK
    proj = xn @ w_in
    q, k, v, g, a_low = jnp.split(
        proj, np.cumsum([GLA_DK, GLA_DK, GLA_DV, GLA_DV]).tolist(), axis=-1)
    log_gate = jax.nn.log_sigmoid(
        (a_low @ w_gate_up + b_gate).astype(jnp.float32)) / GLA_GATE_NORMALIZER

    def to_chunks(t, dh):
        t = t.astype(jnp.float32).reshape(b, n_chunks, GLA_CHUNK, GLA_HEADS, dh)
        return t.transpose(1, 0, 3, 2, 4)

    qc = to_chunks(q, GLA_DK_HEAD) * (GLA_DK_HEAD ** -0.5)
    kc = to_chunks(k, GLA_DK_HEAD)
    vc = to_chunks(v, GLA_DV_HEAD)
    gc = to_chunks(log_gate, GLA_DK_HEAD)
    causal = jnp.tril(jnp.ones((GLA_CHUNK, GLA_CHUNK), dtype=bool))

    def step(state, inp):
        qi, ki, vi, gi = inp
        cum = jnp.cumsum(gi, axis=2)
        inter = jnp.einsum('bhcd,bhde->bhce', qi * jnp.exp(cum), state)
        diff = cum[:, :, :, None, :] - cum[:, :, None, :, :]
        decay = jnp.exp(jnp.where(causal[None, None, :, :, None], diff, -jnp.inf))
        scores = jnp.einsum('bhid,bhjd,bhijd->bhij', qi, ki, decay)
        out = inter + jnp.einsum('bhij,bhje->bhie', scores, vi)
        last = cum[:, :, -1:, :]
        new_state = jnp.exp(last)[:, :, 0, :, None] * state + jnp.einsum(
            'bhcd,bhce->bhde', ki * jnp.exp(last - cum), vi)
        return new_state, out

    state0 = jnp.zeros((b, GLA_HEADS, GLA_DK_HEAD, GLA_DV_HEAD), jnp.float32)
    _, o = lax.scan(step, state0, (qc, kc, vc, gc))
    o = o.transpose(1, 0, 3, 2, 4).reshape(b, s, GLA_HEADS, GLA_DV_HEAD)
    o = rms_norm(o, head_norm).reshape(b, s, GLA_DV).astype(xn.dtype)
    return (o * jax.nn.silu(g)) @ w_o


def squared_relu_mlp(xn, w_up, w_down):
    return jnp.square(jax.nn.relu(xn @ w_up)) @ w_down


def setup_inputs(seed: int = 0) -> dict:
    key = jax.random.key(seed)
    ks = jax.random.split(key, 16)
    n_sb = (DEPTH + 2) // 3
    n_conv = (DEPTH + 1) // 3
    n_gla = DEPTH // 3
    f32 = jnp.float32

    def nrm(k, shape, scale):
        return jax.random.normal(k, shape, f32) * scale

    d_in_gla = 2 * GLA_DK + 2 * GLA_DV + GLA_GATE_RANK
    return {
        "x": nrm(ks[0], (BATCH, SEQ, D_MODEL), 1.0),
        "norm_gains": 1.0 + nrm(ks[1], (DEPTH, 4, D_MODEL), 0.02),
        "sb_w_qkv": nrm(ks[2], (n_sb, D_MODEL, 3 * D_MODEL), D_MODEL ** -0.5),
        "sb_w_o": nrm(ks[3], (n_sb, D_MODEL, D_MODEL), D_MODEL ** -0.5),
        "conv_w_in": nrm(ks[4], (n_conv, D_MODEL, 3 * D_MODEL), D_MODEL ** -0.5),
        "conv_w": nrm(ks[5], (n_conv, CONV_WIDTH, D_MODEL), CONV_WIDTH ** -0.5),
        "conv_w_out": nrm(ks[6], (n_conv, D_MODEL, D_MODEL), D_MODEL ** -0.5),
        "gla_w_in": nrm(ks[7], (n_gla, D_MODEL, d_in_gla), D_MODEL ** -0.5),
        "gla_w_gate_up": nrm(ks[8], (n_gla, GLA_GATE_RANK, GLA_DK), GLA_GATE_RANK ** -0.5),
        "gla_b_gate": nrm(ks[9], (n_gla, GLA_DK), 0.1),
        "gla_head_norm": 1.0 + nrm(ks[10], (n_gla, GLA_HEADS, GLA_DV_HEAD), 0.02),
        "gla_w_o": nrm(ks[11], (n_gla, GLA_DV, D_MODEL), GLA_DV ** -0.5),
        "ffn_w_up": nrm(ks[12], (DEPTH, D_MODEL, D_FF), D_MODEL ** -0.5),
        "ffn_w_down": nrm(ks[13], (DEPTH, D_FF, D_MODEL), D_FF ** -0.5),
    }


def reference(x, norm_gains, sb_w_qkv, sb_w_o, conv_w_in, conv_w, conv_w_out,
              gla_w_in, gla_w_gate_up, gla_b_gate, gla_head_norm, gla_w_o,
              ffn_w_up, ffn_w_down):
    h = x
    for i in range(DEPTH):
        kind, j = i % N_MIXERS, i // N_MIXERS
        xn = rms_norm(h, norm_gains[i, 0])
        if kind == 0:
            m = stick_breaking_mixer(xn, sb_w_qkv[j], sb_w_o[j])
        elif kind == 1:
            m = short_conv_mixer(xn, conv_w_in[j], conv_w[j], conv_w_out[j])
        else:
            m = gla_mixer(xn, gla_w_in[j], gla_w_gate_up[j], gla_b_gate[j],
                          gla_head_norm[j], gla_w_o[j])
        h = h + rms_norm(m, norm_gains[i, 1])
        f = squared_relu_mlp(rms_norm(h, norm_gains[i, 2]), ffn_w_up[i], ffn_w_down[i])
        h = h + rms_norm(f, norm_gains[i, 3])
    return h
```

```python
import functools

import numpy as np
import jax
import jax.numpy as jnp
from jax import lax
from jax.experimental import pallas as pl
from jax.experimental.pallas import tpu as pltpu

F32 = jnp.float32
BF16 = jnp.bfloat16

D_MODEL = 1024
D_FF = 4 * D_MODEL
RMS_EPS = 1e-6

SB_HEADS = 16
SB_HEAD_DIM = D_MODEL // SB_HEADS
SB_SCALE = SB_HEAD_DIM ** -0.5
CONV_WIDTH = 3
GLA_HEADS = 4
GLA_DK = D_MODEL // 2
GLA_DV = D_MODEL
GLA_DK_HEAD = GLA_DK // GLA_HEADS
GLA_DV_HEAD = GLA_DV // GLA_HEADS
GLA_GATE_RANK = 16
GLA_GATE_NORMALIZER = 16.0
GLA_SCALE = GLA_DK_HEAD ** -0.5

LANES = 128
SUBLANES = 8
VMEM_LIMIT_BYTES = 56 * 1024 * 1024

TOKEN_TILE = 512
FF_CHUNK = 1024
SB_TILE = 128
GLA_TILE = 128
GLA_LEVELS = 7


def _rms(x, gain):
    ms = jnp.mean(x * x, axis=-1, keepdims=True)
    return x * lax.rsqrt(ms + RMS_EPS) * gain


def _log_sigmoid(z):
    return jnp.minimum(z, 0.0) - jnp.log(1.0 + jnp.exp(-jnp.abs(z)))


def _split_bf16(x):
    hi = x.astype(BF16)
    lo = (x - hi.astype(F32)).astype(BF16)
    return hi, lo


def _dot(a, b):
    return jnp.dot(a, b, preferred_element_type=F32)


def _resident(shape):
    zeros = (0,) * len(shape)
    return pl.BlockSpec(shape, lambda *_: zeros, pipeline_mode=pl.Buffered(1))


def _params(semantics):
    return pltpu.CompilerParams(dimension_semantics=semantics,
                                vmem_limit_bytes=VMEM_LIMIT_BYTES)


def _sb_proj_kernel(x_ref, g_ref, w_ref, q_ref, kt_ref, v_ref):
    d = D_MODEL
    xn = _rms(x_ref[...], g_ref[...]).astype(BF16)
    q_ref[...] = (_dot(xn, w_ref[:, 0:d]) * SB_SCALE).astype(BF16)
    v_ref[...] = _dot(xn, w_ref[:, 2 * d:3 * d]).astype(BF16)
    k = _dot(xn, w_ref[:, d:2 * d])
    for hp in range(d // LANES):
        for kb in range(TOKEN_TILE // SB_TILE):
            blk = k[kb * SB_TILE:(kb + 1) * SB_TILE, hp * LANES:(hp + 1) * LANES]
            kt_ref[hp, kb] = blk.T.astype(BF16)


def _sb_proj(h, gain, w_qkv):
    b, s, d = h.shape
    n_hp = d // LANES
    n_kb = s // SB_TILE
    kb_per_step = TOKEN_TILE // SB_TILE
    return pl.pallas_call(
        _sb_proj_kernel,
        grid=(b, s // TOKEN_TILE),
        in_specs=[
            pl.BlockSpec((None, TOKEN_TILE, d), lambda bi, i: (bi, i, 0)),
            _resident((1, d)),
            _resident((d, 3 * d)),
        ],
        out_specs=[
            pl.BlockSpec((None, TOKEN_TILE, d), lambda bi, i: (bi, i, 0)),
            pl.BlockSpec((None, n_hp, kb_per_step, LANES, SB_TILE),
                         lambda bi, i: (bi, 0, i, 0, 0)),
            pl.BlockSpec((None, TOKEN_TILE, d), lambda bi, i: (bi, i, 0)),
        ],
        out_shape=[
            jax.ShapeDtypeStruct((b, s, d), BF16),
            jax.ShapeDtypeStruct((b, n_hp, n_kb, LANES, SB_TILE), BF16),
            jax.ShapeDtypeStruct((b, s, d), BF16),
        ],
        compiler_params=_params(("parallel", "parallel")),
        name="sb_proj",
    )(h, gain, w_qkv)


def _sb_suffix_matrix():
    t = SB_TILE
    m = np.arange(t)[:, None]
    j = np.arange(t)[None, :]
    half = np.concatenate([(m > j).astype(np.float32), np.ones((t, t), np.float32)], axis=1)
    return jnp.asarray(np.concatenate([half, half], axis=0), dtype=BF16)


def _sb_attn_kernel(q_ref, kt_ref, v_ref, u_ref, o_ref, acc_ref, c_ref):
    t = SB_TILE
    qi = pl.program_id(2)
    q = q_ref[...]
    lane = lax.broadcasted_iota(jnp.int32, (t, LANES), 1)
    first = lane < SB_HEAD_DIM
    zero = jnp.zeros_like(q)
    q_heads = (jnp.where(first, q, zero), jnp.where(first, zero, q))
    u = u_ref[...]
    row = lax.broadcasted_iota(jnp.int32, (t, t), 0)
    col = lax.broadcasted_iota(jnp.int32, (t, t), 1)
    causal = col < row

    def tile(qh, kblk, vblk, c, mask):
        z = _dot(qh, kblk)
        log_beta = _log_sigmoid(z)
        log_1m = log_beta - z
        if mask is not None:
            log_1m = jnp.where(mask, log_1m, 0.0)
        hi, lo = _split_bf16(log_1m)
        s2 = _dot(jnp.concatenate([hi, lo], axis=1), u)
        w = jnp.exp(log_beta + s2[:, :t] + c)
        if mask is not None:
            w = jnp.where(mask, w, 0.0)
        return _dot(w.astype(BF16), vblk), c + s2[:, t:]

    kd = kt_ref[qi]
    vd = v_ref[pl.ds(pl.multiple_of(qi * t, t), t), :]
    for hd in range(2):
        pv, c = tile(q_heads[hd], kd, vd, jnp.zeros((t, t), F32), causal)
        acc_ref[hd] = pv
        c_ref[hd] = c

    def body(step, carry):
        j = qi - 1 - step
        kblk = kt_ref[j]
        vblk = v_ref[pl.ds(pl.multiple_of(j * t, t), t), :]
        for hd in range(2):
            pv, c = tile(q_heads[hd], kblk, vblk, c_ref[hd], None)
            acc_ref[hd] += pv
            c_ref[hd] = c
        return carry

    lax.fori_loop(0, qi, body, 0)
    o_ref[...] = jnp.where(first, acc_ref[0], acc_ref[1]).astype(BF16)


def _sb_attention(q, kt, v):
    b, s, d = q.shape
    n_hp = d // LANES
    n_kb = s // SB_TILE
    t = SB_TILE
    return pl.pallas_call(
        _sb_attn_kernel,
        grid=(b, n_hp, s // t),
        in_specs=[
            pl.BlockSpec((None, t, LANES), lambda bi, hp, qi: (bi, qi, hp)),
            pl.BlockSpec((None, None, n_kb, LANES, t), lambda bi, hp, qi: (bi, hp, 0, 0, 0)),
            pl.BlockSpec((None, s, LANES), lambda bi, hp, qi: (bi, 0, hp)),
            _resident((2 * t, 2 * t)),
        ],
        out_specs=pl.BlockSpec((None, t, LANES), lambda bi, hp, qi: (bi, qi, hp)),
        out_shape=jax.ShapeDtypeStruct((b, s, d), BF16),
        scratch_shapes=[pltpu.VMEM((2, t, LANES), F32), pltpu.VMEM((2, t, t), F32)],
        compiler_params=_params(("parallel", "parallel", "arbitrary")),
        name="sb_attn",
    )(q, kt, v, _sb_suffix_matrix())


def _conv_kernel(x_ref, xp_ref, g_ref, w_ref, cw_ref, a_ref, hbuf_ref):
    d = D_MODEL
    tm = TOKEN_TILE
    i = pl.program_id(1)
    gain = g_ref[...]
    xn = _rms(x_ref[...], gain).astype(BF16)
    xpn = _rms(xp_ref[...], gain).astype(BF16)
    hbuf_ref[SUBLANES:, :] = _dot(xn, w_ref[:, d:2 * d]) * _dot(xn, w_ref[:, 2 * d:3 * d])
    h_prev = _dot(xpn, w_ref[:, d:2 * d]) * _dot(xpn, w_ref[:, 2 * d:3 * d])
    hbuf_ref[0:SUBLANES, :] = jnp.where(i > 0, h_prev, 0.0)
    conv = cw_ref[0:1, :] * hbuf_ref[pl.ds(SUBLANES - 2, tm), :]
    conv = conv + cw_ref[1:2, :] * hbuf_ref[pl.ds(SUBLANES - 1, tm), :]
    conv = conv + cw_ref[2:3, :] * hbuf_ref[pl.ds(SUBLANES, tm), :]
    a_ref[...] = (_dot(xn, w_ref[:, 0:d]) * conv).astype(BF16)


def _conv_mixer(h, gain, w_in, conv_w):
    b, s, d = h.shape
    tm = TOKEN_TILE
    rows_per_tile = tm // SUBLANES
    return pl.pallas_call(
        _conv_kernel,
        grid=(b, s // tm),
        in_specs=[
            pl.BlockSpec((None, tm, d), lambda bi, i: (bi, i, 0)),
            pl.BlockSpec((None, SUBLANES, d),
                         lambda bi, i: (bi, jnp.maximum(i * rows_per_tile - 1, 0), 0)),
            _resident((1, d)),
            _resident((d, 3 * d)),
            _resident((CONV_WIDTH, d)),
        ],
        out_specs=pl.BlockSpec((None, tm, d), lambda bi, i: (bi, i, 0)),
        out_shape=jax.ShapeDtypeStruct((b, s, d), BF16),
        scratch_shapes=[pltpu.VMEM((tm + SUBLANES, d), F32)],
        compiler_params=_params(("parallel", "parallel")),
        name="conv_mixer",
    )(h, h, gain, w_in, conv_w)


def _gla_proj_kernel(x_ref, g_ref, w_ref, wa_ref, wgu_ref, bg_ref,
                     q_ref, kt_ref, v_ref, og_ref, lg_ref, lgt_ref):
    dk, dv = GLA_DK, GLA_DV
    xn = _rms(x_ref[...], g_ref[...]).astype(BF16)
    q_ref[...] = _dot(xn, w_ref[:, 0:dk]).astype(BF16)
    kt_ref[...] = _dot(xn, w_ref[:, dk:2 * dk]).T.astype(BF16)
    v_ref[...] = _dot(xn, w_ref[:, 2 * dk:2 * dk + dv]).astype(BF16)
    og_ref[...] = _dot(xn, w_ref[:, 2 * dk + dv:2 * dk + 2 * dv]).astype(BF16)
    a_low = _dot(xn, wa_ref[...]).astype(BF16)
    pre = _dot(a_low, wgu_ref[...]) + bg_ref[...]
    lg = _log_sigmoid(pre) * (1.0 / GLA_GATE_NORMALIZER)
    lg_ref[...] = lg
    lgt_ref[...] = lg.T


def _gla_proj(h, gain, w_main, w_a, w_gu, b_gate):
    b, s, d = h.shape
    tm = TOKEN_TILE
    dk, dv = GLA_DK, GLA_DV
    row = lambda bi, i: (bi, i, 0)
    colmajor = lambda bi, i: (bi, 0, i)
    return pl.pallas_call(
        _gla_proj_kernel,
        grid=(b, s // tm),
        in_specs=[
            pl.BlockSpec((None, tm, d), row),
            _resident((1, d)),
            _resident((d, 2 * dk + 2 * dv)),
            _resident((d, LANES)),
            _resident((LANES, dk)),
            _resident((1, dk)),
        ],
        out_specs=[
            pl.BlockSpec((None, tm, dk), row),
            pl.BlockSpec((None, dk, tm), colmajor),
            pl.BlockSpec((None, tm, dv), row),
            pl.BlockSpec((None, tm, dv), row),
            pl.BlockSpec((None, tm, dk), row),
            pl.BlockSpec((None, dk, tm), colmajor),
        ],
        out_shape=[
            jax.ShapeDtypeStruct((b, s, dk), BF16),
            jax.ShapeDtypeStruct((b, dk, s), BF16),
            jax.ShapeDtypeStruct((b, s, dv), BF16),
            jax.ShapeDtypeStruct((b, s, dv), BF16),
            jax.ShapeDtypeStruct((b, s, dk), F32),
            jax.ShapeDtypeStruct((b, dk, s), F32),
        ],
        compiler_params=_params(("parallel", "parallel")),
        name="gla_proj",
    )(h, gain, w_main, w_a, w_gu, b_gate)


def _gla_constants():
    c = GLA_TILE
    i = np.arange(c)[:, None]
    m = np.arange(c)[None, :]
    fwd, rev, masks = [], [], []
    for level in range(GLA_LEVELS + 1):
        n = 1 << level
        same = (i // n) == (m // n)
        fwd.append((same & (m <= i)).astype(np.float32))
        rev.append((same & (m > i)).astype(np.float32).T)
    rev.append(np.ones((c, c), np.float32))
    for level in range(GLA_LEVELS):
        n = 1 << level
        masks.append(((i // (2 * n) == m // (2 * n)) & ((i // n) % 2 == 1)
                      & ((m // n) % 2 == 0)).astype(np.float32))
    masks.append((i == m).astype(np.float32))
    fwd = np.concatenate(fwd, axis=0)
    rev = np.concatenate(rev, axis=1)
    fwd2 = np.concatenate([fwd, fwd], axis=1)
    rev2 = np.concatenate([rev, rev], axis=0)
    return (jnp.asarray(fwd2, dtype=BF16), jnp.asarray(rev2, dtype=BF16),
            jnp.asarray(np.stack(masks), dtype=F32))


def _gla_chunk_kernel(q_ref, kt_ref, v_ref, og_ref, lg_ref, lgt_ref, fwd_ref, rev_ref,
                      mask_ref, hn_ref, a_ref, state_ref):
    c = GLA_TILE
    dkh, dvh = GLA_DK_HEAD, GLA_DV_HEAD
    n_lv = GLA_LEVELS

    @pl.when(pl.program_id(1) == 0)
    def _():
        state_ref[...] = jnp.zeros_like(state_ref)

    hi, lo = _split_bf16(lg_ref[...])
    fwd_all = _dot(fwd_ref[...], jnp.concatenate([hi, lo], axis=0))
    hi_t, lo_t = _split_bf16(lgt_ref[...])
    rev_all = _dot(jnp.concatenate([hi_t, lo_t], axis=1), rev_ref[...])

    for hd in range(GLA_HEADS):
        ks = slice(hd * dkh, (hd + 1) * dkh)
        vs = slice(hd * dvh, (hd + 1) * dvh)
        q = q_ref[:, ks].astype(F32) * GLA_SCALE
        kt = kt_ref[ks, :].astype(F32)
        v = v_ref[:, vs]
        scores = _dot(q.astype(BF16), kt.astype(BF16)) * mask_ref[n_lv]
        for lv in range(n_lv):
            qd = q * jnp.exp(fwd_all[lv * c:(lv + 1) * c, ks])
            kd = kt * jnp.exp(rev_all[ks, lv * c:(lv + 1) * c])
            scores = scores + _dot(qd.astype(BF16), kd.astype(BF16)) * mask_ref[lv]
        state = state_ref[hd]
        q_cum = q * jnp.exp(fwd_all[n_lv * c:(n_lv + 1) * c, ks])
        out = _dot(q_cum.astype(BF16), state.astype(BF16)) + _dot(scores.astype(BF16), v)
        k_rest = kt * jnp.exp(rev_all[ks, n_lv * c:(n_lv + 1) * c])
        keep = jnp.exp(rev_all[ks, (n_lv + 1) * c:(n_lv + 2) * c])
        keep = jnp.concatenate([keep] * (dvh // c), axis=1)
        state_ref[hd] = keep * state + _dot(k_rest.astype(BF16), v)
        o = _rms(out, hn_ref[:, vs])
        og = og_ref[:, vs].astype(F32)
        a_ref[:, vs] = (o * (og * (1.0 / (1.0 + jnp.exp(-og))))).astype(BF16)


def _gla_chunk(q, kt, v, og, lg, lgt, head_norm):
    b, s, dk = q.shape
    dv = v.shape[-1]
    c = GLA_TILE
    fwd, rev, masks = _gla_constants()
    row = lambda bi, i: (bi, i, 0)
    colmajor = lambda bi, i: (bi, 0, i)
    return pl.pallas_call(
        _gla_chunk_kernel,
        grid=(b, s // c),
        in_specs=[
            pl.BlockSpec((None, c, dk), row),
            pl.BlockSpec((None, dk, c), colmajor),
            pl.BlockSpec((None, c, dv), row),
            pl.BlockSpec((None, c, dv), row),
            pl.BlockSpec((None, c, dk), row),
            pl.BlockSpec((None, dk, c), colmajor),
            _resident(fwd.shape),
            _resident(rev.shape),
            _resident(masks.shape),
            _resident((1, dv)),
        ],
        out_specs=pl.BlockSpec((None, c, dv), row),
        out_shape=jax.ShapeDtypeStruct((b, s, dv), BF16),
        scratch_shapes=[pltpu.VMEM((GLA_HEADS, GLA_DK_HEAD, GLA_DV_HEAD), F32)],
        compiler_params=_params(("parallel", "arbitrary")),
        name="gla_chunk",
    )(q, kt, v, og, lg, lgt, fwd, rev, masks, head_norm)


def _post_kernel(a_ref, h_ref, wo_ref, g_ref, wup_ref, wdn_ref, o_ref, acc_ref):
    mixed = _dot(a_ref[...], wo_ref[...])
    h1 = h_ref[...] + _rms(mixed, g_ref[1:2, :])
    xn = _rms(h1, g_ref[2:3, :]).astype(BF16)
    for ci in range(D_FF // FF_CHUNK):
        cols = slice(ci * FF_CHUNK, (ci + 1) * FF_CHUNK)
        up = jnp.maximum(_dot(xn, wup_ref[:, cols]), 0.0)
        down = _dot((up * up).astype(BF16), wdn_ref[cols, :])
        if ci == 0:
            acc_ref[...] = down
        else:
            acc_ref[...] += down
    o_ref[...] = h1 + _rms(acc_ref[...], g_ref[3:4, :])


def _post(a, h, w_o, gains, w_up, w_down):
    b, s, d = h.shape
    tm = TOKEN_TILE
    row = lambda bi, i: (bi, i, 0)
    return pl.pallas_call(
        _post_kernel,
        grid=(b, s // tm),
        in_specs=[
            pl.BlockSpec((None, tm, d), row),
            pl.BlockSpec((None, tm, d), row),
            _resident((d, d)),
            _resident((4, d)),
            _resident((d, D_FF)),
            _resident((D_FF, d)),
        ],
        out_specs=pl.BlockSpec((None, tm, d), row),
        out_shape=jax.ShapeDtypeStruct((b, s, d), F32),
        scratch_shapes=[pltpu.VMEM((tm, d), F32)],
        compiler_params=_params(("parallel", "parallel")),
        name="post_ffn",
    )(a, h, w_o, gains, w_up, w_down)


def kernel(x, norm_gains, sb_w_qkv, sb_w_o, conv_w_in, conv_w, conv_w_out, gla_w_in,
           gla_w_gate_up, gla_b_gate, gla_head_norm, gla_w_o, ffn_w_up, ffn_w_down):
    depth = norm_gains.shape[0]
    h = x
    for i in range(depth):
        kind, j = i % 3, i // 3
        gains = norm_gains[i]
        pre_gain = gains[0:1]
        if kind == 0:
            q, kt, v = _sb_proj(h, pre_gain, sb_w_qkv[j].astype(BF16))
            a = _sb_attention(q, kt, v)
            w_o = sb_w_o[j]
        elif kind == 1:
            a = _conv_mixer(h, pre_gain, conv_w_in[j].astype(BF16), conv_w[j])
            w_o = conv_w_out[j]
        else:
            n_main = 2 * GLA_DK + 2 * GLA_DV
            w_in = gla_w_in[j]
            w_a = jnp.pad(w_in[:, n_main:], ((0, 0), (0, LANES - GLA_GATE_RANK))).astype(BF16)
            w_gu = jnp.pad(gla_w_gate_up[j], ((0, LANES - GLA_GATE_RANK), (0, 0))).astype(BF16)
            q, kt, v, og, lg, lgt = _gla_proj(h, pre_gain, w_in[:, :n_main].astype(BF16), w_a,
                                              w_gu, gla_b_gate[j][None, :])
            a = _gla_chunk(q, kt, v, og, lg, lgt, gla_head_norm[j].reshape(1, GLA_DV))
            w_o = gla_w_o[j]
        h = _post(a, h, w_o.astype(BF16), gains, ffn_w_up[i].astype(BF16),
                  ffn_w_down[i].astype(BF16))
    return h
```

```python
import functools

import numpy as np
import jax
import jax.numpy as jnp
from jax import lax
from jax.experimental import pallas as pl
from jax.experimental.pallas import tpu as pltpu

F32 = jnp.float32
BF16 = jnp.bfloat16

D_MODEL = 1024
D_FF = 4 * D_MODEL
RMS_EPS = 1e-6

SB_HEADS = 16
SB_HEAD_DIM = D_MODEL // SB_HEADS
SB_SCALE = SB_HEAD_DIM ** -0.5
CONV_WIDTH = 3
GLA_HEADS = 4
GLA_DK = D_MODEL // 2
GLA_DV = D_MODEL
GLA_DK_HEAD = GLA_DK // GLA_HEADS
GLA_DV_HEAD = GLA_DV // GLA_HEADS
GLA_GATE_RANK = 16
GLA_GATE_NORMALIZER = 16.0
GLA_SCALE = GLA_DK_HEAD ** -0.5

LANES = 128
SUBLANES = 8
VMEM_LIMIT_BYTES = 56 * 1024 * 1024

TOKEN_TILE = 512
FF_CHUNK = 1024
SB_TILE = 512
SB_SUB = 128
GLA_TILE = 128
GLA_LEVELS = 7


def _rms(x, gain):
    ms = jnp.mean(x * x, axis=-1, keepdims=True)
    return x * lax.rsqrt(ms + RMS_EPS) * gain


def _log_sigmoid(z):
    return jnp.minimum(z, 0.0) - jnp.log(1.0 + jnp.exp(-jnp.abs(z)))


def _split_bf16(x):
    hi = x.astype(BF16)
    lo = (x - hi.astype(F32)).astype(BF16)
    return hi, lo


def _dot(a, b):
    return jnp.dot(a, b, preferred_element_type=F32)


def _resident(shape):
    zeros = (0,) * len(shape)
    return pl.BlockSpec(shape, lambda *_: zeros, pipeline_mode=pl.Buffered(1))


def _params(semantics):
    return pltpu.CompilerParams(dimension_semantics=semantics,
                                vmem_limit_bytes=VMEM_LIMIT_BYTES)


def _sb_proj_kernel(x_ref, g_ref, w_ref, q_ref, kt_ref, v_ref):
    d = D_MODEL
    xn = _rms(x_ref[...], g_ref[...]).astype(BF16)
    q_ref[...] = (_dot(xn, w_ref[:, 0:d]) * SB_SCALE).astype(BF16)
    v_ref[...] = _dot(xn, w_ref[:, 2 * d:3 * d]).astype(BF16)
    k = _dot(xn, w_ref[:, d:2 * d])
    for hp in range(d // LANES):
        for kb in range(TOKEN_TILE // SB_TILE):
            blk = k[kb * SB_TILE:(kb + 1) * SB_TILE, hp * LANES:(hp + 1) * LANES]
            kt_ref[hp, kb] = blk.T.astype(BF16)


def _sb_proj(h, gain, w_qkv):
    b, s, d = h.shape
    n_hp = d // LANES
    n_kb = s // SB_TILE
    kb_per_step = TOKEN_TILE // SB_TILE
    return pl.pallas_call(
        _sb_proj_kernel,
        grid=(b, s // TOKEN_TILE),
        in_specs=[
            pl.BlockSpec((None, TOKEN_TILE, d), lambda bi, i: (bi, i, 0)),
            _resident((1, d)),
            _resident((d, 3 * d)),
        ],
        out_specs=[
            pl.BlockSpec((None, TOKEN_TILE, d), lambda bi, i: (bi, i, 0)),
            pl.BlockSpec((None, n_hp, kb_per_step, LANES, SB_TILE),
                         lambda bi, i: (bi, 0, i, 0, 0)),
            pl.BlockSpec((None, TOKEN_TILE, d), lambda bi, i: (bi, i, 0)),
        ],
        out_shape=[
            jax.ShapeDtypeStruct((b, s, d), BF16),
            jax.ShapeDtypeStruct((b, n_hp, n_kb, LANES, SB_TILE), BF16),
            jax.ShapeDtypeStruct((b, s, d), BF16),
        ],
        compiler_params=_params(("parallel", "parallel")),
        name="sb_proj",
    )(h, gain, w_qkv)


def _sb_suffix_matrix():
    t = SB_SUB
    m = np.arange(t)[:, None]
    j = np.arange(t)[None, :]
    half = np.concatenate([(m > j).astype(np.float32), np.ones((t, t), np.float32)], axis=1)
    return jnp.asarray(np.concatenate([half, half], axis=0), dtype=BF16)


def _sb_attn_kernel(q_ref, kt_ref, v_ref, u_ref, o_ref, acc_ref, c_ref):
    t = SB_TILE
    sub = SB_SUB
    n_sub = t // sub
    qi = pl.program_id(2)
    q = q_ref[...]
    lane = lax.broadcasted_iota(jnp.int32, (t, LANES), 1)
    first = lane < SB_HEAD_DIM
    zero = jnp.zeros_like(q)
    q_heads = (jnp.where(first, q, zero), jnp.where(first, zero, q))
    u = u_ref[...]

    def chunk(qh, kblk, vblk, c, mask):
        z = _dot(qh, kblk)
        log_beta = _log_sigmoid(z)
        log_1m = log_beta - z
        if mask is not None:
            log_1m = jnp.where(mask, log_1m, 0.0)
        hi, lo = _split_bf16(log_1m)
        ws = [None] * n_sub
        for s in reversed(range(n_sub)):
            cols = slice(s * sub, (s + 1) * sub)
            s2 = _dot(jnp.concatenate([hi[:, cols], lo[:, cols]], axis=1), u)
            ws[s] = jnp.exp(log_beta[:, cols] + s2[:, :sub] + c)
            c = c + s2[:, sub:]
        w = jnp.concatenate(ws, axis=1)
        if mask is not None:
            w = jnp.where(mask, w, 0.0)
        return _dot(w.astype(BF16), vblk), c

    row = lax.broadcasted_iota(jnp.int32, (t, t), 0)
    col = lax.broadcasted_iota(jnp.int32, (t, t), 1)
    kd = kt_ref[qi]
    vd = v_ref[pl.ds(pl.multiple_of(qi * t, t), t), :]
    for hd in range(2):
        pv, c = chunk(q_heads[hd], kd, vd, jnp.zeros((t, sub), F32), col < row)
        acc_ref[hd] = pv
        c_ref[hd] = c

    def body(step, carry):
        j = qi - 1 - step
        kblk = kt_ref[j]
        vblk = v_ref[pl.ds(pl.multiple_of(j * t, t), t), :]
        for hd in range(2):
            pv, c = chunk(q_heads[hd], kblk, vblk, c_ref[hd], None)
            acc_ref[hd] += pv
            c_ref[hd] = c
        return carry

    lax.fori_loop(0, qi, body, 0)
    o_ref[...] = jnp.where(first, acc_ref[0], acc_ref[1]).astype(BF16)


def _sb_attention(q, kt, v):
    b, s, d = q.shape
    n_hp = d // LANES
    n_kb = s // SB_TILE
    t = SB_TILE
    return pl.pallas_call(
        _sb_attn_kernel,
        grid=(b, n_hp, s // t),
        in_specs=[
            pl.BlockSpec((None, t, LANES), lambda bi, hp, qi: (bi, qi, hp)),
            pl.BlockSpec((None, None, n_kb, LANES, t), lambda bi, hp, qi: (bi, hp, 0, 0, 0)),
            pl.BlockSpec((None, s, LANES), lambda bi, hp, qi: (bi, 0, hp)),
            _resident((2 * SB_SUB, 2 * SB_SUB)),
        ],
        out_specs=pl.BlockSpec((None, t, LANES), lambda bi, hp, qi: (bi, qi, hp)),
        out_shape=jax.ShapeDtypeStruct((b, s, d), BF16),
        scratch_shapes=[pltpu.VMEM((2, t, LANES), F32), pltpu.VMEM((2, t, SB_SUB), F32)],
        compiler_params=_params(("parallel", "parallel", "arbitrary")),
        name="sb_attn",
    )(q, kt, v, _sb_suffix_matrix())


def _conv_kernel(x_ref, xp_ref, g_ref, w_ref, cw_ref, a_ref, hbuf_ref):
    d = D_MODEL
    tm = TOKEN_TILE
    i = pl.program_id(1)
    gain = g_ref[...]
    xn = _rms(x_ref[...], gain).astype(BF16)
    xpn = _rms(xp_ref[...], gain).astype(BF16)
    hbuf_ref[SUBLANES:, :] = _dot(xn, w_ref[:, d:2 * d]) * _dot(xn, w_ref[:, 2 * d:3 * d])
    h_prev = _dot(xpn, w_ref[:, d:2 * d]) * _dot(xpn, w_ref[:, 2 * d:3 * d])
    hbuf_ref[0:SUBLANES, :] = jnp.where(i > 0, h_prev, 0.0)
    conv = cw_ref[0:1, :] * hbuf_ref[pl.ds(SUBLANES - 2, tm), :]
    conv = conv + cw_ref[1:2, :] * hbuf_ref[pl.ds(SUBLANES - 1, tm), :]
    conv = conv + cw_ref[2:3, :] * hbuf_ref[pl.ds(SUBLANES, tm), :]
    a_ref[...] = (_dot(xn, w_ref[:, 0:d]) * conv).astype(BF16)


def _conv_mixer(h, gain, w_in, conv_w):
    b, s, d = h.shape
    tm = TOKEN_TILE
    rows_per_tile = tm // SUBLANES
    return pl.pallas_call(
        _conv_kernel,
        grid=(b, s // tm),
        in_specs=[
            pl.BlockSpec((None, tm, d), lambda bi, i: (bi, i, 0)),
            pl.BlockSpec((None, SUBLANES, d),
                         lambda bi, i: (bi, jnp.maximum(i * rows_per_tile - 1, 0), 0)),
            _resident((1, d)),
            _resident((d, 3 * d)),
            _resident((CONV_WIDTH, d)),
        ],
        out_specs=pl.BlockSpec((None, tm, d), lambda bi, i: (bi, i, 0)),
        out_shape=jax.ShapeDtypeStruct((b, s, d), BF16),
        scratch_shapes=[pltpu.VMEM((tm + SUBLANES, d), F32)],
        compiler_params=_params(("parallel", "parallel")),
        name="conv_mixer",
    )(h, h, gain, w_in, conv_w)


def _gla_proj_kernel(x_ref, g_ref, w_ref, wa_ref, wgu_ref, bg_ref,
                     q_ref, kt_ref, v_ref, og_ref, lg_ref, lgt_ref):
    dk, dv = GLA_DK, GLA_DV
    xn = _rms(x_ref[...], g_ref[...]).astype(BF16)
    q_ref[...] = _dot(xn, w_ref[:, 0:dk]).astype(BF16)
    kt_ref[...] = _dot(xn, w_ref[:, dk:2 * dk]).T.astype(BF16)
    v_ref[...] = _dot(xn, w_ref[:, 2 * dk:2 * dk + dv]).astype(BF16)
    og_ref[...] = _dot(xn, w_ref[:, 2 * dk + dv:2 * dk + 2 * dv]).astype(BF16)
    a_low = _dot(xn, wa_ref[...]).astype(BF16)
    pre = _dot(a_low, wgu_ref[...]) + bg_ref[...]
    lg = _log_sigmoid(pre) * (1.0 / GLA_GATE_NORMALIZER)
    lg_ref[...] = lg
    lgt_ref[...] = lg.T


def _gla_proj(h, gain, w_main, w_a, w_gu, b_gate):
    b, s, d = h.shape
    tm = TOKEN_TILE
    dk, dv = GLA_DK, GLA_DV
    row = lambda bi, i: (bi, i, 0)
    colmajor = lambda bi, i: (bi, 0, i)
    return pl.pallas_call(
        _gla_proj_kernel,
        grid=(b, s // tm),
        in_specs=[
            pl.BlockSpec((None, tm, d), row),
            _resident((1, d)),
            _resident((d, 2 * dk + 2 * dv)),
            _resident((d, LANES)),
            _resident((LANES, dk)),
            _resident((1, dk)),
        ],
        out_specs=[
            pl.BlockSpec((None, tm, dk), row),
            pl.BlockSpec((None, dk, tm), colmajor),
            pl.BlockSpec((None, tm, dv), row),
            pl.BlockSpec((None, tm, dv), row),
            pl.BlockSpec((None, tm, dk), row),
            pl.BlockSpec((None, dk, tm), colmajor),
        ],
        out_shape=[
            jax.ShapeDtypeStruct((b, s, dk), BF16),
            jax.ShapeDtypeStruct((b, dk, s), BF16),
            jax.ShapeDtypeStruct((b, s, dv), BF16),
            jax.ShapeDtypeStruct((b, s, dv), BF16),
            jax.ShapeDtypeStruct((b, s, dk), F32),
            jax.ShapeDtypeStruct((b, dk, s), F32),
        ],
        compiler_params=_params(("parallel", "parallel")),
        name="gla_proj",
    )(h, gain, w_main, w_a, w_gu, b_gate)


def _gla_constants():
    c = GLA_TILE
    i = np.arange(c)[:, None]
    m = np.arange(c)[None, :]
    fwd, rev, masks = [], [], []
    for level in range(GLA_LEVELS + 1):
        n = 1 << level
        same = (i // n) == (m // n)
        fwd.append((same & (m <= i)).astype(np.float32))
        rev.append((same & (m > i)).astype(np.float32).T)
    rev.append(np.ones((c, c), np.float32))
    for level in range(GLA_LEVELS):
        n = 1 << level
        masks.append(((i // (2 * n) == m // (2 * n)) & ((i // n) % 2 == 1)
                      & ((m // n) % 2 == 0)).astype(np.float32))
    masks.append((i == m).astype(np.float32))
    fwd = np.concatenate(fwd, axis=0)
    rev = np.concatenate(rev, axis=1)
    fwd2 = np.concatenate([fwd, fwd], axis=1)
    rev2 = np.concatenate([rev, rev], axis=0)
    return (jnp.asarray(fwd2, dtype=BF16), jnp.asarray(rev2, dtype=BF16),
            jnp.asarray(np.stack(masks), dtype=F32))


def _gla_chunk_kernel(q_ref, kt_ref, v_ref, og_ref, lg_ref, lgt_ref, fwd_ref, rev_ref,
                      mask_ref, hn_ref, a_ref, state_ref):
    c = GLA_TILE
    dkh, dvh = GLA_DK_HEAD, GLA_DV_HEAD
    n_lv = GLA_LEVELS

    @pl.when(pl.program_id(1) == 0)
    def _():
        state_ref[...] = jnp.zeros_like(state_ref)

    hi, lo = _split_bf16(lg_ref[...])
    fwd_all = _dot(fwd_ref[...], jnp.concatenate([hi, lo], axis=0))
    hi_t, lo_t = _split_bf16(lgt_ref[...])
    rev_all = _dot(jnp.concatenate([hi_t, lo_t], axis=1), rev_ref[...])

    for hd in range(GLA_HEADS):
        ks = slice(hd * dkh, (hd + 1) * dkh)
        vs = slice(hd * dvh, (hd + 1) * dvh)
        q = q_ref[:, ks].astype(F32) * GLA_SCALE
        kt = kt_ref[ks, :].astype(F32)
        v = v_ref[:, vs]
        scores = _dot(q.astype(BF16), kt.astype(BF16)) * mask_ref[n_lv]
        for lv in range(n_lv):
            qd = q * jnp.exp(fwd_all[lv * c:(lv + 1) * c, ks])
            kd = kt * jnp.exp(rev_all[ks, lv * c:(lv + 1) * c])
            scores = scores + _dot(qd.astype(BF16), kd.astype(BF16)) * mask_ref[lv]
        state = state_ref[hd]
        q_cum = q * jnp.exp(fwd_all[n_lv * c:(n_lv + 1) * c, ks])
        out = _dot(q_cum.astype(BF16), state.astype(BF16)) + _dot(scores.astype(BF16), v)
        k_rest = kt * jnp.exp(rev_all[ks, n_lv * c:(n_lv + 1) * c])
        keep = jnp.exp(rev_all[ks, (n_lv + 1) * c:(n_lv + 2) * c])
        keep = jnp.concatenate([keep] * (dvh // c), axis=1)
        state_ref[hd] = keep * state + _dot(k_rest.astype(BF16), v)
        o = _rms(out, hn_ref[:, vs])
        og = og_ref[:, vs].astype(F32)
        a_ref[:, vs] = (o * (og * (1.0 / (1.0 + jnp.exp(-og))))).astype(BF16)


def _gla_chunk(q, kt, v, og, lg, lgt, head_norm):
    b, s, dk = q.shape
    dv = v.shape[-1]
    c = GLA_TILE
    fwd, rev, masks = _gla_constants()
    row = lambda bi, i: (bi, i, 0)
    colmajor = lambda bi, i: (bi, 0, i)
    return pl.pallas_call(
        _gla_chunk_kernel,
        grid=(b, s // c),
        in_specs=[
            pl.BlockSpec((None, c, dk), row),
            pl.BlockSpec((None, dk, c), colmajor),
            pl.BlockSpec((None, c, dv), row),
            pl.BlockSpec((None, c, dv), row),
            pl.BlockSpec((None, c, dk), row),
            pl.BlockSpec((None, dk, c), colmajor),
            _resident(fwd.shape),
            _resident(rev.shape),
            _resident(masks.shape),
            _resident((1, dv)),
        ],
        out_specs=pl.BlockSpec((None, c, dv), row),
        out_shape=jax.ShapeDtypeStruct((b, s, dv), BF16),
        scratch_shapes=[pltpu.VMEM((GLA_HEADS, GLA_DK_HEAD, GLA_DV_HEAD), F32)],
        compiler_params=_params(("parallel", "arbitrary")),
        name="gla_chunk",
    )(q, kt, v, og, lg, lgt, fwd, rev, masks, head_norm)


def _post_kernel(a_ref, h_ref, wo_ref, g_ref, wup_ref, wdn_ref, o_ref, acc_ref):
    mixed = _dot(a_ref[...], wo_ref[...])
    h1 = h_ref[...] + _rms(mixed, g_ref[1:2, :])
    xn = _rms(h1, g_ref[2:3, :]).astype(BF16)
    for ci in range(D_FF // FF_CHUNK):
        cols = slice(ci * FF_CHUNK, (ci + 1) * FF_CHUNK)
        up = jnp.maximum(_dot(xn, wup_ref[:, cols]), 0.0)
        down = _dot((up * up).astype(BF16), wdn_ref[cols, :])
        if ci == 0:
            acc_ref[...] = down
        else:
            acc_ref[...] += down
    o_ref[...] = h1 + _rms(acc_ref[...], g_ref[3:4, :])


def _post(a, h, w_o, gains, w_up, w_down):
    b, s, d = h.shape
    tm = TOKEN_TILE
    row = lambda bi, i: (bi, i, 0)
    return pl.pallas_call(
        _post_kernel,
        grid=(b, s // tm),
        in_specs=[
            pl.BlockSpec((None, tm, d), row),
            pl.BlockSpec((None, tm, d), row),
            _resident((d, d)),
            _resident((4, d)),
            _resident((d, D_FF)),
            _resident((D_FF, d)),
        ],
        out_specs=pl.BlockSpec((None, tm, d), row),
        out_shape=jax.ShapeDtypeStruct((b, s, d), F32),
        scratch_shapes=[pltpu.VMEM((tm, d), F32)],
        compiler_params=_params(("parallel", "parallel")),
        name="post_ffn",
    )(a, h, w_o, gains, w_up, w_down)


def kernel(x, norm_gains, sb_w_qkv, sb_w_o, conv_w_in, conv_w, conv_w_out, gla_w_in,
           gla_w_gate_up, gla_b_gate, gla_head_norm, gla_w_o, ffn_w_up, ffn_w_down):
    depth = norm_gains.shape[0]
    h = x
    for i in range(depth):
        kind, j = i % 3, i // 3
        gains = norm_gains[i]
        pre_gain = gains[0:1]
        if kind == 0:
            q, kt, v = _sb_proj(h, pre_gain, sb_w_qkv[j].astype(BF16))
            a = _sb_attention(q, kt, v)
            w_o = sb_w_o[j]
        elif kind == 1:
            a = _conv_mixer(h, pre_gain, conv_w_in[j].astype(BF16), conv_w[j])
            w_o = conv_w_out[j]
        else:
            n_main = 2 * GLA_DK + 2 * GLA_DV
            w_in = gla_w_in[j]
            w_a = jnp.pad(w_in[:, n_main:], ((0, 0), (0, LANES - GLA_GATE_RANK))).astype(BF16)
            w_gu = jnp.pad(gla_w_gate_up[j], ((0, LANES - GLA_GATE_RANK), (0, 0))).astype(BF16)
            q, kt, v, og, lg, lgt = _gla_proj(h, pre_gain, w_in[:, :n_main].astype(BF16), w_a,
                                              w_gu, gla_b_gate[j][None, :])
            a = _gla_chunk(q, kt, v, og, lg, lgt, gla_head_norm[j].reshape(1, GLA_DV))
            w_o = gla_w_o[j]
        h = _post(a, h, w_o.astype(BF16), gains, ffn_w_up[i].astype(BF16),
                  ffn_w_down[i].astype(BF16))
    return h
```

```python
import functools

import numpy as np
import jax
import jax.numpy as jnp
from jax import lax
from jax.experimental import pallas as pl
from jax.experimental.pallas import tpu as pltpu

F32 = jnp.float32
BF16 = jnp.bfloat16

D_MODEL = 1024
D_FF = 4 * D_MODEL
RMS_EPS = 1e-6

SB_HEADS = 16
SB_HEAD_DIM = D_MODEL // SB_HEADS
SB_SCALE = SB_HEAD_DIM ** -0.5
CONV_WIDTH = 3
GLA_HEADS = 4
GLA_DK = D_MODEL // 2
GLA_DV = D_MODEL
GLA_DK_HEAD = GLA_DK // GLA_HEADS
GLA_DV_HEAD = GLA_DV // GLA_HEADS
GLA_GATE_RANK = 16
GLA_GATE_NORMALIZER = 16.0
GLA_SCALE = GLA_DK_HEAD ** -0.5

LANES = 128
SUBLANES = 8
VMEM_LIMIT_BYTES = 56 * 1024 * 1024

TOKEN_TILE = 512
FF_CHUNK = 1024
SB_TILE = 512
SB_KCHUNK = 256
SB_SUB = 128
SB_F32_EXP_ZERO = -105.0
GLA_TILE = 128
GLA_LEVELS = 7


def _rms(x, gain):
    ms = jnp.mean(x * x, axis=-1, keepdims=True)
    return x * lax.rsqrt(ms + RMS_EPS) * gain


def _log_sigmoid(z):
    return jnp.minimum(z, 0.0) - jnp.log(1.0 + jnp.exp(-jnp.abs(z)))


def _split_bf16(x):
    hi = x.astype(BF16)
    lo = (x - hi.astype(F32)).astype(BF16)
    return hi, lo


def _dot(a, b):
    return jnp.dot(a, b, preferred_element_type=F32)


def _resident(shape):
    zeros = (0,) * len(shape)
    return pl.BlockSpec(shape, lambda *_: zeros, pipeline_mode=pl.Buffered(1))


def _params(semantics):
    return pltpu.CompilerParams(dimension_semantics=semantics,
                                vmem_limit_bytes=VMEM_LIMIT_BYTES)


def _sb_proj_kernel(x_ref, g_ref, w_ref, q_ref, kt_ref, v_ref):
    d = D_MODEL
    xn = _rms(x_ref[...], g_ref[...]).astype(BF16)
    q_ref[...] = (_dot(xn, w_ref[:, 0:d]) * SB_SCALE).astype(BF16)
    v_ref[...] = _dot(xn, w_ref[:, 2 * d:3 * d]).astype(BF16)
    k = _dot(xn, w_ref[:, d:2 * d])
    for hp in range(d // LANES):
        for kb in range(TOKEN_TILE // SB_KCHUNK):
            blk = k[kb * SB_KCHUNK:(kb + 1) * SB_KCHUNK, hp * LANES:(hp + 1) * LANES]
            kt_ref[hp, kb] = blk.T.astype(BF16)


def _sb_proj(h, gain, w_qkv):
    b, s, d = h.shape
    n_hp = d // LANES
    n_kb = s // SB_KCHUNK
    kb_per_step = TOKEN_TILE // SB_KCHUNK
    return pl.pallas_call(
        _sb_proj_kernel,
        grid=(b, s // TOKEN_TILE),
        in_specs=[
            pl.BlockSpec((None, TOKEN_TILE, d), lambda bi, i: (bi, i, 0)),
            _resident((1, d)),
            _resident((d, 3 * d)),
        ],
        out_specs=[
            pl.BlockSpec((None, TOKEN_TILE, d), lambda bi, i: (bi, i, 0)),
            pl.BlockSpec((None, n_hp, kb_per_step, LANES, SB_KCHUNK),
                         lambda bi, i: (bi, 0, i, 0, 0)),
            pl.BlockSpec((None, TOKEN_TILE, d), lambda bi, i: (bi, i, 0)),
        ],
        out_shape=[
            jax.ShapeDtypeStruct((b, s, d), BF16),
            jax.ShapeDtypeStruct((b, n_hp, n_kb, LANES, SB_KCHUNK), BF16),
            jax.ShapeDtypeStruct((b, s, d), BF16),
        ],
        compiler_params=_params(("parallel", "parallel")),
        name="sb_proj",
    )(h, gain, w_qkv)


def _sb_suffix_matrix():
    t = SB_SUB
    m = np.arange(t)[:, None]
    j = np.arange(t)[None, :]
    half = np.concatenate([(m > j).astype(np.float32), np.ones((t, t), np.float32)], axis=1)
    return jnp.asarray(np.concatenate([half, half], axis=0), dtype=BF16)


def _sb_attn_kernel(q_ref, kt_ref, v_ref, u_ref, o_ref, acc_ref, c_ref):
    t = SB_TILE
    sub = SB_SUB
    kc = SB_KCHUNK
    per_tile = t // kc
    qi = pl.program_id(2)
    q = q_ref[...]
    lane = lax.broadcasted_iota(jnp.int32, (t, LANES), 1)
    first = lane < SB_HEAD_DIM
    zero = jnp.zeros_like(q)
    q_heads = (jnp.where(first, q, zero), jnp.where(first, zero, q))
    u = u_ref[...]

    def chunk(qh, kblk, vblk, c, mask):
        n_sub = kblk.shape[1] // sub
        z = _dot(qh, kblk)
        log_beta = _log_sigmoid(z)
        log_1m = log_beta - z
        if mask is not None:
            log_1m = jnp.where(mask, log_1m, 0.0)
        hi, lo = _split_bf16(log_1m)
        ws = [None] * n_sub
        for s in reversed(range(n_sub)):
            cols = slice(s * sub, (s + 1) * sub)
            s2 = _dot(jnp.concatenate([hi[:, cols], lo[:, cols]], axis=1), u)
            ws[s] = jnp.exp(log_beta[:, cols] + s2[:, :sub] + c)
            c = c + s2[:, sub:]
        w = jnp.concatenate(ws, axis=1)
        if mask is not None:
            w = jnp.where(mask, w, 0.0)
        return _dot(w.astype(BF16), vblk), c

    row = lax.broadcasted_iota(jnp.int32, (t, t), 0)
    col = lax.broadcasted_iota(jnp.int32, (t, t), 1)
    kd = jnp.concatenate([kt_ref[qi * per_tile + i] for i in range(per_tile)], axis=1)
    vd = v_ref[pl.ds(pl.multiple_of(qi * t, t), t), :]
    for hd in range(2):
        pv, c = chunk(q_heads[hd], kd, vd, jnp.zeros((t, sub), F32), col < row)
        acc_ref[hd] = pv
        c_ref[hd] = c

    def more_keys_matter(carry):
        j, c_max = carry
        return jnp.logical_and(j >= 0, c_max > SB_F32_EXP_ZERO)

    def body(carry):
        j, _ = carry
        kblk = kt_ref[j]
        vblk = v_ref[pl.ds(pl.multiple_of(j * kc, kc), kc), :]
        for hd in range(2):
            pv, c = chunk(q_heads[hd], kblk, vblk, c_ref[hd], None)
            acc_ref[hd] += pv
            c_ref[hd] = c
        return j - 1, jnp.max(c_ref[...])

    lax.while_loop(more_keys_matter, body, (qi * per_tile - 1, jnp.max(c_ref[...])))
    o_ref[...] = jnp.where(first, acc_ref[0], acc_ref[1]).astype(BF16)


def _sb_attention(q, kt, v):
    b, s, d = q.shape
    n_hp = d // LANES
    n_kb = s // SB_KCHUNK
    t = SB_TILE
    return pl.pallas_call(
        _sb_attn_kernel,
        grid=(b, n_hp, s // t),
        in_specs=[
            pl.BlockSpec((None, t, LANES), lambda bi, hp, qi: (bi, qi, hp)),
            pl.BlockSpec((None, None, n_kb, LANES, SB_KCHUNK),
                         lambda bi, hp, qi: (bi, hp, 0, 0, 0)),
            pl.BlockSpec((None, s, LANES), lambda bi, hp, qi: (bi, 0, hp)),
            _resident((2 * SB_SUB, 2 * SB_SUB)),
        ],
        out_specs=pl.BlockSpec((None, t, LANES), lambda bi, hp, qi: (bi, qi, hp)),
        out_shape=jax.ShapeDtypeStruct((b, s, d), BF16),
        scratch_shapes=[pltpu.VMEM((2, t, LANES), F32), pltpu.VMEM((2, t, SB_SUB), F32)],
        compiler_params=_params(("parallel", "parallel", "arbitrary")),
        name="sb_attn",
    )(q, kt, v, _sb_suffix_matrix())


def _conv_kernel(x_ref, xp_ref, g_ref, w_ref, cw_ref, a_ref, hbuf_ref):
    d = D_MODEL
    tm = TOKEN_TILE
    i = pl.program_id(1)
    gain = g_ref[...]
    xn = _rms(x_ref[...], gain).astype(BF16)
    xpn = _rms(xp_ref[...], gain).astype(BF16)
    hbuf_ref[SUBLANES:, :] = _dot(xn, w_ref[:, d:2 * d]) * _dot(xn, w_ref[:, 2 * d:3 * d])
    h_prev = _dot(xpn, w_ref[:, d:2 * d]) * _dot(xpn, w_ref[:, 2 * d:3 * d])
    hbuf_ref[0:SUBLANES, :] = jnp.where(i > 0, h_prev, 0.0)
    conv = cw_ref[0:1, :] * hbuf_ref[pl.ds(SUBLANES - 2, tm), :]
    conv = conv + cw_ref[1:2, :] * hbuf_ref[pl.ds(SUBLANES - 1, tm), :]
    conv = conv + cw_ref[2:3, :] * hbuf_ref[pl.ds(SUBLANES, tm), :]
    a_ref[...] = (_dot(xn, w_ref[:, 0:d]) * conv).astype(BF16)


def _conv_mixer(h, gain, w_in, conv_w):
    b, s, d = h.shape
    tm = TOKEN_TILE
    rows_per_tile = tm // SUBLANES
    return pl.pallas_call(
        _conv_kernel,
        grid=(b, s // tm),
        in_specs=[
            pl.BlockSpec((None, tm, d), lambda bi, i: (bi, i, 0)),
            pl.BlockSpec((None, SUBLANES, d),
                         lambda bi, i: (bi, jnp.maximum(i * rows_per_tile - 1, 0), 0)),
            _resident((1, d)),
            _resident((d, 3 * d)),
            _resident((CONV_WIDTH, d)),
        ],
        out_specs=pl.BlockSpec((None, tm, d), lambda bi, i: (bi, i, 0)),
        out_shape=jax.ShapeDtypeStruct((b, s, d), BF16),
        scratch_shapes=[pltpu.VMEM((tm + SUBLANES, d), F32)],
        compiler_params=_params(("parallel", "parallel")),
        name="conv_mixer",
    )(h, h, gain, w_in, conv_w)


def _gla_proj_kernel(x_ref, g_ref, w_ref, wa_ref, wgu_ref, bg_ref,
                     q_ref, kt_ref, v_ref, og_ref, lg_ref, lgt_ref):
    dk, dv = GLA_DK, GLA_DV
    xn = _rms(x_ref[...], g_ref[...]).astype(BF16)
    q_ref[...] = _dot(xn, w_ref[:, 0:dk]).astype(BF16)
    kt_ref[...] = _dot(xn, w_ref[:, dk:2 * dk]).T.astype(BF16)
    v_ref[...] = _dot(xn, w_ref[:, 2 * dk:2 * dk + dv]).astype(BF16)
    og_ref[...] = _dot(xn, w_ref[:, 2 * dk + dv:2 * dk + 2 * dv]).astype(BF16)
    a_low = _dot(xn, wa_ref[...]).astype(BF16)
    pre = _dot(a_low, wgu_ref[...]) + bg_ref[...]
    lg = _log_sigmoid(pre) * (1.0 / GLA_GATE_NORMALIZER)
    lg_ref[...] = lg
    lgt_ref[...] = lg.T


def _gla_proj(h, gain, w_main, w_a, w_gu, b_gate):
    b, s, d = h.shape
    tm = TOKEN_TILE
    dk, dv = GLA_DK, GLA_DV
    row = lambda bi, i: (bi, i, 0)
    colmajor = lambda bi, i: (bi, 0, i)
    return pl.pallas_call(
        _gla_proj_kernel,
        grid=(b, s // tm),
        in_specs=[
            pl.BlockSpec((None, tm, d), row),
            _resident((1, d)),
            _resident((d, 2 * dk + 2 * dv)),
            _resident((d, LANES)),
            _resident((LANES, dk)),
            _resident((1, dk)),
        ],
        out_specs=[
            pl.BlockSpec((None, tm, dk), row),
            pl.BlockSpec((None, dk, tm), colmajor),
            pl.BlockSpec((None, tm, dv), row),
            pl.BlockSpec((None, tm, dv), row),
            pl.BlockSpec((None, tm, dk), row),
            pl.BlockSpec((None, dk, tm), colmajor),
        ],
        out_shape=[
            jax.ShapeDtypeStruct((b, s, dk), BF16),
            jax.ShapeDtypeStruct((b, dk, s), BF16),
            jax.ShapeDtypeStruct((b, s, dv), BF16),
            jax.ShapeDtypeStruct((b, s, dv), BF16),
            jax.ShapeDtypeStruct((b, s, dk), F32),
            jax.ShapeDtypeStruct((b, dk, s), F32),
        ],
        compiler_params=_params(("parallel", "parallel")),
        name="gla_proj",
    )(h, gain, w_main, w_a, w_gu, b_gate)


def _gla_constants():
    c = GLA_TILE
    i = np.arange(c)[:, None]
    m = np.arange(c)[None, :]
    fwd, rev, masks = [], [], []
    for level in range(GLA_LEVELS + 1):
        n = 1 << level
        same = (i // n) == (m // n)
        fwd.append((same & (m <= i)).astype(np.float32))
        rev.append((same & (m > i)).astype(np.float32).T)
    rev.append(np.ones((c, c), np.float32))
    for level in range(GLA_LEVELS):
        n = 1 << level
        masks.append(((i // (2 * n) == m // (2 * n)) & ((i // n) % 2 == 1)
                      & ((m // n) % 2 == 0)).astype(np.float32))
    masks.append((i == m).astype(np.float32))
    fwd = np.concatenate(fwd, axis=0)
    rev = np.concatenate(rev, axis=1)
    fwd2 = np.concatenate([fwd, fwd], axis=1)
    rev2 = np.concatenate([rev, rev], axis=0)
    return (jnp.asarray(fwd2, dtype=BF16), jnp.asarray(rev2, dtype=BF16),
            jnp.asarray(np.stack(masks), dtype=F32))


def _gla_chunk_kernel(q_ref, kt_ref, v_ref, og_ref, lg_ref, lgt_ref, fwd_ref, rev_ref,
                      mask_ref, hn_ref, a_ref, state_ref):
    c = GLA_TILE
    dkh, dvh = GLA_DK_HEAD, GLA_DV_HEAD
    n_lv = GLA_LEVELS

    @pl.when(pl.program_id(1) == 0)
    def _():
        state_ref[...] = jnp.zeros_like(state_ref)

    hi, lo = _split_bf16(lg_ref[...])
    fwd_all = _dot(fwd_ref[...], jnp.concatenate([hi, lo], axis=0))
    hi_t, lo_t = _split_bf16(lgt_ref[...])
    rev_all = _dot(jnp.concatenate([hi_t, lo_t], axis=1), rev_ref[...])

    for hd in range(GLA_HEADS):
        ks = slice(hd * dkh, (hd + 1) * dkh)
        vs = slice(hd * dvh, (hd + 1) * dvh)
        q = q_ref[:, ks].astype(F32) * GLA_SCALE
        kt = kt_ref[ks, :].astype(F32)
        v = v_ref[:, vs]
        scores = _dot(q.astype(BF16), kt.astype(BF16)) * mask_ref[n_lv]
        for lv in range(n_lv):
            qd = q * jnp.exp(fwd_all[lv * c:(lv + 1) * c, ks])
            kd = kt * jnp.exp(rev_all[ks, lv * c:(lv + 1) * c])
            scores = scores + _dot(qd.astype(BF16), kd.astype(BF16)) * mask_ref[lv]
        state = state_ref[hd]
        q_cum = q * jnp.exp(fwd_all[n_lv * c:(n_lv + 1) * c, ks])
        out = _dot(q_cum.astype(BF16), state.astype(BF16)) + _dot(scores.astype(BF16), v)
        k_rest = kt * jnp.exp(rev_all[ks, n_lv * c:(n_lv + 1) * c])
        keep = jnp.exp(rev_all[ks, (n_lv + 1) * c:(n_lv + 2) * c])
        keep = jnp.concatenate([keep] * (dvh // c), axis=1)
        state_ref[hd] = keep * state + _dot(k_rest.astype(BF16), v)
        o = _rms(out, hn_ref[:, vs])
        og = og_ref[:, vs].astype(F32)
        a_ref[:, vs] = (o * (og * (1.0 / (1.0 + jnp.exp(-og))))).astype(BF16)


def _gla_chunk(q, kt, v, og, lg, lgt, head_norm):
    b, s, dk = q.shape
    dv = v.shape[-1]
    c = GLA_TILE
    fwd, rev, masks = _gla_constants()
    row = lambda bi, i: (bi, i, 0)
    colmajor = lambda bi, i: (bi, 0, i)
    return pl.pallas_call(
        _gla_chunk_kernel,
        grid=(b, s // c),
        in_specs=[
            pl.BlockSpec((None, c, dk), row),
            pl.BlockSpec((None, dk, c), colmajor),
            pl.BlockSpec((None, c, dv), row),
            pl.BlockSpec((None, c, dv), row),
            pl.BlockSpec((None, c, dk), row),
            pl.BlockSpec((None, dk, c), colmajor),
            _resident(fwd.shape),
            _resident(rev.shape),
            _resident(masks.shape),
            _resident((1, dv)),
        ],
        out_specs=pl.BlockSpec((None, c, dv), row),
        out_shape=jax.ShapeDtypeStruct((b, s, dv), BF16),
        scratch_shapes=[pltpu.VMEM((GLA_HEADS, GLA_DK_HEAD, GLA_DV_HEAD), F32)],
        compiler_params=_params(("parallel", "arbitrary")),
        name="gla_chunk",
    )(q, kt, v, og, lg, lgt, fwd, rev, masks, head_norm)


def _post_kernel(a_ref, h_ref, wo_ref, g_ref, wup_ref, wdn_ref, o_ref, acc_ref):
    mixed = _dot(a_ref[...], wo_ref[...])
    h1 = h_ref[...] + _rms(mixed, g_ref[1:2, :])
    xn = _rms(h1, g_ref[2:3, :]).astype(BF16)
    for ci in range(D_FF // FF_CHUNK):
        cols = slice(ci * FF_CHUNK, (ci + 1) * FF_CHUNK)
        up = jnp.maximum(_dot(xn, wup_ref[:, cols]), 0.0)
        down = _dot((up * up).astype(BF16), wdn_ref[cols, :])
        if ci == 0:
            acc_ref[...] = down
        else:
            acc_ref[...] += down
    o_ref[...] = h1 + _rms(acc_ref[...], g_ref[3:4, :])


def _post(a, h, w_o, gains, w_up, w_down):
    b, s, d = h.shape
    tm = TOKEN_TILE
    row = lambda bi, i: (bi, i, 0)
    return pl.pallas_call(
        _post_kernel,
        grid=(b, s // tm),
        in_specs=[
            pl.BlockSpec((None, tm, d), row),
            pl.BlockSpec((None, tm, d), row),
            _resident((d, d)),
            _resident((4, d)),
            _resident((d, D_FF)),
            _resident((D_FF, d)),
        ],
        out_specs=pl.BlockSpec((None, tm, d), row),
        out_shape=jax.ShapeDtypeStruct((b, s, d), F32),
        scratch_shapes=[pltpu.VMEM((tm, d), F32)],
        compiler_params=_params(("parallel", "parallel")),
        name="post_ffn",
    )(a, h, w_o, gains, w_up, w_down)


def kernel(x, norm_gains, sb_w_qkv, sb_w_o, conv_w_in, conv_w, conv_w_out, gla_w_in,
           gla_w_gate_up, gla_b_gate, gla_head_norm, gla_w_o, ffn_w_up, ffn_w_down):
    depth = norm_gains.shape[0]
    h = x
    for i in range(depth):
        kind, j = i % 3, i // 3
        gains = norm_gains[i]
        pre_gain = gains[0:1]
        if kind == 0:
            q, kt, v = _sb_proj(h, pre_gain, sb_w_qkv[j].astype(BF16))
            a = _sb_attention(q, kt, v)
            w_o = sb_w_o[j]
        elif kind == 1:
            a = _conv_mixer(h, pre_gain, conv_w_in[j].astype(BF16), conv_w[j])
            w_o = conv_w_out[j]
        else:
            n_main = 2 * GLA_DK + 2 * GLA_DV
            w_in = gla_w_in[j]
            w_a = jnp.pad(w_in[:, n_main:], ((0, 0), (0, LANES - GLA_GATE_RANK))).astype(BF16)
            w_gu = jnp.pad(gla_w_gate_up[j], ((0, LANES - GLA_GATE_RANK), (0, 0))).astype(BF16)
            q, kt, v, og, lg, lgt = _gla_proj(h, pre_gain, w_in[:, :n_main].astype(BF16), w_a,
                                              w_gu, gla_b_gate[j][None, :])
            a = _gla_chunk(q, kt, v, og, lg, lgt, gla_head_norm[j].reshape(1, GLA_DV))
            w_o = gla_w_o[j]
        h = _post(a, h, w_o.astype(BF16), gains, ffn_w_up[i].astype(BF16),
                  ffn_w_down[i].astype(BF16))
    return h
```

```python
import functools

import numpy as np
import jax
import jax.numpy as jnp
from jax import lax
from jax.experimental import pallas as pl
from jax.experimental.pallas import tpu as pltpu

F32 = jnp.float32
BF16 = jnp.bfloat16

D_MODEL = 1024
D_FF = 4 * D_MODEL
RMS_EPS = 1e-6

SB_HEADS = 16
SB_HEAD_DIM = D_MODEL // SB_HEADS
SB_SCALE = SB_HEAD_DIM ** -0.5
CONV_WIDTH = 3
GLA_HEADS = 4
GLA_DK = D_MODEL // 2
GLA_DV = D_MODEL
GLA_DK_HEAD = GLA_DK // GLA_HEADS
GLA_DV_HEAD = GLA_DV // GLA_HEADS
GLA_GATE_RANK = 16
GLA_GATE_NORMALIZER = 16.0
GLA_SCALE = GLA_DK_HEAD ** -0.5

LANES = 128
SUBLANES = 8
VMEM_LIMIT_BYTES = 56 * 1024 * 1024

TOKEN_TILE = 512
FF_CHUNK = 1024
SB_TILE = 512
SB_SUB = 128
SB_BAND_SUBS = 3
SB_GROUP = 2
SB_F32_EXP_ZERO = -105.0
GLA_TILE = 128
GLA_LEVELS = 7


def _rms(x, gain):
    ms = jnp.mean(x * x, axis=-1, keepdims=True)
    return x * lax.rsqrt(ms + RMS_EPS) * gain


def _log_sigmoid(z):
    return jnp.minimum(z, 0.0) - jnp.log(1.0 + jnp.exp(-jnp.abs(z)))


def _split_bf16(x):
    hi = x.astype(BF16)
    lo = (x - hi.astype(F32)).astype(BF16)
    return hi, lo


def _dot(a, b):
    return jnp.dot(a, b, preferred_element_type=F32)


def _resident(shape):
    zeros = (0,) * len(shape)
    return pl.BlockSpec(shape, lambda *_: zeros, pipeline_mode=pl.Buffered(1))


def _params(semantics):
    return pltpu.CompilerParams(dimension_semantics=semantics,
                                vmem_limit_bytes=VMEM_LIMIT_BYTES)


def _sb_proj_kernel(x_ref, g_ref, w_ref, q_ref, kt_ref, v_ref):
    d = D_MODEL
    xn = _rms(x_ref[...], g_ref[...]).astype(BF16)
    q_ref[...] = (_dot(xn, w_ref[:, 0:d]) * SB_SCALE).astype(BF16)
    v_ref[...] = _dot(xn, w_ref[:, 2 * d:3 * d]).astype(BF16)
    k = _dot(xn, w_ref[:, d:2 * d])
    for hp in range(d // LANES):
        for kb in range(TOKEN_TILE // SB_SUB):
            blk = k[kb * SB_SUB:(kb + 1) * SB_SUB, hp * LANES:(hp + 1) * LANES]
            kt_ref[hp, kb] = blk.T.astype(BF16)


def _sb_proj(h, gain, w_qkv):
    b, s, d = h.shape
    n_hp = d // LANES
    n_kb = s // SB_SUB
    kb_per_step = TOKEN_TILE // SB_SUB
    return pl.pallas_call(
        _sb_proj_kernel,
        grid=(b, s // TOKEN_TILE),
        in_specs=[
            pl.BlockSpec((None, TOKEN_TILE, d), lambda bi, i: (bi, i, 0)),
            _resident((1, d)),
            _resident((d, 3 * d)),
        ],
        out_specs=[
            pl.BlockSpec((None, TOKEN_TILE, d), lambda bi, i: (bi, i, 0)),
            pl.BlockSpec((None, n_hp, kb_per_step, LANES, SB_SUB),
                         lambda bi, i: (bi, 0, i, 0, 0)),
            pl.BlockSpec((None, TOKEN_TILE, d), lambda bi, i: (bi, i, 0)),
        ],
        out_shape=[
            jax.ShapeDtypeStruct((b, s, d), BF16),
            jax.ShapeDtypeStruct((b, n_hp, n_kb, LANES, SB_SUB), BF16),
            jax.ShapeDtypeStruct((b, s, d), BF16),
        ],
        compiler_params=_params(("parallel", "parallel")),
        name="sb_proj",
    )(h, gain, w_qkv)


def _sb_suffix_matrix():
    t = SB_SUB
    m = np.arange(t)[:, None]
    j = np.arange(t)[None, :]
    half = np.concatenate([(m > j).astype(np.float32), np.ones((t, t), np.float32)], axis=1)
    return jnp.asarray(np.concatenate([half, half], axis=0), dtype=BF16)


def _sb_attn_kernel(q_ref, kt_ref, v_ref, u_ref, o_ref, acc_ref, c_ref):
    t = SB_TILE
    sub = SB_SUB
    n_rb = t // sub
    band = SB_BAND_SUBS
    qi = pl.program_id(2)
    q = q_ref[...]
    lane = lax.broadcasted_iota(jnp.int32, (t, LANES), 1)
    first = lane < SB_HEAD_DIM
    zero = jnp.zeros_like(q)
    q_heads = (jnp.where(first, q, zero), jnp.where(first, zero, q))
    u = u_ref[...]

    def chunk(qh, kblk, vblk, c, masks):
        n_sub = kblk.shape[1] // sub
        rows = qh.shape[0]
        z = _dot(qh, kblk)
        log_beta = _log_sigmoid(z)
        log_1m = log_beta - z
        parts = []
        for s in range(n_sub):
            part = log_1m[:, s * sub:(s + 1) * sub]
            if masks[s] is not None:
                part = jnp.where(masks[s], part, 0.0)
            hi, lo = _split_bf16(part)
            parts.append(jnp.concatenate([hi, lo], axis=1))
        s2 = _dot(jnp.concatenate(parts, axis=0), u)
        ws = [None] * n_sub
        for s in reversed(range(n_sub)):
            blk = s2[s * rows:(s + 1) * rows]
            w = jnp.exp(log_beta[:, s * sub:(s + 1) * sub] + blk[:, :sub] + c)
            if masks[s] is not None:
                w = jnp.where(masks[s], w, 0.0)
            ws[s] = w.astype(BF16)
            c = c + blk[:, sub:]
        return _dot(jnp.concatenate(ws, axis=1), vblk), c

    @pl.when(qi == 0)
    def _():
        row = lax.broadcasted_iota(jnp.int32, (t, sub), 0)
        col = lax.broadcasted_iota(jnp.int32, (t, sub), 1)
        masks = [col + s * sub < row for s in range(n_rb)]
        kd = jnp.concatenate([kt_ref[s] for s in range(n_rb)], axis=1)
        vd = v_ref[0:t, :]
        for hd in range(2):
            pv, _ = chunk(q_heads[hd], kd, vd, jnp.zeros((t, sub), F32), masks)
            acc_ref[hd] = pv

    @pl.when(qi > 0)
    def _():
        row = lax.broadcasted_iota(jnp.int32, (sub, sub), 0)
        col = lax.broadcasted_iota(jnp.int32, (sub, sub), 1)
        diag = col < row
        grp = SB_GROUP
        win = band + grp - 1
        no_weight = jnp.zeros((sub, sub), BF16)
        for hd in range(2):
            parts = []
            log_betas = []
            for p in range(n_rb // grp):
                first_sub = qi * n_rb + p * grp - band + 1
                kwin = jnp.concatenate([kt_ref[first_sub + s] for s in range(win)], axis=1)
                z = _dot(q_heads[hd][p * grp * sub:(p + 1) * grp * sub], kwin)
                for rr in range(grp):
                    zt = z[rr * sub:(rr + 1) * sub, rr * sub:(rr + band) * sub]
                    log_beta = _log_sigmoid(zt)
                    log_1m = log_beta - zt
                    for s in range(band):
                        part = log_1m[:, s * sub:(s + 1) * sub]
                        if s == band - 1:
                            part = jnp.where(diag, part, 0.0)
                        hi, lo = _split_bf16(part)
                        parts.append(jnp.concatenate([hi, lo], axis=1))
                    log_betas.append(log_beta)
            s2 = _dot(jnp.concatenate(parts, axis=0), u)
            for p in range(n_rb // grp):
                first_sub = qi * n_rb + p * grp - band + 1
                w_rows = []
                for rr in range(grp):
                    r = p * grp + rr
                    c = jnp.zeros((sub, sub), F32)
                    ws = [None] * band
                    for s in reversed(range(band)):
                        blk = s2[(r * band + s) * sub:(r * band + s + 1) * sub]
                        w = jnp.exp(log_betas[r][:, s * sub:(s + 1) * sub] + blk[:, :sub] + c)
                        if s == band - 1:
                            w = jnp.where(diag, w, 0.0)
                        ws[s] = w.astype(BF16)
                        c = c + blk[:, sub:]
                    c_ref[hd, r * sub:(r + 1) * sub, :] = c
                    w_rows.append(jnp.concatenate(
                        [no_weight] * rr + ws + [no_weight] * (grp - 1 - rr), axis=1))
                vwin = v_ref[pl.ds(pl.multiple_of(first_sub * sub, sub), win * sub), :]
                acc_ref[hd, p * grp * sub:(p + 1) * grp * sub, :] = _dot(
                    jnp.concatenate(w_rows, axis=0), vwin)

        row_blk = lax.broadcasted_iota(jnp.int32, (t, sub), 0) // sub

        def more_keys_matter(carry):
            k, c_max = carry
            return jnp.logical_and(k >= 0, c_max > SB_F32_EXP_ZERO)

        def body(carry):
            k, _ = carry
            not_yet_seen = [row_blk >= k - qi * n_rb + band]
            kblk = kt_ref[k]
            vblk = v_ref[pl.ds(pl.multiple_of(k * sub, sub), sub), :]
            for hd in range(2):
                pv, c = chunk(q_heads[hd], kblk, vblk, c_ref[hd], not_yet_seen)
                acc_ref[hd] += pv
                c_ref[hd] = c
            return k - 1, jnp.max(c_ref[...])

        k_start = qi * n_rb + n_rb - 1 - band
        lax.while_loop(more_keys_matter, body, (k_start, jnp.max(c_ref[...])))

    o_ref[...] = jnp.where(first, acc_ref[0], acc_ref[1]).astype(BF16)


def _sb_attention(q, kt, v):
    b, s, d = q.shape
    n_hp = d // LANES
    n_kb = s // SB_SUB
    t = SB_TILE
    return pl.pallas_call(
        _sb_attn_kernel,
        grid=(b, n_hp, s // t),
        in_specs=[
            pl.BlockSpec((None, t, LANES), lambda bi, hp, qi: (bi, qi, hp)),
            pl.BlockSpec((None, None, n_kb, LANES, SB_SUB),
                         lambda bi, hp, qi: (bi, hp, 0, 0, 0)),
            pl.BlockSpec((None, s, LANES), lambda bi, hp, qi: (bi, 0, hp)),
            _resident((2 * SB_SUB, 2 * SB_SUB)),
        ],
        out_specs=pl.BlockSpec((None, t, LANES), lambda bi, hp, qi: (bi, qi, hp)),
        out_shape=jax.ShapeDtypeStruct((b, s, d), BF16),
        scratch_shapes=[pltpu.VMEM((2, t, LANES), F32), pltpu.VMEM((2, t, SB_SUB), F32)],
        compiler_params=_params(("parallel", "parallel", "arbitrary")),
        name="sb_attn",
    )(q, kt, v, _sb_suffix_matrix())


def _conv_kernel(x_ref, xp_ref, g_ref, w_ref, cw_ref, a_ref, hbuf_ref):
    d = D_MODEL
    tm = TOKEN_TILE
    i = pl.program_id(1)
    gain = g_ref[...]
    xn = _rms(x_ref[...], gain).astype(BF16)
    xpn = _rms(xp_ref[...], gain).astype(BF16)
    hbuf_ref[SUBLANES:, :] = _dot(xn, w_ref[:, d:2 * d]) * _dot(xn, w_ref[:, 2 * d:3 * d])
    h_prev = _dot(xpn, w_ref[:, d:2 * d]) * _dot(xpn, w_ref[:, 2 * d:3 * d])
    hbuf_ref[0:SUBLANES, :] = jnp.where(i > 0, h_prev, 0.0)
    conv = cw_ref[0:1, :] * hbuf_ref[pl.ds(SUBLANES - 2, tm), :]
    conv = conv + cw_ref[1:2, :] * hbuf_ref[pl.ds(SUBLANES - 1, tm), :]
    conv = conv + cw_ref[2:3, :] * hbuf_ref[pl.ds(SUBLANES, tm), :]
    a_ref[...] = (_dot(xn, w_ref[:, 0:d]) * conv).astype(BF16)


def _conv_mixer(h, gain, w_in, conv_w):
    b, s, d = h.shape
    tm = TOKEN_TILE
    rows_per_tile = tm // SUBLANES
    return pl.pallas_call(
        _conv_kernel,
        grid=(b, s // tm),
        in_specs=[
            pl.BlockSpec((None, tm, d), lambda bi, i: (bi, i, 0)),
            pl.BlockSpec((None, SUBLANES, d),
                         lambda bi, i: (bi, jnp.maximum(i * rows_per_tile - 1, 0), 0)),
            _resident((1, d)),
            _resident((d, 3 * d)),
            _resident((CONV_WIDTH, d)),
        ],
        out_specs=pl.BlockSpec((None, tm, d), lambda bi, i: (bi, i, 0)),
        out_shape=jax.ShapeDtypeStruct((b, s, d), BF16),
        scratch_shapes=[pltpu.VMEM((tm + SUBLANES, d), F32)],
        compiler_params=_params(("parallel", "parallel")),
        name="conv_mixer",
    )(h, h, gain, w_in, conv_w)


def _gla_proj_kernel(x_ref, g_ref, w_ref, wa_ref, wgu_ref, bg_ref,
                     q_ref, kt_ref, v_ref, og_ref, lg_ref, lgt_ref):
    dk, dv = GLA_DK, GLA_DV
    xn = _rms(x_ref[...], g_ref[...]).astype(BF16)
    q_ref[...] = _dot(xn, w_ref[:, 0:dk]).astype(BF16)
    kt_ref[...] = _dot(xn, w_ref[:, dk:2 * dk]).T.astype(BF16)
    v_ref[...] = _dot(xn, w_ref[:, 2 * dk:2 * dk + dv]).astype(BF16)
    og_ref[...] = _dot(xn, w_ref[:, 2 * dk + dv:2 * dk + 2 * dv]).astype(BF16)
    a_low = _dot(xn, wa_ref[...]).astype(BF16)
    pre = _dot(a_low, wgu_ref[...]) + bg_ref[...]
    lg = _log_sigmoid(pre) * (1.0 / GLA_GATE_NORMALIZER)
    lg_ref[...] = lg
    lgt_ref[...] = lg.T


def _gla_proj(h, gain, w_main, w_a, w_gu, b_gate):
    b, s, d = h.shape
    tm = TOKEN_TILE
    dk, dv = GLA_DK, GLA_DV
    row = lambda bi, i: (bi, i, 0)
    colmajor = lambda bi, i: (bi, 0, i)
    return pl.pallas_call(
        _gla_proj_kernel,
        grid=(b, s // tm),
        in_specs=[
            pl.BlockSpec((None, tm, d), row),
            _resident((1, d)),
            _resident((d, 2 * dk + 2 * dv)),
            _resident((d, LANES)),
            _resident((LANES, dk)),
            _resident((1, dk)),
        ],
        out_specs=[
            pl.BlockSpec((None, tm, dk), row),
            pl.BlockSpec((None, dk, tm), colmajor),
            pl.BlockSpec((None, tm, dv), row),
            pl.BlockSpec((None, tm, dv), row),
            pl.BlockSpec((None, tm, dk), row),
            pl.BlockSpec((None, dk, tm), colmajor),
        ],
        out_shape=[
            jax.ShapeDtypeStruct((b, s, dk), BF16),
            jax.ShapeDtypeStruct((b, dk, s), BF16),
            jax.ShapeDtypeStruct((b, s, dv), BF16),
            jax.ShapeDtypeStruct((b, s, dv), BF16),
            jax.ShapeDtypeStruct((b, s, dk), F32),
            jax.ShapeDtypeStruct((b, dk, s), F32),
        ],
        compiler_params=_params(("parallel", "parallel")),
        name="gla_proj",
    )(h, gain, w_main, w_a, w_gu, b_gate)


def _gla_constants():
    c = GLA_TILE
    i = np.arange(c)[:, None]
    m = np.arange(c)[None, :]
    fwd, rev, masks = [], [], []
    for level in range(GLA_LEVELS + 1):
        n = 1 << level
        same = (i // n) == (m // n)
        fwd.append((same & (m <= i)).astype(np.float32))
        rev.append((same & (m > i)).astype(np.float32).T)
    rev.append(np.ones((c, c), np.float32))
    for level in range(GLA_LEVELS):
        n = 1 << level
        masks.append(((i // (2 * n) == m // (2 * n)) & ((i // n) % 2 == 1)
                      & ((m // n) % 2 == 0)).astype(np.float32))
    masks.append((i == m).astype(np.float32))
    fwd = np.concatenate(fwd, axis=0)
    rev = np.concatenate(rev, axis=1)
    fwd2 = np.concatenate([fwd, fwd], axis=1)
    rev2 = np.concatenate([rev, rev], axis=0)
    return (jnp.asarray(fwd2, dtype=BF16), jnp.asarray(rev2, dtype=BF16),
            jnp.asarray(np.stack(masks), dtype=F32))


def _gla_chunk_kernel(q_ref, kt_ref, v_ref, og_ref, lg_ref, lgt_ref, fwd_ref, rev_ref,
                      mask_ref, hn_ref, a_ref, state_ref):
    c = GLA_TILE
    dkh, dvh = GLA_DK_HEAD, GLA_DV_HEAD
    n_lv = GLA_LEVELS

    @pl.when(pl.program_id(1) == 0)
    def _():
        state_ref[...] = jnp.zeros_like(state_ref)

    hi, lo = _split_bf16(lg_ref[...])
    fwd_all = _dot(fwd_ref[...], jnp.concatenate([hi, lo], axis=0))
    hi_t, lo_t = _split_bf16(lgt_ref[...])
    rev_all = _dot(jnp.concatenate([hi_t, lo_t], axis=1), rev_ref[...])

    for hd in range(GLA_HEADS):
        ks = slice(hd * dkh, (hd + 1) * dkh)
        vs = slice(hd * dvh, (hd + 1) * dvh)
        q = q_ref[:, ks].astype(F32) * GLA_SCALE
        kt = kt_ref[ks, :].astype(F32)
        v = v_ref[:, vs]
        scores = _dot(q.astype(BF16), kt.astype(BF16)) * mask_ref[n_lv]
        for lv in range(n_lv):
            qd = q * jnp.exp(fwd_all[lv * c:(lv + 1) * c, ks])
            kd = kt * jnp.exp(rev_all[ks, lv * c:(lv + 1) * c])
            scores = scores + _dot(qd.astype(BF16), kd.astype(BF16)) * mask_ref[lv]
        state = state_ref[hd]
        q_cum = q * jnp.exp(fwd_all[n_lv * c:(n_lv + 1) * c, ks])
        out = _dot(q_cum.astype(BF16), state.astype(BF16)) + _dot(scores.astype(BF16), v)
        k_rest = kt * jnp.exp(rev_all[ks, n_lv * c:(n_lv + 1) * c])
        keep = jnp.exp(rev_all[ks, (n_lv + 1) * c:(n_lv + 2) * c])
        keep = jnp.concatenate([keep] * (dvh // c), axis=1)
        state_ref[hd] = keep * state + _dot(k_rest.astype(BF16), v)
        o = _rms(out, hn_ref[:, vs])
        og = og_ref[:, vs].astype(F32)
        a_ref[:, vs] = (o * (og * (1.0 / (1.0 + jnp.exp(-og))))).astype(BF16)


def _gla_chunk(q, kt, v, og, lg, lgt, head_norm):
    b, s, dk = q.shape
    dv = v.shape[-1]
    c = GLA_TILE
    fwd, rev, masks = _gla_constants()
    row = lambda bi, i: (bi, i, 0)
    colmajor = lambda bi, i: (bi, 0, i)
    return pl.pallas_call(
        _gla_chunk_kernel,
        grid=(b, s // c),
        in_specs=[
            pl.BlockSpec((None, c, dk), row),
            pl.BlockSpec((None, dk, c), colmajor),
            pl.BlockSpec((None, c, dv), row),
            pl.BlockSpec((None, c, dv), row),
            pl.BlockSpec((None, c, dk), row),
            pl.BlockSpec((None, dk, c), colmajor),
            _resident(fwd.shape),
            _resident(rev.shape),
            _resident(masks.shape),
            _resident((1, dv)),
        ],
        out_specs=pl.BlockSpec((None, c, dv), row),
        out_shape=jax.ShapeDtypeStruct((b, s, dv), BF16),
        scratch_shapes=[pltpu.VMEM((GLA_HEADS, GLA_DK_HEAD, GLA_DV_HEAD), F32)],
        compiler_params=_params(("parallel", "arbitrary")),
        name="gla_chunk",
    )(q, kt, v, og, lg, lgt, fwd, rev, masks, head_norm)


def _post_kernel(a_ref, h_ref, wo_ref, g_ref, wup_ref, wdn_ref, o_ref, acc_ref):
    mixed = _dot(a_ref[...], wo_ref[...])
    h1 = h_ref[...] + _rms(mixed, g_ref[1:2, :])
    xn = _rms(h1, g_ref[2:3, :]).astype(BF16)
    for ci in range(D_FF // FF_CHUNK):
        cols = slice(ci * FF_CHUNK, (ci + 1) * FF_CHUNK)
        up = jnp.maximum(_dot(xn, wup_ref[:, cols]), 0.0)
        down = _dot((up * up).astype(BF16), wdn_ref[cols, :])
        if ci == 0:
            acc_ref[...] = down
        else:
            acc_ref[...] += down
    o_ref[...] = h1 + _rms(acc_ref[...], g_ref[3:4, :])


def _post(a, h, w_o, gains, w_up, w_down):
    b, s, d = h.shape
    tm = TOKEN_TILE
    row = lambda bi, i: (bi, i, 0)
    return pl.pallas_call(
        _post_kernel,
        grid=(b, s // tm),
        in_specs=[
            pl.BlockSpec((None, tm, d), row),
            pl.BlockSpec((None, tm, d), row),
            _resident((d, d)),
            _resident((4, d)),
            _resident((d, D_FF)),
            _resident((D_FF, d)),
        ],
        out_specs=pl.BlockSpec((None, tm, d), row),
        out_shape=jax.ShapeDtypeStruct((b, s, d), F32),
        scratch_shapes=[pltpu.VMEM((tm, d), F32)],
        compiler_params=_params(("parallel", "parallel")),
        name="post_ffn",
    )(a, h, w_o, gains, w_up, w_down)


def kernel(x, norm_gains, sb_w_qkv, sb_w_o, conv_w_in, conv_w, conv_w_out, gla_w_in,
           gla_w_gate_up, gla_b_gate, gla_head_norm, gla_w_o, ffn_w_up, ffn_w_down):
    depth = norm_gains.shape[0]
    h = x
    for i in range(depth):
        kind, j = i % 3, i // 3
        gains = norm_gains[i]
        pre_gain = gains[0:1]
        if kind == 0:
            q, kt, v = _sb_proj(h, pre_gain, sb_w_qkv[j].astype(BF16))
            a = _sb_attention(q, kt, v)
            w_o = sb_w_o[j]
        elif kind == 1:
            a = _conv_mixer(h, pre_gain, conv_w_in[j].astype(BF16), conv_w[j])
            w_o = conv_w_out[j]
        else:
            n_main = 2 * GLA_DK + 2 * GLA_DV
            w_in = gla_w_in[j]
            w_a = jnp.pad(w_in[:, n_main:], ((0, 0), (0, LANES - GLA_GATE_RANK))).astype(BF16)
            w_gu = jnp.pad(gla_w_gate_up[j], ((0, LANES - GLA_GATE_RANK), (0, 0))).astype(BF16)
            q, kt, v, og, lg, lgt = _gla_proj(h, pre_gain, w_in[:, :n_main].astype(BF16), w_a,
                                              w_gu, gla_b_gate[j][None, :])
            a = _gla_chunk(q, kt, v, og, lg, lgt, gla_head_norm[j].reshape(1, GLA_DV))
            w_o = gla_w_o[j]
        h = _post(a, h, w_o.astype(BF16), gains, ffn_w_up[i].astype(BF16),
                  ffn_w_down[i].astype(BF16))
    return h
```

```python
import functools

import numpy as np
import jax
import jax.numpy as jnp
from jax import lax
from jax.experimental import pallas as pl
from jax.experimental.pallas import tpu as pltpu

F32 = jnp.float32
BF16 = jnp.bfloat16

D_MODEL = 1024
D_FF = 4 * D_MODEL
RMS_EPS = 1e-6

SB_HEADS = 16
SB_HEAD_DIM = D_MODEL // SB_HEADS
SB_SCALE = SB_HEAD_DIM ** -0.5
CONV_WIDTH = 3
GLA_HEADS = 4
GLA_DK = D_MODEL // 2
GLA_DV = D_MODEL
GLA_DK_HEAD = GLA_DK // GLA_HEADS
GLA_DV_HEAD = GLA_DV // GLA_HEADS
GLA_GATE_RANK = 16
GLA_GATE_NORMALIZER = 16.0
GLA_SCALE = GLA_DK_HEAD ** -0.5

LANES = 128
SUBLANES = 8
VMEM_LIMIT_BYTES = 56 * 1024 * 1024

TOKEN_TILE = 512
FF_CHUNK = 1024
SB_TILE = 512
SB_SUB = 128
SB_BAND_SUBS = 3
SB_GROUP = 2
SB_F32_EXP_ZERO = -105.0
GLA_TILE = 128
GLA_STEP_CHUNKS = 2
GLA_LEVELS = 7


def _rms(x, gain):
    ms = jnp.mean(x * x, axis=-1, keepdims=True)
    return x * lax.rsqrt(ms + RMS_EPS) * gain


def _neg_abs(z):
    bits = lax.bitcast_convert_type(z, jnp.uint32) | jnp.uint32(0x80000000)
    return lax.bitcast_convert_type(bits, F32)


def _log_sigmoid(z):
    return jnp.minimum(z, 0.0) - jnp.log(1.0 + jnp.exp(_neg_abs(z)))


def _split_bf16(x):
    hi = x.astype(BF16)
    lo = (x - hi.astype(F32)).astype(BF16)
    return hi, lo


def _dot(a, b):
    return jnp.dot(a, b, preferred_element_type=F32)


def _resident(shape):
    zeros = (0,) * len(shape)
    return pl.BlockSpec(shape, lambda *_: zeros, pipeline_mode=pl.Buffered(1))


def _params(semantics):
    return pltpu.CompilerParams(dimension_semantics=semantics,
                                vmem_limit_bytes=VMEM_LIMIT_BYTES)


def _sb_proj_kernel(x_ref, g_ref, w_ref, q_ref, kt_ref, v_ref):
    d = D_MODEL
    xn = _rms(x_ref[...], g_ref[...]).astype(BF16)
    q_ref[...] = (_dot(xn, w_ref[:, 0:d]) * SB_SCALE).astype(BF16)
    v_ref[...] = _dot(xn, w_ref[:, 2 * d:3 * d]).astype(BF16)
    k = _dot(xn, w_ref[:, d:2 * d])
    for hp in range(d // LANES):
        for kb in range(TOKEN_TILE // SB_SUB):
            blk = k[kb * SB_SUB:(kb + 1) * SB_SUB, hp * LANES:(hp + 1) * LANES]
            kt_ref[hp, kb] = blk.T.astype(BF16)


def _sb_proj(h, gain, w_qkv):
    b, s, d = h.shape
    n_hp = d // LANES
    n_kb = s // SB_SUB
    kb_per_step = TOKEN_TILE // SB_SUB
    return pl.pallas_call(
        _sb_proj_kernel,
        grid=(b, s // TOKEN_TILE),
        in_specs=[
            pl.BlockSpec((None, TOKEN_TILE, d), lambda bi, i: (bi, i, 0)),
            _resident((1, d)),
            _resident((d, 3 * d)),
        ],
        out_specs=[
            pl.BlockSpec((None, TOKEN_TILE, d), lambda bi, i: (bi, i, 0)),
            pl.BlockSpec((None, n_hp, kb_per_step, LANES, SB_SUB),
                         lambda bi, i: (bi, 0, i, 0, 0)),
            pl.BlockSpec((None, TOKEN_TILE, d), lambda bi, i: (bi, i, 0)),
        ],
        out_shape=[
            jax.ShapeDtypeStruct((b, s, d), BF16),
            jax.ShapeDtypeStruct((b, n_hp, n_kb, LANES, SB_SUB), BF16),
            jax.ShapeDtypeStruct((b, s, d), BF16),
        ],
        compiler_params=_params(("parallel", "parallel")),
        name="sb_proj",
    )(h, gain, w_qkv)


def _sb_suffix_matrix():
    t = SB_SUB
    m = np.arange(t)[:, None]
    j = np.arange(t)[None, :]
    half = np.concatenate([(m > j).astype(np.float32), np.ones((t, t), np.float32)], axis=1)
    return jnp.asarray(np.concatenate([half, half], axis=0), dtype=BF16)


def _sb_attn_kernel(q_ref, kt_ref, v_ref, u_ref, o_ref, acc_ref, c_ref):
    t = SB_TILE
    sub = SB_SUB
    n_rb = t // sub
    band = SB_BAND_SUBS
    qi = pl.program_id(2)
    q = q_ref[...]
    lane = lax.broadcasted_iota(jnp.int32, (t, LANES), 1)
    first = lane < SB_HEAD_DIM
    zero = jnp.zeros_like(q)
    q_heads = (jnp.where(first, q, zero), jnp.where(first, zero, q))
    u = u_ref[...]

    def chunk(qh, kblk, vblk, c, masks):
        n_sub = kblk.shape[1] // sub
        rows = qh.shape[0]
        z = _dot(qh, kblk)
        log_beta = _log_sigmoid(z)
        log_1m = log_beta - z
        parts = []
        for s in range(n_sub):
            part = log_1m[:, s * sub:(s + 1) * sub]
            if masks[s] is not None:
                part = jnp.where(masks[s], part, 0.0)
            hi, lo = _split_bf16(part)
            parts.append(jnp.concatenate([hi, lo], axis=1))
        s2 = _dot(jnp.concatenate(parts, axis=0), u)
        ws = [None] * n_sub
        for s in reversed(range(n_sub)):
            blk = s2[s * rows:(s + 1) * rows]
            w = jnp.exp(log_beta[:, s * sub:(s + 1) * sub] + blk[:, :sub] + c)
            if masks[s] is not None:
                w = jnp.where(masks[s], w, 0.0)
            ws[s] = w.astype(BF16)
            c = c + blk[:, sub:]
        return _dot(jnp.concatenate(ws, axis=1), vblk), c

    @pl.when(qi == 0)
    def _():
        row = lax.broadcasted_iota(jnp.int32, (t, sub), 0)
        col = lax.broadcasted_iota(jnp.int32, (t, sub), 1)
        masks = [col + s * sub < row for s in range(n_rb)]
        kd = jnp.concatenate([kt_ref[s] for s in range(n_rb)], axis=1)
        vd = v_ref[0:t, :]
        for hd in range(2):
            pv, _ = chunk(q_heads[hd], kd, vd, jnp.zeros((t, sub), F32), masks)
            acc_ref[hd] = pv

    @pl.when(qi > 0)
    def _():
        row = lax.broadcasted_iota(jnp.int32, (sub, sub), 0)
        col = lax.broadcasted_iota(jnp.int32, (sub, sub), 1)
        diag = col < row
        grp = SB_GROUP
        win = band + grp - 1
        no_weight = jnp.zeros((sub, sub), BF16)
        parts = []
        log_betas = []
        for hd in range(2):
            for p in range(n_rb // grp):
                first_sub = qi * n_rb + p * grp - band + 1
                kwin = jnp.concatenate([kt_ref[first_sub + s] for s in range(win)], axis=1)
                z = _dot(q_heads[hd][p * grp * sub:(p + 1) * grp * sub], kwin)
                for rr in range(grp):
                    zt = z[rr * sub:(rr + 1) * sub, rr * sub:(rr + band) * sub]
                    log_beta = _log_sigmoid(zt)
                    log_1m = log_beta - zt
                    for s in range(band):
                        part = log_1m[:, s * sub:(s + 1) * sub]
                        if s == band - 1:
                            part = jnp.where(diag, part, 0.0)
                        hi, lo = _split_bf16(part)
                        parts.append(jnp.concatenate([hi, lo], axis=1))
                    log_betas.append(log_beta)
        s2 = _dot(jnp.concatenate(parts, axis=0), u)
        for hd in range(2):
            for p in range(n_rb // grp):
                first_sub = qi * n_rb + p * grp - band + 1
                w_rows = []
                for rr in range(grp):
                    r = p * grp + rr
                    tile = hd * n_rb + r
                    c = jnp.zeros((sub, sub), F32)
                    ws = [None] * band
                    for s in reversed(range(band)):
                        blk = s2[(tile * band + s) * sub:(tile * band + s + 1) * sub]
                        w = jnp.exp(log_betas[tile][:, s * sub:(s + 1) * sub] + blk[:, :sub] + c)
                        if s == band - 1:
                            w = jnp.where(diag, w, 0.0)
                        ws[s] = w.astype(BF16)
                        c = c + blk[:, sub:]
                    c_ref[hd, r * sub:(r + 1) * sub, :] = c
                    w_rows.append(jnp.concatenate(
                        [no_weight] * rr + ws + [no_weight] * (grp - 1 - rr), axis=1))
                vwin = v_ref[pl.ds(pl.multiple_of(first_sub * sub, sub), win * sub), :]
                acc_ref[hd, p * grp * sub:(p + 1) * grp * sub, :] = _dot(
                    jnp.concatenate(w_rows, axis=0), vwin)

        row_blk = lax.broadcasted_iota(jnp.int32, (t, sub), 0) // sub

        def more_keys_matter(carry):
            k, c_max = carry
            return jnp.logical_and(k >= 0, c_max > SB_F32_EXP_ZERO)

        def body(carry):
            k, _ = carry
            not_yet_seen = [row_blk >= k - qi * n_rb + band]
            kblk = kt_ref[k]
            vblk = v_ref[pl.ds(pl.multiple_of(k * sub, sub), sub), :]
            for hd in range(2):
                pv, c = chunk(q_heads[hd], kblk, vblk, c_ref[hd], not_yet_seen)
                acc_ref[hd] += pv
                c_ref[hd] = c
            return k - 1, jnp.max(c_ref[...])

        k_start = qi * n_rb + n_rb - 1 - band
        lax.while_loop(more_keys_matter, body, (k_start, jnp.max(c_ref[...])))

    o_ref[...] = jnp.where(first, acc_ref[0], acc_ref[1]).astype(BF16)


def _sb_attention(q, kt, v):
    b, s, d = q.shape
    n_hp = d // LANES
    n_kb = s // SB_SUB
    t = SB_TILE
    return pl.pallas_call(
        _sb_attn_kernel,
        grid=(b, n_hp, s // t),
        in_specs=[
            pl.BlockSpec((None, t, LANES), lambda bi, hp, qi: (bi, qi, hp)),
            pl.BlockSpec((None, None, n_kb, LANES, SB_SUB),
                         lambda bi, hp, qi: (bi, hp, 0, 0, 0)),
            pl.BlockSpec((None, s, LANES), lambda bi, hp, qi: (bi, 0, hp)),
            _resident((2 * SB_SUB, 2 * SB_SUB)),
        ],
        out_specs=pl.BlockSpec((None, t, LANES), lambda bi, hp, qi: (bi, qi, hp)),
        out_shape=jax.ShapeDtypeStruct((b, s, d), BF16),
        scratch_shapes=[pltpu.VMEM((2, t, LANES), F32), pltpu.VMEM((2, t, SB_SUB), F32)],
        compiler_params=_params(("parallel", "parallel", "arbitrary")),
        name="sb_attn",
    )(q, kt, v, _sb_suffix_matrix())


def _conv_kernel(x_ref, xp_ref, g_ref, w_ref, cw_ref, a_ref, hbuf_ref):
    d = D_MODEL
    tm = TOKEN_TILE
    i = pl.program_id(1)
    gain = g_ref[...]
    xn = _rms(x_ref[...], gain).astype(BF16)
    xpn = _rms(xp_ref[...], gain).astype(BF16)
    hbuf_ref[SUBLANES:, :] = _dot(xn, w_ref[:, d:2 * d]) * _dot(xn, w_ref[:, 2 * d:3 * d])
    h_prev = _dot(xpn, w_ref[:, d:2 * d]) * _dot(xpn, w_ref[:, 2 * d:3 * d])
    hbuf_ref[0:SUBLANES, :] = jnp.where(i > 0, h_prev, 0.0)
    conv = cw_ref[0:1, :] * hbuf_ref[pl.ds(SUBLANES - 2, tm), :]
    conv = conv + cw_ref[1:2, :] * hbuf_ref[pl.ds(SUBLANES - 1, tm), :]
    conv = conv + cw_ref[2:3, :] * hbuf_ref[pl.ds(SUBLANES, tm), :]
    a_ref[...] = (_dot(xn, w_ref[:, 0:d]) * conv).astype(BF16)


def _conv_mixer(h, gain, w_in, conv_w):
    b, s, d = h.shape
    tm = TOKEN_TILE
    rows_per_tile = tm // SUBLANES
    return pl.pallas_call(
        _conv_kernel,
        grid=(b, s // tm),
        in_specs=[
            pl.BlockSpec((None, tm, d), lambda bi, i: (bi, i, 0)),
            pl.BlockSpec((None, SUBLANES, d),
                         lambda bi, i: (bi, jnp.maximum(i * rows_per_tile - 1, 0), 0)),
            _resident((1, d)),
            _resident((d, 3 * d)),
            _resident((CONV_WIDTH, d)),
        ],
        out_specs=pl.BlockSpec((None, tm, d), lambda bi, i: (bi, i, 0)),
        out_shape=jax.ShapeDtypeStruct((b, s, d), BF16),
        scratch_shapes=[pltpu.VMEM((tm + SUBLANES, d), F32)],
        compiler_params=_params(("parallel", "parallel")),
        name="conv_mixer",
    )(h, h, gain, w_in, conv_w)


def _gla_proj_kernel(x_ref, g_ref, w_ref, wa_ref, wgu_ref, bg_ref,
                     q_ref, kt_ref, v_ref, og_ref, lg_ref, lgt_ref):
    dk, dv = GLA_DK, GLA_DV
    xn = _rms(x_ref[...], g_ref[...]).astype(BF16)
    q_ref[...] = _dot(xn, w_ref[:, 0:dk]).astype(BF16)
    kt_ref[...] = _dot(xn, w_ref[:, dk:2 * dk]).T.astype(BF16)
    v_ref[...] = _dot(xn, w_ref[:, 2 * dk:2 * dk + dv]).astype(BF16)
    og_ref[...] = _dot(xn, w_ref[:, 2 * dk + dv:2 * dk + 2 * dv]).astype(BF16)
    a_low = _dot(xn, wa_ref[...]).astype(BF16)
    pre = _dot(a_low, wgu_ref[...]) + bg_ref[...]
    lg = _log_sigmoid(pre) * (1.0 / GLA_GATE_NORMALIZER)
    lg_ref[...] = lg
    lgt_ref[...] = lg.T


def _gla_proj(h, gain, w_main, w_a, w_gu, b_gate):
    b, s, d = h.shape
    tm = TOKEN_TILE
    dk, dv = GLA_DK, GLA_DV
    row = lambda bi, i: (bi, i, 0)
    colmajor = lambda bi, i: (bi, 0, i)
    return pl.pallas_call(
        _gla_proj_kernel,
        grid=(b, s // tm),
        in_specs=[
            pl.BlockSpec((None, tm, d), row),
            _resident((1, d)),
            _resident((d, 2 * dk + 2 * dv)),
            _resident((d, LANES)),
            _resident((LANES, dk)),
            _resident((1, dk)),
        ],
        out_specs=[
            pl.BlockSpec((None, tm, dk), row),
            pl.BlockSpec((None, dk, tm), colmajor),
            pl.BlockSpec((None, tm, dv), row),
            pl.BlockSpec((None, tm, dv), row),
            pl.BlockSpec((None, tm, dk), row),
            pl.BlockSpec((None, dk, tm), colmajor),
        ],
        out_shape=[
            jax.ShapeDtypeStruct((b, s, dk), BF16),
            jax.ShapeDtypeStruct((b, dk, s), BF16),
            jax.ShapeDtypeStruct((b, s, dv), BF16),
            jax.ShapeDtypeStruct((b, s, dv), BF16),
            jax.ShapeDtypeStruct((b, s, dk), F32),
            jax.ShapeDtypeStruct((b, dk, s), F32),
        ],
        compiler_params=_params(("parallel", "parallel")),
        name="gla_proj",
    )(h, gain, w_main, w_a, w_gu, b_gate)


def _gla_constants():
    c = GLA_TILE
    i = np.arange(c)[:, None]
    m = np.arange(c)[None, :]
    fwd, rev, masks = [], [], []
    for level in range(1, GLA_LEVELS + 1):
        n = 1 << level
        same = (i // n) == (m // n)
        fwd.append((same & (m <= i)).astype(np.float32))
        rev.append((same & (m > i)).astype(np.float32).T)
    rev.append(np.ones((c, c), np.float32))
    for level in range(GLA_LEVELS):
        n = 1 << level
        masks.append(((i // (2 * n) == m // (2 * n)) & ((i // n) % 2 == 1)
                      & ((m // n) % 2 == 0)).astype(np.float32))
    masks.append((i == m).astype(np.float32))
    fwd = np.concatenate(fwd, axis=0)
    rev = np.concatenate(rev, axis=1)
    fwd2 = np.concatenate([fwd, fwd], axis=1)
    rev2 = np.concatenate([rev, rev], axis=0)
    return (jnp.asarray(fwd2, dtype=BF16), jnp.asarray(rev2, dtype=BF16),
            jnp.asarray(np.stack(masks), dtype=F32))


def _gla_chunk_kernel(q_ref, kt_ref, v_ref, og_ref, lg_ref, lgt_ref, fwd_ref, rev_ref,
                      mask_ref, hn_ref, a_ref, state_ref):
    c = GLA_TILE
    n_ch = GLA_STEP_CHUNKS
    dk = GLA_DK
    dkh, dvh = GLA_DK_HEAD, GLA_DV_HEAD
    n_lv = GLA_LEVELS

    @pl.when(pl.program_id(1) == 0)
    def _():
        state_ref[...] = jnp.zeros_like(state_ref)

    lg = lg_ref[...]
    hi, lo = _split_bf16(lg)
    g_cat = jnp.concatenate(
        [jnp.concatenate([hi[ch * c:(ch + 1) * c], lo[ch * c:(ch + 1) * c]], axis=0)
         for ch in range(n_ch)], axis=1)
    fwd_all = _dot(fwd_ref[...], g_cat)
    hi_t, lo_t = _split_bf16(lgt_ref[...])
    gt_cat = jnp.concatenate(
        [jnp.concatenate([hi_t[:, ch * c:(ch + 1) * c], lo_t[:, ch * c:(ch + 1) * c]], axis=1)
         for ch in range(n_ch)], axis=0)
    rev_all = _dot(gt_cat, rev_ref[...])

    for hd in range(GLA_HEADS):
        ks = slice(hd * dkh, (hd + 1) * dkh)
        vs = slice(hd * dvh, (hd + 1) * dvh)
        state = state_ref[hd]
        for ch in range(n_ch):
            rows = slice(ch * c, (ch + 1) * c)
            kcols = slice(ch * dk + hd * dkh, ch * dk + (hd + 1) * dkh)
            q = q_ref[rows, ks].astype(F32) * GLA_SCALE
            kt = kt_ref[ks, rows].astype(F32)
            v = v_ref[rows, vs]
            kt_bf = kt.astype(BF16)
            scores = _dot(q.astype(BF16), kt_bf) * mask_ref[n_lv]
            qd = q * jnp.exp(lg[rows, ks])
            scores = scores + _dot(qd.astype(BF16), kt_bf) * mask_ref[0]
            for lv in range(1, n_lv):
                qd = q * jnp.exp(fwd_all[(lv - 1) * c:lv * c, kcols])
                kd = kt * jnp.exp(rev_all[kcols, (lv - 1) * c:lv * c])
                scores = scores + _dot(qd.astype(BF16), kd.astype(BF16)) * mask_ref[lv]
            q_cum = q * jnp.exp(fwd_all[(n_lv - 1) * c:n_lv * c, kcols])
            out = _dot(q_cum.astype(BF16), state.astype(BF16)) + _dot(scores.astype(BF16), v)
            k_rest = kt * jnp.exp(rev_all[kcols, (n_lv - 1) * c:n_lv * c])
            keep = jnp.exp(rev_all[kcols, n_lv * c:(n_lv + 1) * c])
            keep = jnp.concatenate([keep] * (dvh // c), axis=1)
            state = keep * state + _dot(k_rest.astype(BF16), v)
            o = _rms(out, hn_ref[:, vs])
            og = og_ref[rows, vs].astype(F32)
            a_ref[rows, vs] = (o * (og * (1.0 / (1.0 + jnp.exp(-og))))).astype(BF16)
        state_ref[hd] = state


def _gla_chunk(q, kt, v, og, lg, lgt, head_norm):
    b, s, dk = q.shape
    dv = v.shape[-1]
    c = GLA_TILE * GLA_STEP_CHUNKS
    fwd, rev, masks = _gla_constants()
    row = lambda bi, i: (bi, i, 0)
    colmajor = lambda bi, i: (bi, 0, i)
    return pl.pallas_call(
        _gla_chunk_kernel,
        grid=(b, s // c),
        in_specs=[
            pl.BlockSpec((None, c, dk), row),
            pl.BlockSpec((None, dk, c), colmajor),
            pl.BlockSpec((None, c, dv), row),
            pl.BlockSpec((None, c, dv), row),
            pl.BlockSpec((None, c, dk), row),
            pl.BlockSpec((None, dk, c), colmajor),
            _resident(fwd.shape),
            _resident(rev.shape),
            _resident(masks.shape),
            _resident((1, dv)),
        ],
        out_specs=pl.BlockSpec((None, c, dv), row),
        out_shape=jax.ShapeDtypeStruct((b, s, dv), BF16),
        scratch_shapes=[pltpu.VMEM((GLA_HEADS, GLA_DK_HEAD, GLA_DV_HEAD), F32)],
        compiler_params=_params(("parallel", "arbitrary")),
        name="gla_chunk",
    )(q, kt, v, og, lg, lgt, fwd, rev, masks, head_norm)


def _post_kernel(a_ref, h_ref, wo_ref, g_ref, wup_ref, wdn_ref, o_ref, acc_ref):
    mixed = _dot(a_ref[...], wo_ref[...])
    h1 = h_ref[...] + _rms(mixed, g_ref[1:2, :])
    xn = _rms(h1, g_ref[2:3, :]).astype(BF16)
    for ci in range(D_FF // FF_CHUNK):
        cols = slice(ci * FF_CHUNK, (ci + 1) * FF_CHUNK)
        up = jnp.maximum(_dot(xn, wup_ref[:, cols]), 0.0)
        down = _dot((up * up).astype(BF16), wdn_ref[cols, :])
        if ci == 0:
            acc_ref[...] = down
        else:
            acc_ref[...] += down
    o_ref[...] = h1 + _rms(acc_ref[...], g_ref[3:4, :])


def _post(a, h, w_o, gains, w_up, w_down):
    b, s, d = h.shape
    tm = TOKEN_TILE
    row = lambda bi, i: (bi, i, 0)
    return pl.pallas_call(
        _post_kernel,
        grid=(b, s // tm),
        in_specs=[
            pl.BlockSpec((None, tm, d), row),
            pl.BlockSpec((None, tm, d), row),
            _resident((d, d)),
            _resident((4, d)),
            _resident((d, D_FF)),
            _resident((D_FF, d)),
        ],
        out_specs=pl.BlockSpec((None, tm, d), row),
        out_shape=jax.ShapeDtypeStruct((b, s, d), F32),
        scratch_shapes=[pltpu.VMEM((tm, d), F32)],
        compiler_params=_params(("parallel", "parallel")),
        name="post_ffn",
    )(a, h, w_o, gains, w_up, w_down)


def kernel(x, norm_gains, sb_w_qkv, sb_w_o, conv_w_in, conv_w, conv_w_out, gla_w_in,
           gla_w_gate_up, gla_b_gate, gla_head_norm, gla_w_o, ffn_w_up, ffn_w_down):
    depth = norm_gains.shape[0]
    h = x
    for i in range(depth):
        kind, j = i % 3, i // 3
        gains = norm_gains[i]
        pre_gain = gains[0:1]
        if kind == 0:
            q, kt, v = _sb_proj(h, pre_gain, sb_w_qkv[j].astype(BF16))
            a = _sb_attention(q, kt, v)
            w_o = sb_w_o[j]
        elif kind == 1:
            a = _conv_mixer(h, pre_gain, conv_w_in[j].astype(BF16), conv_w[j])
            w_o = conv_w_out[j]
        else:
            n_main = 2 * GLA_DK + 2 * GLA_DV
            w_in = gla_w_in[j]
            w_a = jnp.pad(w_in[:, n_main:], ((0, 0), (0, LANES - GLA_GATE_RANK))).astype(BF16)
            w_gu = jnp.pad(gla_w_gate_up[j], ((0, LANES - GLA_GATE_RANK), (0, 0))).astype(BF16)
            q, kt, v, og, lg, lgt = _gla_proj(h, pre_gain, w_in[:, :n_main].astype(BF16), w_a,
                                              w_gu, gla_b_gate[j][None, :])
            a = _gla_chunk(q, kt, v, og, lg, lgt, gla_head_norm[j].reshape(1, GLA_DV))
            w_o = gla_w_o[j]
        h = _post(a, h, w_o.astype(BF16), gains, ffn_w_up[i].astype(BF16),
                  ffn_w_down[i].astype(BF16))
    return h
```

```python
import functools

import numpy as np
import jax
import jax.numpy as jnp
from jax import lax
from jax.experimental import pallas as pl
from jax.experimental.pallas import tpu as pltpu

F32 = jnp.float32
BF16 = jnp.bfloat16

D_MODEL = 1024
D_FF = 4 * D_MODEL
RMS_EPS = 1e-6

SB_HEADS = 16
SB_HEAD_DIM = D_MODEL // SB_HEADS
SB_SCALE = SB_HEAD_DIM ** -0.5
CONV_WIDTH = 3
GLA_HEADS = 4
GLA_DK = D_MODEL // 2
GLA_DV = D_MODEL
GLA_DK_HEAD = GLA_DK // GLA_HEADS
GLA_DV_HEAD = GLA_DV // GLA_HEADS
GLA_GATE_RANK = 16
GLA_GATE_NORMALIZER = 16.0
GLA_SCALE = GLA_DK_HEAD ** -0.5

LANES = 128
SUBLANES = 8
VMEM_LIMIT_BYTES = 56 * 1024 * 1024

TOKEN_TILE = 512
FF_CHUNK = 1024
SB_TILE = 512
SB_SUB = 128
SB_BAND_SUBS = 3
SB_GROUP = 2
SB_TOP_ROWS = 64
SB_F32_EXP_ZERO = -105.0
GLA_TILE = 128
GLA_STEP_CHUNKS = 2
GLA_LEVELS = 7


def _rms(x, gain):
    ms = jnp.mean(x * x, axis=-1, keepdims=True)
    return x * lax.rsqrt(ms + RMS_EPS) * gain


def _neg_abs(z):
    bits = lax.bitcast_convert_type(z, jnp.uint32) | jnp.uint32(0x80000000)
    return lax.bitcast_convert_type(bits, F32)


def _log_sigmoid(z):
    return jnp.minimum(z, 0.0) - jnp.log(1.0 + jnp.exp(_neg_abs(z)))


def _split_bf16(x):
    hi = x.astype(BF16)
    lo = (x - hi.astype(F32)).astype(BF16)
    return hi, lo


def _dot(a, b):
    return jnp.dot(a, b, preferred_element_type=F32)


def _resident(shape):
    zeros = (0,) * len(shape)
    return pl.BlockSpec(shape, lambda *_: zeros, pipeline_mode=pl.Buffered(1))


def _params(semantics):
    return pltpu.CompilerParams(dimension_semantics=semantics,
                                vmem_limit_bytes=VMEM_LIMIT_BYTES)


def _sb_proj_kernel(x_ref, g_ref, w_ref, q_ref, kt_ref, v_ref):
    d = D_MODEL
    xn = _rms(x_ref[...], g_ref[...]).astype(BF16)
    q_ref[...] = (_dot(xn, w_ref[:, 0:d]) * SB_SCALE).astype(BF16)
    v_ref[...] = _dot(xn, w_ref[:, 2 * d:3 * d]).astype(BF16)
    k = _dot(xn, w_ref[:, d:2 * d])
    for hp in range(d // LANES):
        for kb in range(TOKEN_TILE // SB_SUB):
            blk = k[kb * SB_SUB:(kb + 1) * SB_SUB, hp * LANES:(hp + 1) * LANES]
            kt_ref[hp, kb] = blk.T.astype(BF16)


def _sb_proj(h, gain, w_qkv):
    b, s, d = h.shape
    n_hp = d // LANES
    n_kb = s // SB_SUB
    kb_per_step = TOKEN_TILE // SB_SUB
    return pl.pallas_call(
        _sb_proj_kernel,
        grid=(b, s // TOKEN_TILE),
        in_specs=[
            pl.BlockSpec((None, TOKEN_TILE, d), lambda bi, i: (bi, i, 0)),
            _resident((1, d)),
            _resident((d, 3 * d)),
        ],
        out_specs=[
            pl.BlockSpec((None, TOKEN_TILE, d), lambda bi, i: (bi, i, 0)),
            pl.BlockSpec((None, n_hp, kb_per_step, LANES, SB_SUB),
                         lambda bi, i: (bi, 0, i, 0, 0)),
            pl.BlockSpec((None, TOKEN_TILE, d), lambda bi, i: (bi, i, 0)),
        ],
        out_shape=[
            jax.ShapeDtypeStruct((b, s, d), BF16),
            jax.ShapeDtypeStruct((b, n_hp, n_kb, LANES, SB_SUB), BF16),
            jax.ShapeDtypeStruct((b, s, d), BF16),
        ],
        compiler_params=_params(("parallel", "parallel")),
        name="sb_proj",
    )(h, gain, w_qkv)


def _sb_suffix_matrix():
    t = SB_SUB
    m = np.arange(t)[:, None]
    j = np.arange(t)[None, :]
    half = np.concatenate([(m > j).astype(np.float32), np.ones((t, t), np.float32)], axis=1)
    return jnp.asarray(np.concatenate([half, half], axis=0), dtype=BF16)


def _sb_attn_kernel(q_ref, kt_ref, v_ref, u_ref, o_ref, acc_ref, c_ref):
    t = SB_TILE
    sub = SB_SUB
    n_rb = t // sub
    band = SB_BAND_SUBS
    qi = pl.program_id(2)
    q = q_ref[...]
    lane = lax.broadcasted_iota(jnp.int32, (t, LANES), 1)
    first = lane < SB_HEAD_DIM
    zero = jnp.zeros_like(q)
    q_heads = (jnp.where(first, q, zero), jnp.where(first, zero, q))
    u = u_ref[...]

    def chunk(qh, kblk, vblk, c, masks):
        n_sub = kblk.shape[1] // sub
        rows = qh.shape[0]
        z = _dot(qh, kblk)
        log_beta = _log_sigmoid(z)
        log_1m = log_beta - z
        parts = []
        for s in range(n_sub):
            part = log_1m[:, s * sub:(s + 1) * sub]
            if masks[s] is not None:
                part = jnp.where(masks[s], part, 0.0)
            hi, lo = _split_bf16(part)
            parts.append(jnp.concatenate([hi, lo], axis=1))
        s2 = _dot(jnp.concatenate(parts, axis=0), u)
        ws = [None] * n_sub
        for s in reversed(range(n_sub)):
            blk = s2[s * rows:(s + 1) * rows]
            w = jnp.exp(log_beta[:, s * sub:(s + 1) * sub] + blk[:, :sub] + c)
            if masks[s] is not None:
                w = jnp.where(masks[s], w, 0.0)
            ws[s] = w.astype(BF16)
            c = c + blk[:, sub:]
        return _dot(jnp.concatenate(ws, axis=1), vblk), c

    @pl.when(qi == 0)
    def _():
        row = lax.broadcasted_iota(jnp.int32, (t, sub), 0)
        col = lax.broadcasted_iota(jnp.int32, (t, sub), 1)
        masks = [col + s * sub < row for s in range(n_rb)]
        kd = jnp.concatenate([kt_ref[s] for s in range(n_rb)], axis=1)
        vd = v_ref[0:t, :]
        for hd in range(2):
            pv, _ = chunk(q_heads[hd], kd, vd, jnp.zeros((t, sub), F32), masks)
            acc_ref[hd] = pv

    @pl.when(qi > 0)
    def _():
        row = lax.broadcasted_iota(jnp.int32, (sub, sub), 0)
        col = lax.broadcasted_iota(jnp.int32, (sub, sub), 1)
        diag = col < row
        grp = SB_GROUP
        win = band + grp - 1
        no_weight = jnp.zeros((sub, sub), BF16)
        top = SB_TOP_ROWS
        tile_rows = [top] + [sub] * (band - 1)
        parts = []
        log_betas = []
        for hd in range(2):
            for p in range(n_rb // grp):
                first_sub = qi * n_rb + p * grp - band + 1
                kwin = jnp.concatenate([kt_ref[first_sub + s] for s in range(win)], axis=1)
                z = _dot(q_heads[hd][p * grp * sub:(p + 1) * grp * sub], kwin)
                for rr in range(grp):
                    for s in range(band):
                        zt = z[rr * sub:rr * sub + tile_rows[s], (rr + s) * sub:(rr + s + 1) * sub]
                        log_beta = _log_sigmoid(zt)
                        part = log_beta - zt
                        if s == band - 1:
                            part = jnp.where(diag, part, 0.0)
                        hi, lo = _split_bf16(part)
                        parts.append(jnp.concatenate([hi, lo], axis=1))
                        log_betas.append(log_beta)
        s2 = _dot(jnp.concatenate(parts, axis=0), u)
        offsets = np.cumsum([0] + [tile_rows[i % band] for i in range(len(parts))])
        for hd in range(2):
            for p in range(n_rb // grp):
                first_sub = qi * n_rb + p * grp - band + 1
                w_rows = []
                for rr in range(grp):
                    r = p * grp + rr
                    c = jnp.zeros((sub, sub), F32)
                    ws = [None] * band
                    for s in reversed(range(band)):
                        tile = (hd * n_rb + r) * band + s
                        n = tile_rows[s]
                        blk = s2[offsets[tile]:offsets[tile] + n]
                        w = jnp.exp(log_betas[tile] + blk[:, :sub] + c[:n])
                        if s == band - 1:
                            w = jnp.where(diag, w, 0.0)
                        w = w.astype(BF16)
                        c_new = c[:n] + blk[:, sub:]
                        if n < sub:
                            w = jnp.concatenate([w, jnp.zeros((sub - n, sub), BF16)], axis=0)
                            c_new = jnp.concatenate([c_new, c[n:]], axis=0)
                        ws[s] = w
                        c = c_new
                    c_ref[hd, r * sub:(r + 1) * sub, :] = c
                    w_rows.append(jnp.concatenate(
                        [no_weight] * rr + ws + [no_weight] * (grp - 1 - rr), axis=1))
                vwin = v_ref[pl.ds(pl.multiple_of(first_sub * sub, sub), win * sub), :]
                acc_ref[hd, p * grp * sub:(p + 1) * grp * sub, :] = _dot(
                    jnp.concatenate(w_rows, axis=0), vwin)

        row_id = lax.broadcasted_iota(jnp.int32, (t, sub), 0)
        seen_from = row_id // sub - band + 1 + jnp.where(row_id % sub >= top, 1, 0)

        def more_keys_matter(carry):
            k, c_max = carry
            return jnp.logical_and(k >= 0, c_max > SB_F32_EXP_ZERO)

        def body(carry):
            k, _ = carry
            not_yet_seen = [k - qi * n_rb < seen_from]
            kblk = kt_ref[k]
            vblk = v_ref[pl.ds(pl.multiple_of(k * sub, sub), sub), :]
            for hd in range(2):
                pv, c = chunk(q_heads[hd], kblk, vblk, c_ref[hd], not_yet_seen)
                acc_ref[hd] += pv
                c_ref[hd] = c
            return k - 1, jnp.max(c_ref[...])

        k_start = qi * n_rb + n_rb - band
        lax.while_loop(more_keys_matter, body, (k_start, jnp.max(c_ref[...])))

    o_ref[...] = jnp.where(first, acc_ref[0], acc_ref[1]).astype(BF16)


def _sb_attention(q, kt, v):
    b, s, d = q.shape
    n_hp = d // LANES
    n_kb = s // SB_SUB
    t = SB_TILE
    return pl.pallas_call(
        _sb_attn_kernel,
        grid=(b, n_hp, s // t),
        in_specs=[
            pl.BlockSpec((None, t, LANES), lambda bi, hp, qi: (bi, qi, hp)),
            pl.BlockSpec((None, None, n_kb, LANES, SB_SUB),
                         lambda bi, hp, qi: (bi, hp, 0, 0, 0)),
            pl.BlockSpec((None, s, LANES), lambda bi, hp, qi: (bi, 0, hp)),
            _resident((2 * SB_SUB, 2 * SB_SUB)),
        ],
        out_specs=pl.BlockSpec((None, t, LANES), lambda bi, hp, qi: (bi, qi, hp)),
        out_shape=jax.ShapeDtypeStruct((b, s, d), BF16),
        scratch_shapes=[pltpu.VMEM((2, t, LANES), F32), pltpu.VMEM((2, t, SB_SUB), F32)],
        compiler_params=_params(("parallel", "parallel", "arbitrary")),
        name="sb_attn",
    )(q, kt, v, _sb_suffix_matrix())


def _conv_kernel(x_ref, xp_ref, g_ref, w_ref, cw_ref, a_ref, hbuf_ref):
    d = D_MODEL
    tm = TOKEN_TILE
    i = pl.program_id(1)
    gain = g_ref[...]
    xn = _rms(x_ref[...], gain).astype(BF16)
    xpn = _rms(xp_ref[...], gain).astype(BF16)
    hbuf_ref[SUBLANES:, :] = _dot(xn, w_ref[:, d:2 * d]) * _dot(xn, w_ref[:, 2 * d:3 * d])
    h_prev = _dot(xpn, w_ref[:, d:2 * d]) * _dot(xpn, w_ref[:, 2 * d:3 * d])
    hbuf_ref[0:SUBLANES, :] = jnp.where(i > 0, h_prev, 0.0)
    conv = cw_ref[0:1, :] * hbuf_ref[pl.ds(SUBLANES - 2, tm), :]
    conv = conv + cw_ref[1:2, :] * hbuf_ref[pl.ds(SUBLANES - 1, tm), :]
    conv = conv + cw_ref[2:3, :] * hbuf_ref[pl.ds(SUBLANES, tm), :]
    a_ref[...] = (_dot(xn, w_ref[:, 0:d]) * conv).astype(BF16)


def _conv_mixer(h, gain, w_in, conv_w):
    b, s, d = h.shape
    tm = TOKEN_TILE
    rows_per_tile = tm // SUBLANES
    return pl.pallas_call(
        _conv_kernel,
        grid=(b, s // tm),
        in_specs=[
            pl.BlockSpec((None, tm, d), lambda bi, i: (bi, i, 0)),
            pl.BlockSpec((None, SUBLANES, d),
                         lambda bi, i: (bi, jnp.maximum(i * rows_per_tile - 1, 0), 0)),
            _resident((1, d)),
            _resident((d, 3 * d)),
            _resident((CONV_WIDTH, d)),
        ],
        out_specs=pl.BlockSpec((None, tm, d), lambda bi, i: (bi, i, 0)),
        out_shape=jax.ShapeDtypeStruct((b, s, d), BF16),
        scratch_shapes=[pltpu.VMEM((tm + SUBLANES, d), F32)],
        compiler_params=_params(("parallel", "parallel")),
        name="conv_mixer",
    )(h, h, gain, w_in, conv_w)


def _gla_proj_kernel(x_ref, g_ref, w_ref, wa_ref, wgu_ref, bg_ref,
                     q_ref, kt_ref, v_ref, og_ref, lg_ref, lgt_ref):
    dk, dv = GLA_DK, GLA_DV
    xn = _rms(x_ref[...], g_ref[...]).astype(BF16)
    q_ref[...] = _dot(xn, w_ref[:, 0:dk]).astype(BF16)
    kt_ref[...] = _dot(xn, w_ref[:, dk:2 * dk]).T.astype(BF16)
    v_ref[...] = _dot(xn, w_ref[:, 2 * dk:2 * dk + dv]).astype(BF16)
    og_ref[...] = _dot(xn, w_ref[:, 2 * dk + dv:2 * dk + 2 * dv]).astype(BF16)
    a_low = _dot(xn, wa_ref[...]).astype(BF16)
    pre = _dot(a_low, wgu_ref[...]) + bg_ref[...]
    lg = _log_sigmoid(pre) * (1.0 / GLA_GATE_NORMALIZER)
    lg_ref[...] = lg
    lgt_ref[...] = lg.T


def _gla_proj(h, gain, w_main, w_a, w_gu, b_gate):
    b, s, d = h.shape
    tm = TOKEN_TILE
    dk, dv = GLA_DK, GLA_DV
    row = lambda bi, i: (bi, i, 0)
    colmajor = lambda bi, i: (bi, 0, i)
    return pl.pallas_call(
        _gla_proj_kernel,
        grid=(b, s // tm),
        in_specs=[
            pl.BlockSpec((None, tm, d), row),
            _resident((1, d)),
            _resident((d, 2 * dk + 2 * dv)),
            _resident((d, LANES)),
            _resident((LANES, dk)),
            _resident((1, dk)),
        ],
        out_specs=[
            pl.BlockSpec((None, tm, dk), row),
            pl.BlockSpec((None, dk, tm), colmajor),
            pl.BlockSpec((None, tm, dv), row),
            pl.BlockSpec((None, tm, dv), row),
            pl.BlockSpec((None, tm, dk), row),
            pl.BlockSpec((None, dk, tm), colmajor),
        ],
        out_shape=[
            jax.ShapeDtypeStruct((b, s, dk), BF16),
            jax.ShapeDtypeStruct((b, dk, s), BF16),
            jax.ShapeDtypeStruct((b, s, dv), BF16),
            jax.ShapeDtypeStruct((b, s, dv), BF16),
            jax.ShapeDtypeStruct((b, s, dk), F32),
            jax.ShapeDtypeStruct((b, dk, s), F32),
        ],
        compiler_params=_params(("parallel", "parallel")),
        name="gla_proj",
    )(h, gain, w_main, w_a, w_gu, b_gate)


def _gla_constants():
    c = GLA_TILE
    i = np.arange(c)[:, None]
    m = np.arange(c)[None, :]
    fwd, rev, masks = [], [], []
    for level in range(1, GLA_LEVELS + 1):
        n = 1 << level
        same = (i // n) == (m // n)
        fwd.append((same & (m <= i)).astype(np.float32))
        rev.append((same & (m > i)).astype(np.float32).T)
    rev.append(np.ones((c, c), np.float32))
    for level in range(GLA_LEVELS):
        n = 1 << level
        masks.append(((i // (2 * n) == m // (2 * n)) & ((i // n) % 2 == 1)
                      & ((m // n) % 2 == 0)).astype(np.float32))
    masks.append((i == m).astype(np.float32))
    fwd = np.concatenate(fwd, axis=0)
    rev = np.concatenate(rev, axis=1)
    fwd2 = np.concatenate([fwd, fwd], axis=1)
    rev2 = np.concatenate([rev, rev], axis=0)
    return (jnp.asarray(fwd2, dtype=BF16), jnp.asarray(rev2, dtype=BF16),
            jnp.asarray(np.stack(masks), dtype=F32))


def _gla_chunk_kernel(q_ref, kt_ref, v_ref, og_ref, lg_ref, lgt_ref, fwd_ref, rev_ref,
                      mask_ref, hn_ref, a_ref, state_ref):
    c = GLA_TILE
    n_ch = GLA_STEP_CHUNKS
    dk = GLA_DK
    dkh, dvh = GLA_DK_HEAD, GLA_DV_HEAD
    n_lv = GLA_LEVELS

    @pl.when(pl.program_id(1) == 0)
    def _():
        state_ref[...] = jnp.zeros_like(state_ref)

    lg = lg_ref[...]
    hi, lo = _split_bf16(lg)
    g_cat = jnp.concatenate(
        [jnp.concatenate([hi[ch * c:(ch + 1) * c], lo[ch * c:(ch + 1) * c]], axis=0)
         for ch in range(n_ch)], axis=1)
    fwd_all = _dot(fwd_ref[...], g_cat)
    hi_t, lo_t = _split_bf16(lgt_ref[...])
    gt_cat = jnp.concatenate(
        [jnp.concatenate([hi_t[:, ch * c:(ch + 1) * c], lo_t[:, ch * c:(ch + 1) * c]], axis=1)
         for ch in range(n_ch)], axis=0)
    rev_all = _dot(gt_cat, rev_ref[...])

    for hd in range(GLA_HEADS):
        ks = slice(hd * dkh, (hd + 1) * dkh)
        vs = slice(hd * dvh, (hd + 1) * dvh)
        state = state_ref[hd]
        for ch in range(n_ch):
            rows = slice(ch * c, (ch + 1) * c)
            kcols = slice(ch * dk + hd * dkh, ch * dk + (hd + 1) * dkh)
            q = q_ref[rows, ks].astype(F32) * GLA_SCALE
            kt = kt_ref[ks, rows].astype(F32)
            v = v_ref[rows, vs]
            kt_bf = kt.astype(BF16)
            scores = _dot(q.astype(BF16), kt_bf) * mask_ref[n_lv]
            qd = q * jnp.exp(lg[rows, ks])
            scores = scores + _dot(qd.astype(BF16), kt_bf) * mask_ref[0]
            for lv in range(1, n_lv):
                qd = q * jnp.exp(fwd_all[(lv - 1) * c:lv * c, kcols])
                kd = kt * jnp.exp(rev_all[kcols, (lv - 1) * c:lv * c])
                scores = scores + _dot(qd.astype(BF16), kd.astype(BF16)) * mask_ref[lv]
            q_cum = q * jnp.exp(fwd_all[(n_lv - 1) * c:n_lv * c, kcols])
            out = _dot(q_cum.astype(BF16), state.astype(BF16)) + _dot(scores.astype(BF16), v)
            k_rest = kt * jnp.exp(rev_all[kcols, (n_lv - 1) * c:n_lv * c])
            keep = jnp.exp(rev_all[kcols, n_lv * c:(n_lv + 1) * c])
            keep = jnp.concatenate([keep] * (dvh // c), axis=1)
            state = keep * state + _dot(k_rest.astype(BF16), v)
            o = _rms(out, hn_ref[:, vs])
            og = og_ref[rows, vs].astype(F32)
            a_ref[rows, vs] = (o * (og * (1.0 / (1.0 + jnp.exp(-og))))).astype(BF16)
        state_ref[hd] = state


def _gla_chunk(q, kt, v, og, lg, lgt, head_norm):
    b, s, dk = q.shape
    dv = v.shape[-1]
    c = GLA_TILE * GLA_STEP_CHUNKS
    fwd, rev, masks = _gla_constants()
    row = lambda bi, i: (bi, i, 0)
    colmajor = lambda bi, i: (bi, 0, i)
    return pl.pallas_call(
        _gla_chunk_kernel,
        grid=(b, s // c),
        in_specs=[
            pl.BlockSpec((None, c, dk), row),
            pl.BlockSpec((None, dk, c), colmajor),
            pl.BlockSpec((None, c, dv), row),
            pl.BlockSpec((None, c, dv), row),
            pl.BlockSpec((None, c, dk), row),
            pl.BlockSpec((None, dk, c), colmajor),
            _resident(fwd.shape),
            _resident(rev.shape),
            _resident(masks.shape),
            _resident((1, dv)),
        ],
        out_specs=pl.BlockSpec((None, c, dv), row),
        out_shape=jax.ShapeDtypeStruct((b, s, dv), BF16),
        scratch_shapes=[pltpu.VMEM((GLA_HEADS, GLA_DK_HEAD, GLA_DV_HEAD), F32)],
        compiler_params=_params(("parallel", "arbitrary")),
        name="gla_chunk",
    )(q, kt, v, og, lg, lgt, fwd, rev, masks, head_norm)


def _post_kernel(a_ref, h_ref, wo_ref, g_ref, wup_ref, wdn_ref, o_ref, acc_ref):
    half = TOKEN_TILE // 2
    halves = (slice(0, half), slice(half, TOKEN_TILE))
    h1 = []
    xn = []
    for rows in halves:
        mixed = _dot(a_ref[rows, :], wo_ref[...])
        h1.append(h_ref[rows, :] + _rms(mixed, g_ref[1:2, :]))
        xn.append(_rms(h1[-1], g_ref[2:3, :]).astype(BF16))
    xn_full = jnp.concatenate(xn, axis=0)
    n_chunks = D_FF // FF_CHUNK
    for ci in range(n_chunks):
        cols = slice(ci * FF_CHUNK, (ci + 1) * FF_CHUNK)
        if ci == 0:
            up = jnp.concatenate([_dot(part, wup_ref[:, cols]) for part in xn], axis=0)
        else:
            up = _dot(xn_full, wup_ref[:, cols])
        up = jnp.maximum(up, 0.0)
        act = (up * up).astype(BF16)
        if ci == 0:
            acc_ref[...] = _dot(act, wdn_ref[cols, :])
        elif ci < n_chunks - 1:
            acc_ref[...] += _dot(act, wdn_ref[cols, :])
        else:
            for hi, rows in enumerate(halves):
                ffn = acc_ref[rows, :] + _dot(act[rows, :], wdn_ref[cols, :])
                o_ref[rows, :] = h1[hi] + _rms(ffn, g_ref[3:4, :])


def _post(a, h, w_o, gains, w_up, w_down):
    b, s, d = h.shape
    tm = TOKEN_TILE
    row = lambda bi, i: (bi, i, 0)
    return pl.pallas_call(
        _post_kernel,
        grid=(b, s // tm),
        in_specs=[
            pl.BlockSpec((None, tm, d), row),
            pl.BlockSpec((None, tm, d), row),
            _resident((d, d)),
            _resident((4, d)),
            _resident((d, D_FF)),
            _resident((D_FF, d)),
        ],
        out_specs=pl.BlockSpec((None, tm, d), row),
        out_shape=jax.ShapeDtypeStruct((b, s, d), F32),
        scratch_shapes=[pltpu.VMEM((tm, d), F32)],
        compiler_params=_params(("parallel", "parallel")),
        name="post_ffn",
    )(a, h, w_o, gains, w_up, w_down)


def kernel(x, norm_gains, sb_w_qkv, sb_w_o, conv_w_in, conv_w, conv_w_out, gla_w_in,
           gla_w_gate_up, gla_b_gate, gla_head_norm, gla_w_o, ffn_w_up, ffn_w_down):
    depth = norm_gains.shape[0]
    h = x
    for i in range(depth):
        kind, j = i % 3, i // 3
        gains = norm_gains[i]
        pre_gain = gains[0:1]
        if kind == 0:
            q, kt, v = _sb_proj(h, pre_gain, sb_w_qkv[j].astype(BF16))
            a = _sb_attention(q, kt, v)
            w_o = sb_w_o[j]
        elif kind == 1:
            a = _conv_mixer(h, pre_gain, conv_w_in[j].astype(BF16), conv_w[j])
            w_o = conv_w_out[j]
        else:
            n_main = 2 * GLA_DK + 2 * GLA_DV
            w_in = gla_w_in[j]
            w_a = jnp.pad(w_in[:, n_main:], ((0, 0), (0, LANES - GLA_GATE_RANK))).astype(BF16)
            w_gu = jnp.pad(gla_w_gate_up[j], ((0, LANES - GLA_GATE_RANK), (0, 0))).astype(BF16)
            q, kt, v, og, lg, lgt = _gla_proj(h, pre_gain, w_in[:, :n_main].astype(BF16), w_a,
                                              w_gu, gla_b_gate[j][None, :])
            a = _gla_chunk(q, kt, v, og, lg, lgt, gla_head_norm[j].reshape(1, GLA_DV))
            w_o = gla_w_o[j]
        h = _post(a, h, w_o.astype(BF16), gains, ffn_w_up[i].astype(BF16),
                  ffn_w_down[i].astype(BF16))
    return h
```

```python
import functools

import numpy as np
import jax
import jax.numpy as jnp
from jax import lax
from jax.experimental import pallas as pl
from jax.experimental.pallas import tpu as pltpu

F32 = jnp.float32
BF16 = jnp.bfloat16

D_MODEL = 1024
D_FF = 4 * D_MODEL
RMS_EPS = 1e-6

SB_HEADS = 16
SB_HEAD_DIM = D_MODEL // SB_HEADS
SB_SCALE = SB_HEAD_DIM ** -0.5
CONV_WIDTH = 3
GLA_HEADS = 4
GLA_DK = D_MODEL // 2
GLA_DV = D_MODEL
GLA_DK_HEAD = GLA_DK // GLA_HEADS
GLA_DV_HEAD = GLA_DV // GLA_HEADS
GLA_GATE_RANK = 16
GLA_GATE_NORMALIZER = 16.0
GLA_SCALE = GLA_DK_HEAD ** -0.5

LANES = 128
SUBLANES = 8
VMEM_LIMIT_BYTES = 56 * 1024 * 1024

TOKEN_TILE = 512
FF_CHUNK = 1024
SB_TILE = 512
SB_SUB = 128
SB_BAND_SUBS = 3
SB_GROUP = 2
SB_F32_EXP_ZERO = -105.0
GLA_TILE = 128
GLA_STEP_CHUNKS = 2
GLA_LEVELS = 7


def _rms(x, gain):
    ms = jnp.mean(x * x, axis=-1, keepdims=True)
    return x * lax.rsqrt(ms + RMS_EPS) * gain


def _neg_abs(z):
    bits = lax.bitcast_convert_type(z, jnp.uint32) | jnp.uint32(0x80000000)
    return lax.bitcast_convert_type(bits, F32)


def _log_sigmoid(z):
    return jnp.minimum(z, 0.0) - jnp.log(1.0 + jnp.exp(_neg_abs(z)))


def _split_bf16(x):
    hi = x.astype(BF16)
    lo = (x - hi.astype(F32)).astype(BF16)
    return hi, lo


def _dot(a, b):
    return jnp.dot(a, b, preferred_element_type=F32)


def _resident(shape):
    zeros = (0,) * len(shape)
    return pl.BlockSpec(shape, lambda *_: zeros, pipeline_mode=pl.Buffered(1))


def _params(semantics):
    return pltpu.CompilerParams(dimension_semantics=semantics,
                                vmem_limit_bytes=VMEM_LIMIT_BYTES)


def _sb_proj_kernel(x_ref, g_ref, w_ref, q_ref, kt_ref, v_ref):
    d = D_MODEL
    xn = _rms(x_ref[...], g_ref[...]).astype(BF16)
    q_ref[...] = (_dot(xn, w_ref[:, 0:d]) * SB_SCALE).astype(BF16)
    v_ref[...] = _dot(xn, w_ref[:, 2 * d:3 * d]).astype(BF16)
    k = _dot(xn, w_ref[:, d:2 * d])
    for hp in range(d // LANES):
        for kb in range(TOKEN_TILE // SB_SUB):
            blk = k[kb * SB_SUB:(kb + 1) * SB_SUB, hp * LANES:(hp + 1) * LANES]
            kt_ref[hp, kb] = blk.T.astype(BF16)


def _sb_proj(h, gain, w_qkv):
    b, s, d = h.shape
    n_hp = d // LANES
    n_kb = s // SB_SUB
    kb_per_step = TOKEN_TILE // SB_SUB
    return pl.pallas_call(
        _sb_proj_kernel,
        grid=(b, s // TOKEN_TILE),
        in_specs=[
            pl.BlockSpec((None, TOKEN_TILE, d), lambda bi, i: (bi, i, 0)),
            _resident((1, d)),
            _resident((d, 3 * d)),
        ],
        out_specs=[
            pl.BlockSpec((None, TOKEN_TILE, d), lambda bi, i: (bi, i, 0)),
            pl.BlockSpec((None, n_hp, kb_per_step, LANES, SB_SUB),
                         lambda bi, i: (bi, 0, i, 0, 0)),
            pl.BlockSpec((None, TOKEN_TILE, d), lambda bi, i: (bi, i, 0)),
        ],
        out_shape=[
            jax.ShapeDtypeStruct((b, s, d), BF16),
            jax.ShapeDtypeStruct((b, n_hp, n_kb, LANES, SB_SUB), BF16),
            jax.ShapeDtypeStruct((b, s, d), BF16),
        ],
        compiler_params=_params(("parallel", "parallel")),
        name="sb_proj",
    )(h, gain, w_qkv)


def _sb_suffix_matrix():
    t = SB_SUB
    m = np.arange(t)[:, None]
    j = np.arange(t)[None, :]
    half = np.concatenate([(m > j).astype(np.float32), np.ones((t, t), np.float32)], axis=1)
    return jnp.asarray(np.concatenate([half, half], axis=0), dtype=BF16)


def _sb_attn_kernel(q_ref, kt_ref, v_ref, u_ref, o_ref, acc_ref, c_ref):
    t = SB_TILE
    sub = SB_SUB
    n_rb = t // sub
    band = SB_BAND_SUBS
    qi = pl.program_id(2)
    q = q_ref[...]
    lane = lax.broadcasted_iota(jnp.int32, (t, LANES), 1)
    first = lane < SB_HEAD_DIM
    zero = jnp.zeros_like(q)
    q_heads = (jnp.where(first, q, zero), jnp.where(first, zero, q))
    u = u_ref[...]

    def chunk(qh, kblk, vblk, c, masks):
        n_sub = kblk.shape[1] // sub
        rows = qh.shape[0]
        z = _dot(qh, kblk)
        log_beta = _log_sigmoid(z)
        log_1m = log_beta - z
        parts = []
        for s in range(n_sub):
            part = log_1m[:, s * sub:(s + 1) * sub]
            if masks[s] is not None:
                part = jnp.where(masks[s], part, 0.0)
            hi, lo = _split_bf16(part)
            parts.append(jnp.concatenate([hi, lo], axis=1))
        s2 = _dot(jnp.concatenate(parts, axis=0), u)
        ws = [None] * n_sub
        for s in reversed(range(n_sub)):
            blk = s2[s * rows:(s + 1) * rows]
            w = jnp.exp(log_beta[:, s * sub:(s + 1) * sub] + blk[:, :sub] + c)
            if masks[s] is not None:
                w = jnp.where(masks[s], w, 0.0)
            ws[s] = w.astype(BF16)
            c = c + blk[:, sub:]
        return _dot(jnp.concatenate(ws, axis=1), vblk), c

    @pl.when(qi == 0)
    def _():
        row = lax.broadcasted_iota(jnp.int32, (t, sub), 0)
        col = lax.broadcasted_iota(jnp.int32, (t, sub), 1)
        masks = [col + s * sub < row for s in range(n_rb)]
        kd = jnp.concatenate([kt_ref[s] for s in range(n_rb)], axis=1)
        vd = v_ref[0:t, :]
        for hd in range(2):
            pv, _ = chunk(q_heads[hd], kd, vd, jnp.zeros((t, sub), F32), masks)
            acc_ref[hd] = pv

    @pl.when(qi > 0)
    def _():
        row = lax.broadcasted_iota(jnp.int32, (sub, sub), 0)
        col = lax.broadcasted_iota(jnp.int32, (sub, sub), 1)
        diag = col < row
        grp = SB_GROUP
        win = band + grp - 1
        no_weight = jnp.zeros((sub, sub), BF16)
        pairs = n_rb // grp
        grp_rows = grp * sub
        parts = []
        log_betas = {}
        for p in range(pairs):
            first_sub = qi * n_rb + p * grp - band + 1
            kwin = jnp.concatenate([kt_ref[first_sub + s] for s in range(win)], axis=1)
            q_both = jnp.concatenate(
                [q_heads[hd][p * grp_rows:(p + 1) * grp_rows] for hd in range(2)], axis=0)
            z = _dot(q_both, kwin)
            for hd in range(2):
                for rr in range(grp):
                    top = hd * grp_rows + rr * sub
                    for s in range(band):
                        zt = z[top:top + sub, (rr + s) * sub:(rr + s + 1) * sub]
                        log_beta = _log_sigmoid(zt)
                        part = log_beta - zt
                        if s == band - 1:
                            part = jnp.where(diag, part, 0.0)
                        hi, lo = _split_bf16(part)
                        log_betas[p, hd, rr, s] = (len(parts), log_beta)
                        parts.append(jnp.concatenate([hi, lo], axis=1))
        s2 = _dot(jnp.concatenate(parts, axis=0), u)
        for p in range(pairs):
            first_sub = qi * n_rb + p * grp - band + 1
            w_rows = []
            for hd in range(2):
                for rr in range(grp):
                    r = p * grp + rr
                    c = jnp.zeros((sub, sub), F32)
                    ws = [None] * band
                    for s in reversed(range(band)):
                        index, log_beta = log_betas[p, hd, rr, s]
                        blk = s2[index * sub:(index + 1) * sub]
                        w = jnp.exp(log_beta + blk[:, :sub] + c)
                        if s == band - 1:
                            w = jnp.where(diag, w, 0.0)
                        ws[s] = w.astype(BF16)
                        c = c + blk[:, sub:]
                    c_ref[hd, r * sub:(r + 1) * sub, :] = c
                    w_rows.append(jnp.concatenate(
                        [no_weight] * rr + ws + [no_weight] * (grp - 1 - rr), axis=1))
            vwin = v_ref[pl.ds(pl.multiple_of(first_sub * sub, sub), win * sub), :]
            pv = _dot(jnp.concatenate(w_rows, axis=0), vwin)
            for hd in range(2):
                acc_ref[hd, p * grp_rows:(p + 1) * grp_rows, :] = pv[hd * grp_rows:(hd + 1) * grp_rows]

        row_blk = lax.broadcasted_iota(jnp.int32, (t, sub), 0) // sub

        def more_keys_matter(carry):
            k, c_max = carry
            return jnp.logical_and(k >= 0, c_max > SB_F32_EXP_ZERO)

        def body(carry):
            k, _ = carry
            not_yet_seen = [row_blk >= k - qi * n_rb + band]
            kblk = kt_ref[k]
            vblk = v_ref[pl.ds(pl.multiple_of(k * sub, sub), sub), :]
            for hd in range(2):
                pv, c = chunk(q_heads[hd], kblk, vblk, c_ref[hd], not_yet_seen)
                acc_ref[hd] += pv
                c_ref[hd] = c
            return k - 1, jnp.max(c_ref[...])

        k_start = qi * n_rb + n_rb - 1 - band
        lax.while_loop(more_keys_matter, body, (k_start, jnp.max(c_ref[...])))

    o_ref[...] = jnp.where(first, acc_ref[0], acc_ref[1]).astype(BF16)


def _sb_attention(q, kt, v):
    b, s, d = q.shape
    n_hp = d // LANES
    n_kb = s // SB_SUB
    t = SB_TILE
    return pl.pallas_call(
        _sb_attn_kernel,
        grid=(b, n_hp, s // t),
        in_specs=[
            pl.BlockSpec((None, t, LANES), lambda bi, hp, qi: (bi, qi, hp)),
            pl.BlockSpec((None, None, n_kb, LANES, SB_SUB),
                         lambda bi, hp, qi: (bi, hp, 0, 0, 0)),
            pl.BlockSpec((None, s, LANES), lambda bi, hp, qi: (bi, 0, hp)),
            _resident((2 * SB_SUB, 2 * SB_SUB)),
        ],
        out_specs=pl.BlockSpec((None, t, LANES), lambda bi, hp, qi: (bi, qi, hp)),
        out_shape=jax.ShapeDtypeStruct((b, s, d), BF16),
        scratch_shapes=[pltpu.VMEM((2, t, LANES), F32), pltpu.VMEM((2, t, SB_SUB), F32)],
        compiler_params=_params(("parallel", "parallel", "arbitrary")),
        name="sb_attn",
    )(q, kt, v, _sb_suffix_matrix())


def _conv_kernel(x_ref, xp_ref, g_ref, w_ref, cw_ref, a_ref, hbuf_ref):
    d = D_MODEL
    tm = TOKEN_TILE
    i = pl.program_id(1)
    gain = g_ref[...]
    xn = _rms(x_ref[...], gain).astype(BF16)
    xpn = _rms(xp_ref[...], gain).astype(BF16)
    hbuf_ref[SUBLANES:, :] = _dot(xn, w_ref[:, d:2 * d]) * _dot(xn, w_ref[:, 2 * d:3 * d])
    h_prev = _dot(xpn, w_ref[:, d:2 * d]) * _dot(xpn, w_ref[:, 2 * d:3 * d])
    hbuf_ref[0:SUBLANES, :] = jnp.where(i > 0, h_prev, 0.0)
    conv = cw_ref[0:1, :] * hbuf_ref[pl.ds(SUBLANES - 2, tm), :]
    conv = conv + cw_ref[1:2, :] * hbuf_ref[pl.ds(SUBLANES - 1, tm), :]
    conv = conv + cw_ref[2:3, :] * hbuf_ref[pl.ds(SUBLANES, tm), :]
    a_ref[...] = (_dot(xn, w_ref[:, 0:d]) * conv).astype(BF16)


def _conv_mixer(h, gain, w_in, conv_w):
    b, s, d = h.shape
    tm = TOKEN_TILE
    rows_per_tile = tm // SUBLANES
    return pl.pallas_call(
        _conv_kernel,
        grid=(b, s // tm),
        in_specs=[
            pl.BlockSpec((None, tm, d), lambda bi, i: (bi, i, 0)),
            pl.BlockSpec((None, SUBLANES, d),
                         lambda bi, i: (bi, jnp.maximum(i * rows_per_tile - 1, 0), 0)),
            _resident((1, d)),
            _resident((d, 3 * d)),
            _resident((CONV_WIDTH, d)),
        ],
        out_specs=pl.BlockSpec((None, tm, d), lambda bi, i: (bi, i, 0)),
        out_shape=jax.ShapeDtypeStruct((b, s, d), BF16),
        scratch_shapes=[pltpu.VMEM((tm + SUBLANES, d), F32)],
        compiler_params=_params(("parallel", "parallel")),
        name="conv_mixer",
    )(h, h, gain, w_in, conv_w)


def _gla_proj_kernel(x_ref, g_ref, w_ref, wa_ref, wgu_ref, bg_ref,
                     q_ref, kt_ref, v_ref, og_ref, lg_ref, lgt_ref):
    dk, dv = GLA_DK, GLA_DV
    xn = _rms(x_ref[...], g_ref[...]).astype(BF16)
    q_ref[...] = _dot(xn, w_ref[:, 0:dk]).astype(BF16)
    kt_ref[...] = _dot(xn, w_ref[:, dk:2 * dk]).T.astype(BF16)
    v_ref[...] = _dot(xn, w_ref[:, 2 * dk:2 * dk + dv]).astype(BF16)
    og_ref[...] = _dot(xn, w_ref[:, 2 * dk + dv:2 * dk + 2 * dv]).astype(BF16)
    a_low = _dot(xn, wa_ref[...]).astype(BF16)
    pre = _dot(a_low, wgu_ref[...]) + bg_ref[...]
    lg = _log_sigmoid(pre) * (1.0 / GLA_GATE_NORMALIZER)
    lg_ref[...] = lg
    lgt_ref[...] = lg.T


def _gla_proj(h, gain, w_main, w_a, w_gu, b_gate):
    b, s, d = h.shape
    tm = TOKEN_TILE
    dk, dv = GLA_DK, GLA_DV
    row = lambda bi, i: (bi, i, 0)
    colmajor = lambda bi, i: (bi, 0, i)
    return pl.pallas_call(
        _gla_proj_kernel,
        grid=(b, s // tm),
        in_specs=[
            pl.BlockSpec((None, tm, d), row),
            _resident((1, d)),
            _resident((d, 2 * dk + 2 * dv)),
            _resident((d, LANES)),
            _resident((LANES, dk)),
            _resident((1, dk)),
        ],
        out_specs=[
            pl.BlockSpec((None, tm, dk), row),
            pl.BlockSpec((None, dk, tm), colmajor),
            pl.BlockSpec((None, tm, dv), row),
            pl.BlockSpec((None, tm, dv), row),
            pl.BlockSpec((None, tm, dk), row),
            pl.BlockSpec((None, dk, tm), colmajor),
        ],
        out_shape=[
            jax.ShapeDtypeStruct((b, s, dk), BF16),
            jax.ShapeDtypeStruct((b, dk, s), BF16),
            jax.ShapeDtypeStruct((b, s, dv), BF16),
            jax.ShapeDtypeStruct((b, s, dv), BF16),
            jax.ShapeDtypeStruct((b, s, dk), F32),
            jax.ShapeDtypeStruct((b, dk, s), F32),
        ],
        compiler_params=_params(("parallel", "parallel")),
        name="gla_proj",
    )(h, gain, w_main, w_a, w_gu, b_gate)


def _gla_constants():
    c = GLA_TILE
    i = np.arange(c)[:, None]
    m = np.arange(c)[None, :]
    fwd, rev, masks = [], [], []
    for level in range(1, GLA_LEVELS + 1):
        n = 1 << level
        same = (i // n) == (m // n)
        fwd.append((same & (m <= i)).astype(np.float32))
        rev.append((same & (m > i)).astype(np.float32).T)
    rev.append(np.ones((c, c), np.float32))
    for level in range(GLA_LEVELS):
        n = 1 << level
        masks.append(((i // (2 * n) == m // (2 * n)) & ((i // n) % 2 == 1)
                      & ((m // n) % 2 == 0)).astype(np.float32))
    masks.append((i == m).astype(np.float32))
    fwd = np.concatenate(fwd, axis=0)
    rev = np.concatenate(rev, axis=1)
    fwd2 = np.concatenate([fwd, fwd], axis=1)
    rev2 = np.concatenate([rev, rev], axis=0)
    return (jnp.asarray(fwd2, dtype=BF16), jnp.asarray(rev2, dtype=BF16),
            jnp.asarray(np.stack(masks), dtype=F32))


def _gla_chunk_kernel(q_ref, kt_ref, v_ref, og_ref, lg_ref, lgt_ref, fwd_ref, rev_ref,
                      mask_ref, hn_ref, a_ref, state_ref):
    c = GLA_TILE
    n_ch = GLA_STEP_CHUNKS
    dk = GLA_DK
    dkh, dvh = GLA_DK_HEAD, GLA_DV_HEAD
    n_lv = GLA_LEVELS

    @pl.when(pl.program_id(1) == 0)
    def _():
        state_ref[...] = jnp.zeros_like(state_ref)

    lg = lg_ref[...]
    hi, lo = _split_bf16(lg)
    g_cat = jnp.concatenate(
        [jnp.concatenate([hi[ch * c:(ch + 1) * c], lo[ch * c:(ch + 1) * c]], axis=0)
         for ch in range(n_ch)], axis=1)
    fwd_all = _dot(fwd_ref[...], g_cat)
    hi_t, lo_t = _split_bf16(lgt_ref[...])
    gt_cat = jnp.concatenate(
        [jnp.concatenate([hi_t[:, ch * c:(ch + 1) * c], lo_t[:, ch * c:(ch + 1) * c]], axis=1)
         for ch in range(n_ch)], axis=0)
    rev_all = _dot(gt_cat, rev_ref[...])

    for hd in range(GLA_HEADS):
        ks = slice(hd * dkh, (hd + 1) * dkh)
        vs = slice(hd * dvh, (hd + 1) * dvh)
        state = state_ref[hd]
        for ch in range(n_ch):
            rows = slice(ch * c, (ch + 1) * c)
            kcols = slice(ch * dk + hd * dkh, ch * dk + (hd + 1) * dkh)
            q = q_ref[rows, ks].astype(F32) * GLA_SCALE
            kt = kt_ref[ks, rows].astype(F32)
            v = v_ref[rows, vs]
            kt_bf = kt.astype(BF16)
            scores = _dot(q.astype(BF16), kt_bf) * mask_ref[n_lv]
            qd = q * jnp.exp(lg[rows, ks])
            scores = scores + _dot(qd.astype(BF16), kt_bf) * mask_ref[0]
            for lv in range(1, n_lv):
                qd = q * jnp.exp(fwd_all[(lv - 1) * c:lv * c, kcols])
                kd = kt * jnp.exp(rev_all[kcols, (lv - 1) * c:lv * c])
                scores = scores + _dot(qd.astype(BF16), kd.astype(BF16)) * mask_ref[lv]
            q_cum = q * jnp.exp(fwd_all[(n_lv - 1) * c:n_lv * c, kcols])
            out = _dot(q_cum.astype(BF16), state.astype(BF16)) + _dot(scores.astype(BF16), v)
            k_rest = kt * jnp.exp(rev_all[kcols, (n_lv - 1) * c:n_lv * c])
            keep = jnp.exp(rev_all[kcols, n_lv * c:(n_lv + 1) * c])
            keep = jnp.concatenate([keep] * (dvh // c), axis=1)
            state = keep * state + _dot(k_rest.astype(BF16), v)
            o = _rms(out, hn_ref[:, vs])
            og = og_ref[rows, vs].astype(F32)
            a_ref[rows, vs] = (o * (og * (1.0 / (1.0 + jnp.exp(-og))))).astype(BF16)
        state_ref[hd] = state


def _gla_chunk(q, kt, v, og, lg, lgt, head_norm):
    b, s, dk = q.shape
    dv = v.shape[-1]
    c = GLA_TILE * GLA_STEP_CHUNKS
    fwd, rev, masks = _gla_constants()
    row = lambda bi, i: (bi, i, 0)
    colmajor = lambda bi, i: (bi, 0, i)
    return pl.pallas_call(
        _gla_chunk_kernel,
        grid=(b, s // c),
        in_specs=[
            pl.BlockSpec((None, c, dk), row),
            pl.BlockSpec((None, dk, c), colmajor),
            pl.BlockSpec((None, c, dv), row),
            pl.BlockSpec((None, c, dv), row),
            pl.BlockSpec((None, c, dk), row),
            pl.BlockSpec((None, dk, c), colmajor),
            _resident(fwd.shape),
            _resident(rev.shape),
            _resident(masks.shape),
            _resident((1, dv)),
        ],
        out_specs=pl.BlockSpec((None, c, dv), row),
        out_shape=jax.ShapeDtypeStruct((b, s, dv), BF16),
        scratch_shapes=[pltpu.VMEM((GLA_HEADS, GLA_DK_HEAD, GLA_DV_HEAD), F32)],
        compiler_params=_params(("parallel", "arbitrary")),
        name="gla_chunk",
    )(q, kt, v, og, lg, lgt, fwd, rev, masks, head_norm)


def _post_kernel(a_ref, h_ref, wo_ref, g_ref, wup_ref, wdn_ref, o_ref, acc_ref):
    half = TOKEN_TILE // 2
    halves = (slice(0, half), slice(half, TOKEN_TILE))
    h1 = []
    xn = []
    for rows in halves:
        mixed = _dot(a_ref[rows, :], wo_ref[...])
        h1.append(h_ref[rows, :] + _rms(mixed, g_ref[1:2, :]))
        xn.append(_rms(h1[-1], g_ref[2:3, :]).astype(BF16))
    xn_full = jnp.concatenate(xn, axis=0)
    n_chunks = D_FF // FF_CHUNK
    for ci in range(n_chunks):
        cols = slice(ci * FF_CHUNK, (ci + 1) * FF_CHUNK)
        if ci == 0:
            up = jnp.concatenate([_dot(part, wup_ref[:, cols]) for part in xn], axis=0)
        else:
            up = _dot(xn_full, wup_ref[:, cols])
        up = jnp.maximum(up, 0.0)
        act = (up * up).astype(BF16)
        if ci == 0:
            acc_ref[...] = _dot(act, wdn_ref[cols, :])
        elif ci < n_chunks - 1:
            acc_ref[...] += _dot(act, wdn_ref[cols, :])
        else:
            for hi, rows in enumerate(halves):
                ffn = acc_ref[rows, :] + _dot(act[rows, :], wdn_ref[cols, :])
                o_ref[rows, :] = h1[hi] + _rms(ffn, g_ref[3:4, :])


def _post(a, h, w_o, gains, w_up, w_down):
    b, s, d = h.shape
    tm = TOKEN_TILE
    row = lambda bi, i: (bi, i, 0)
    return pl.pallas_call(
        _post_kernel,
        grid=(b, s // tm),
        in_specs=[
            pl.BlockSpec((None, tm, d), row),
            pl.BlockSpec((None, tm, d), row),
            _resident((d, d)),
            _resident((4, d)),
            _resident((d, D_FF)),
            _resident((D_FF, d)),
        ],
        out_specs=pl.BlockSpec((None, tm, d), row),
        out_shape=jax.ShapeDtypeStruct((b, s, d), F32),
        scratch_shapes=[pltpu.VMEM((tm, d), F32)],
        compiler_params=_params(("parallel", "parallel")),
        name="post_ffn",
    )(a, h, w_o, gains, w_up, w_down)


def kernel(x, norm_gains, sb_w_qkv, sb_w_o, conv_w_in, conv_w, conv_w_out, gla_w_in,
           gla_w_gate_up, gla_b_gate, gla_head_norm, gla_w_o, ffn_w_up, ffn_w_down):
    depth = norm_gains.shape[0]
    h = x
    for i in range(depth):
        kind, j = i % 3, i // 3
        gains = norm_gains[i]
        pre_gain = gains[0:1]
        if kind == 0:
            q, kt, v = _sb_proj(h, pre_gain, sb_w_qkv[j].astype(BF16))
            a = _sb_attention(q, kt, v)
            w_o = sb_w_o[j]
        elif kind == 1:
            a = _conv_mixer(h, pre_gain, conv_w_in[j].astype(BF16), conv_w[j])
            w_o = conv_w_out[j]
        else:
            n_main = 2 * GLA_DK + 2 * GLA_DV
            w_in = gla_w_in[j]
            w_a = jnp.pad(w_in[:, n_main:], ((0, 0), (0, LANES - GLA_GATE_RANK))).astype(BF16)
            w_gu = jnp.pad(gla_w_gate_up[j], ((0, LANES - GLA_GATE_RANK), (0, 0))).astype(BF16)
            q, kt, v, og, lg, lgt = _gla_proj(h, pre_gain, w_in[:, :n_main].astype(BF16), w_a,
                                              w_gu, gla_b_gate[j][None, :])
            a = _gla_chunk(q, kt, v, og, lg, lgt, gla_head_norm[j].reshape(1, GLA_DV))
            w_o = gla_w_o[j]
        h = _post(a, h, w_o.astype(BF16), gains, ffn_w_up[i].astype(BF16),
                  ffn_w_down[i].astype(BF16))
    return h
```

```python
import functools

import numpy as np
import jax
import jax.numpy as jnp
from jax import lax
from jax.experimental import pallas as pl
from jax.experimental.pallas import tpu as pltpu

F32 = jnp.float32
BF16 = jnp.bfloat16

D_MODEL = 1024
D_FF = 4 * D_MODEL
RMS_EPS = 1e-6

SB_HEADS = 16
SB_HEAD_DIM = D_MODEL // SB_HEADS
SB_SCALE = SB_HEAD_DIM ** -0.5
CONV_WIDTH = 3
GLA_HEADS = 4
GLA_DK = D_MODEL // 2
GLA_DV = D_MODEL
GLA_DK_HEAD = GLA_DK // GLA_HEADS
GLA_DV_HEAD = GLA_DV // GLA_HEADS
GLA_GATE_RANK = 16
GLA_GATE_NORMALIZER = 16.0
GLA_SCALE = GLA_DK_HEAD ** -0.5

LANES = 128
SUBLANES = 8
VMEM_LIMIT_BYTES = 56 * 1024 * 1024

TOKEN_TILE = 512
FF_CHUNK = 1024
SB_TILE = 512
SB_SUB = 128
SB_BAND_SUBS = 3
SB_GROUP = 2
SB_PAD_SUBS = SB_BAND_SUBS - 1
SB_HP_PER_STEP = 2
SB_F32_EXP_ZERO = -105.0
GLA_TILE = 128
GLA_STEP_CHUNKS = 2
GLA_LEVELS = 7


def _rms(x, gain):
    ms = jnp.mean(x * x, axis=-1, keepdims=True)
    return x * lax.rsqrt(ms + RMS_EPS) * gain


def _neg_abs(z):
    bits = lax.bitcast_convert_type(z, jnp.uint32) | jnp.uint32(0x80000000)
    return lax.bitcast_convert_type(bits, F32)


def _log_sigmoid(z):
    return jnp.minimum(z, 0.0) - jnp.log(1.0 + jnp.exp(_neg_abs(z)))


def _split_bf16(x):
    hi = x.astype(BF16)
    lo = (x - hi.astype(F32)).astype(BF16)
    return hi, lo


def _dot(a, b):
    return jnp.dot(a, b, preferred_element_type=F32)


def _resident(shape):
    zeros = (0,) * len(shape)
    return pl.BlockSpec(shape, lambda *_: zeros, pipeline_mode=pl.Buffered(1))


def _params(semantics):
    return pltpu.CompilerParams(dimension_semantics=semantics,
                                vmem_limit_bytes=VMEM_LIMIT_BYTES)


def _sb_suffix_matrix():
    t = SB_SUB
    m = np.arange(t)[:, None]
    j = np.arange(t)[None, :]
    half = np.concatenate([(m > j).astype(np.float32), np.ones((t, t), np.float32)], axis=1)
    return jnp.asarray(np.concatenate([half, half], axis=0), dtype=BF16)


def _sb_layer_kernel(xcur_ref, xnext_ref, g_ref, w_ref, u_ref, o_ref,
                     q_scr, kt_scr, v_scr, kt_new, v_new, xn_scr, o_scr, acc_ref, c_ref):
    t = SB_TILE
    sub = SB_SUB
    n_rb = t // sub
    band = SB_BAND_SUBS
    pad = SB_PAD_SUBS
    grp = SB_GROUP
    win = band + grp - 1
    pairs = n_rb // grp
    grp_rows = grp * sub
    n_hp = D_MODEL // LANES
    duo = SB_HP_PER_STEP
    i = pl.program_id(1)
    gain = g_ref[...]
    u = u_ref[...]

    def project_q(xn, j, slot):
        y = _dot(xn, w_ref[j, :, 0:duo * LANES]) * SB_SCALE
        for e in range(duo):
            q_scr[slot, j * duo + e] = y[:, e * LANES:(e + 1) * LANES].astype(BF16)

    def project_k(xn, j):
        y = _dot(xn, w_ref[j, :, duo * LANES:2 * duo * LANES])
        for e in range(duo):
            for kb in range(n_rb):
                kt_new[j * duo + e, kb] = (
                    y[kb * sub:(kb + 1) * sub, e * LANES:(e + 1) * LANES].T.astype(BF16))

    def project_v(xn, j):
        y = _dot(xn, w_ref[j, :, 2 * duo * LANES:3 * duo * LANES]).astype(BF16)
        for e in range(duo):
            v_new[j * duo + e] = y[:, e * LANES:(e + 1) * LANES]

    @pl.when(i == 0)
    def _():
        kt_scr[:, 0:pad] = jnp.zeros((n_hp, pad, LANES, sub), BF16)
        v_scr[:, 0:pad * sub, :] = jnp.zeros((n_hp, pad * sub, LANES), BF16)
        xn = _rms(xcur_ref[...], gain).astype(BF16)

        def first_tile(j, carry):
            project_q(xn, j, 0)
            project_k(xn, j)
            project_v(xn, j)
            return carry

        lax.fori_loop(0, n_hp // duo, first_tile, 0)

    xn_scr[...] = _rms(xnext_ref[...], gain).astype(BF16)
    slot = i % 2
    next_slot = (i + 1) % 2

    lane = lax.broadcasted_iota(jnp.int32, (t, LANES), 1)
    first = lane < SB_HEAD_DIM
    row = lax.broadcasted_iota(jnp.int32, (sub, sub), 0)
    col = lax.broadcasted_iota(jnp.int32, (sub, sub), 1)
    diag = col < row
    no_weight = jnp.zeros((sub, sub), BF16)
    row_blk = lax.broadcasted_iota(jnp.int32, (t, sub), 0) // sub
    not_first_tile = i > 0

    def chunk(qh, kblk, vblk, c, mask):
        z = _dot(qh, kblk)
        log_beta = _log_sigmoid(z)
        part = jnp.where(mask, log_beta - z, 0.0)
        hi, lo = _split_bf16(part)
        s2 = _dot(jnp.concatenate([hi, lo], axis=1), u)
        w = jnp.where(mask, jnp.exp(log_beta + s2[:, :sub] + c), 0.0)
        return _dot(w.astype(BF16), vblk), c + s2[:, sub:]

    def append_projected(hp):
        for kb in range(n_rb):
            kt_scr[hp, pad + i * n_rb + kb] = kt_new[hp, kb]
        v_scr[hp, pl.ds(pl.multiple_of((pad + i * n_rb) * sub, sub), t), :] = v_new[hp]

    def split_heads(hp):
        q = q_scr[slot, hp]
        zero = jnp.zeros_like(q)
        return jnp.where(first, q, zero), jnp.where(first, zero, q)

    def band_logits(hp, q_heads):
        parts = []
        tiles = {}
        for p in range(pairs):
            first_sub = pad + i * n_rb + p * grp - band + 1
            kwin = jnp.concatenate([kt_scr[hp, first_sub + s] for s in range(win)], axis=1)
            q_both = jnp.concatenate(
                [q_heads[hd][p * grp_rows:(p + 1) * grp_rows] for hd in range(2)], axis=0)
            z = _dot(q_both, kwin)
            for hd in range(2):
                for rr in range(grp):
                    top = hd * grp_rows + rr * sub
                    for s in range(band):
                        zt = z[top:top + sub, (rr + s) * sub:(rr + s + 1) * sub]
                        log_beta = _log_sigmoid(zt)
                        part = log_beta - zt
                        mask = diag if s == band - 1 else None
                        if p * grp + rr + s < band - 1:
                            mask = not_first_tile if mask is None else mask & not_first_tile
                        if mask is not None:
                            part = jnp.where(mask, part, 0.0)
                        hi, lo = _split_bf16(part)
                        tiles[p, hd, rr, s] = (len(parts), log_beta, mask)
                        parts.append(jnp.concatenate([hi, lo], axis=1))
        return jnp.concatenate(parts, axis=0), tiles

    def band_weights(e, hp, s2, tiles, p):
        first_sub = pad + i * n_rb + p * grp - band + 1
        w_rows = []
        for hd in range(2):
            for rr in range(grp):
                r = p * grp + rr
                c = jnp.zeros((sub, sub), F32)
                ws = [None] * band
                for s in reversed(range(band)):
                    index, log_beta, mask = tiles[p, hd, rr, s]
                    blk = s2[index * sub:(index + 1) * sub]
                    w = jnp.exp(log_beta + blk[:, :sub] + c)
                    if mask is not None:
                        w = jnp.where(mask, w, 0.0)
                    ws[s] = w.astype(BF16)
                    c = c + blk[:, sub:]
                c_ref[e, hd, r * sub:(r + 1) * sub, :] = c
                w_rows.append(jnp.concatenate(
                    [no_weight] * rr + ws + [no_weight] * (grp - 1 - rr), axis=1))
        vwin = v_scr[hp, pl.ds(pl.multiple_of(first_sub * sub, sub), win * sub), :]
        pv = _dot(jnp.concatenate(w_rows, axis=0), vwin)
        for hd in range(2):
            acc_ref[e, hd, p * grp_rows:(p + 1) * grp_rows, :] = pv[hd * grp_rows:(hd + 1) * grp_rows]

    def further_left(e, hp, q_heads):
        def more_keys_matter(state):
            k, c_max = state
            return jnp.logical_and(k >= 0, c_max > SB_F32_EXP_ZERO)

        def one_more_sub_block(state):
            k, _ = state
            not_yet_seen = row_blk >= k - i * n_rb + band
            kblk = kt_scr[hp, pad + k]
            vblk = v_scr[hp, pl.ds(pl.multiple_of((pad + k) * sub, sub), sub), :]
            for hd in range(2):
                pv, c = chunk(q_heads[hd], kblk, vblk, c_ref[e, hd], not_yet_seen)
                acc_ref[e, hd] += pv
                c_ref[e, hd] = c
            return k - 1, jnp.max(c_ref[e])

        k_start = i * n_rb + n_rb - 1 - band
        lax.while_loop(more_keys_matter, one_more_sub_block, (k_start, jnp.max(c_ref[e])))
        o_scr[hp] = jnp.where(first, acc_ref[e, 0], acc_ref[e, 1]).astype(BF16)

    def per_head_pair_duo(j, carry):
        hps = [j * duo + e for e in range(duo)]
        xn_next = xn_scr[...]
        for hp in hps:
            append_projected(hp)
        q_heads = [split_heads(hp) for hp in hps]
        stacked = [band_logits(hp, q_heads[e]) for e, hp in enumerate(hps)]
        project_q(xn_next, j, next_slot)
        s2 = []
        for e in range(duo):
            s2.append(_dot(stacked[e][0], u))
            if e == 0:
                project_k(xn_next, j)
        for e, hp in enumerate(hps):
            for p in range(pairs):
                band_weights(e, hp, s2[e], stacked[e][1], p)
            if e == 0:
                project_v(xn_next, j)
        for e, hp in enumerate(hps):
            further_left(e, hp, q_heads[e])
        return carry

    lax.fori_loop(0, n_hp // duo, per_head_pair_duo, 0)
    o_ref[...] = jnp.concatenate([o_scr[hp] for hp in range(n_hp)], axis=1)


def _sb_mixer(h, gain, w_qkv):
    b, s, d = h.shape
    t = SB_TILE
    n_hp = d // LANES
    duo = SB_HP_PER_STEP
    n_tiles = s // t
    n_subs = SB_PAD_SUBS + s // SB_SUB
    w_duo = (w_qkv.reshape(d, 3, n_hp // duo, duo * LANES).transpose(2, 0, 1, 3)
             .reshape(n_hp // duo, d, 3 * duo * LANES))
    return pl.pallas_call(
        _sb_layer_kernel,
        grid=(b, n_tiles),
        in_specs=[
            pl.BlockSpec((None, t, d), lambda bi, i: (bi, 0, 0)),
            pl.BlockSpec((None, t, d), lambda bi, i: (bi, jnp.minimum(i + 1, n_tiles - 1), 0)),
            _resident((1, d)),
            _resident((n_hp // duo, d, 3 * duo * LANES)),
            _resident((2 * SB_SUB, 2 * SB_SUB)),
        ],
        out_specs=pl.BlockSpec((None, t, d), lambda bi, i: (bi, i, 0)),
        out_shape=jax.ShapeDtypeStruct((b, s, d), BF16),
        scratch_shapes=[
            pltpu.VMEM((2, n_hp, t, LANES), BF16),
            pltpu.VMEM((n_hp, n_subs, LANES, SB_SUB), BF16),
            pltpu.VMEM((n_hp, n_subs * SB_SUB, LANES), BF16),
            pltpu.VMEM((n_hp, t // SB_SUB, LANES, SB_SUB), BF16),
            pltpu.VMEM((n_hp, t, LANES), BF16),
            pltpu.VMEM((t, d), BF16),
            pltpu.VMEM((n_hp, t, LANES), BF16),
            pltpu.VMEM((duo, 2, t, LANES), F32),
            pltpu.VMEM((duo, 2, t, SB_SUB), F32),
        ],
        compiler_params=_params(("arbitrary", "arbitrary")),
        name="sb_layer",
    )(h, h, gain, w_duo, _sb_suffix_matrix())


def _conv_kernel(x_ref, xp_ref, g_ref, w_ref, cw_ref, a_ref, hbuf_ref):
    d = D_MODEL
    tm = TOKEN_TILE
    i = pl.program_id(1)
    gain = g_ref[...]
    xn = _rms(x_ref[...], gain).astype(BF16)
    xpn = _rms(xp_ref[...], gain).astype(BF16)
    hbuf_ref[SUBLANES:, :] = _dot(xn, w_ref[:, d:2 * d]) * _dot(xn, w_ref[:, 2 * d:3 * d])
    h_prev = _dot(xpn, w_ref[:, d:2 * d]) * _dot(xpn, w_ref[:, 2 * d:3 * d])
    hbuf_ref[0:SUBLANES, :] = jnp.where(i > 0, h_prev, 0.0)
    conv = cw_ref[0:1, :] * hbuf_ref[pl.ds(SUBLANES - 2, tm), :]
    conv = conv + cw_ref[1:2, :] * hbuf_ref[pl.ds(SUBLANES - 1, tm), :]
    conv = conv + cw_ref[2:3, :] * hbuf_ref[pl.ds(SUBLANES, tm), :]
    a_ref[...] = (_dot(xn, w_ref[:, 0:d]) * conv).astype(BF16)


def _conv_mixer(h, gain, w_in, conv_w):
    b, s, d = h.shape
    tm = TOKEN_TILE
    rows_per_tile = tm // SUBLANES
    return pl.pallas_call(
        _conv_kernel,
        grid=(b, s // tm),
        in_specs=[
            pl.BlockSpec((None, tm, d), lambda bi, i: (bi, i, 0)),
            pl.BlockSpec((None, SUBLANES, d),
                         lambda bi, i: (bi, jnp.maximum(i * rows_per_tile - 1, 0), 0)),
            _resident((1, d)),
            _resident((d, 3 * d)),
            _resident((CONV_WIDTH, d)),
        ],
        out_specs=pl.BlockSpec((None, tm, d), lambda bi, i: (bi, i, 0)),
        out_shape=jax.ShapeDtypeStruct((b, s, d), BF16),
        scratch_shapes=[pltpu.VMEM((tm + SUBLANES, d), F32)],
        compiler_params=_params(("parallel", "parallel")),
        name="conv_mixer",
    )(h, h, gain, w_in, conv_w)


def _gla_proj_kernel(x_ref, g_ref, w_ref, wa_ref, wgu_ref, bg_ref,
                     q_ref, kt_ref, v_ref, og_ref, lg_ref, lgt_ref):
    dk, dv = GLA_DK, GLA_DV
    xn = _rms(x_ref[...], g_ref[...]).astype(BF16)
    q_ref[...] = _dot(xn, w_ref[:, 0:dk]).astype(BF16)
    kt_ref[...] = _dot(xn, w_ref[:, dk:2 * dk]).T.astype(BF16)
    v_ref[...] = _dot(xn, w_ref[:, 2 * dk:2 * dk + dv]).astype(BF16)
    og_ref[...] = _dot(xn, w_ref[:, 2 * dk + dv:2 * dk + 2 * dv]).astype(BF16)
    a_low = _dot(xn, wa_ref[...]).astype(BF16)
    pre = _dot(a_low, wgu_ref[...]) + bg_ref[...]
    lg = _log_sigmoid(pre) * (1.0 / GLA_GATE_NORMALIZER)
    lg_ref[...] = lg
    lgt_ref[...] = lg.T


def _gla_proj(h, gain, w_main, w_a, w_gu, b_gate):
    b, s, d = h.shape
    tm = TOKEN_TILE
    dk, dv = GLA_DK, GLA_DV
    row = lambda bi, i: (bi, i, 0)
    colmajor = lambda bi, i: (bi, 0, i)
    return pl.pallas_call(
        _gla_proj_kernel,
        grid=(b, s // tm),
        in_specs=[
            pl.BlockSpec((None, tm, d), row),
            _resident((1, d)),
            _resident((d, 2 * dk + 2 * dv)),
            _resident((d, LANES)),
            _resident((LANES, dk)),
            _resident((1, dk)),
        ],
        out_specs=[
            pl.BlockSpec((None, tm, dk), row),
            pl.BlockSpec((None, dk, tm), colmajor),
            pl.BlockSpec((None, tm, dv), row),
            pl.BlockSpec((None, tm, dv), row),
            pl.BlockSpec((None, tm, dk), row),
            pl.BlockSpec((None, dk, tm), colmajor),
        ],
        out_shape=[
            jax.ShapeDtypeStruct((b, s, dk), BF16),
            jax.ShapeDtypeStruct((b, dk, s), BF16),
            jax.ShapeDtypeStruct((b, s, dv), BF16),
            jax.ShapeDtypeStruct((b, s, dv), BF16),
            jax.ShapeDtypeStruct((b, s, dk), F32),
            jax.ShapeDtypeStruct((b, dk, s), F32),
        ],
        compiler_params=_params(("parallel", "parallel")),
        name="gla_proj",
    )(h, gain, w_main, w_a, w_gu, b_gate)


def _gla_constants():
    c = GLA_TILE
    i = np.arange(c)[:, None]
    m = np.arange(c)[None, :]
    fwd, rev, masks = [], [], []
    for level in range(1, GLA_LEVELS + 1):
        n = 1 << level
        same = (i // n) == (m // n)
        fwd.append((same & (m <= i)).astype(np.float32))
        rev.append((same & (m > i)).astype(np.float32).T)
    rev.append(np.ones((c, c), np.float32))
    for level in range(GLA_LEVELS):
        n = 1 << level
        masks.append(((i // (2 * n) == m // (2 * n)) & ((i // n) % 2 == 1)
                      & ((m // n) % 2 == 0)).astype(np.float32))
    masks.append((i == m).astype(np.float32))
    fwd = np.concatenate(fwd, axis=0)
    rev = np.concatenate(rev, axis=1)
    fwd2 = np.concatenate([fwd, fwd], axis=1)
    rev2 = np.concatenate([rev, rev], axis=0)
    return (jnp.asarray(fwd2, dtype=BF16), jnp.asarray(rev2, dtype=BF16),
            jnp.asarray(np.stack(masks), dtype=F32))


def _gla_chunk_kernel(q_ref, kt_ref, v_ref, og_ref, lg_ref, lgt_ref, fwd_ref, rev_ref,
                      mask_ref, hn_ref, a_ref, state_ref):
    c = GLA_TILE
    n_ch = GLA_STEP_CHUNKS
    dk = GLA_DK
    dkh, dvh = GLA_DK_HEAD, GLA_DV_HEAD
    n_lv = GLA_LEVELS

    @pl.when(pl.program_id(1) == 0)
    def _():
        state_ref[...] = jnp.zeros_like(state_ref)

    lg = lg_ref[...]
    hi, lo = _split_bf16(lg)
    g_cat = jnp.concatenate(
        [jnp.concatenate([hi[ch * c:(ch + 1) * c], lo[ch * c:(ch + 1) * c]], axis=0)
         for ch in range(n_ch)], axis=1)
    fwd_all = _dot(fwd_ref[...], g_cat)
    hi_t, lo_t = _split_bf16(lgt_ref[...])
    gt_cat = jnp.concatenate(
        [jnp.concatenate([hi_t[:, ch * c:(ch + 1) * c], lo_t[:, ch * c:(ch + 1) * c]], axis=1)
         for ch in range(n_ch)], axis=0)
    rev_all = _dot(gt_cat, rev_ref[...])

    for hd in range(GLA_HEADS):
        ks = slice(hd * dkh, (hd + 1) * dkh)
        vs = slice(hd * dvh, (hd + 1) * dvh)
        state = state_ref[hd]
        for ch in range(n_ch):
            rows = slice(ch * c, (ch + 1) * c)
            kcols = slice(ch * dk + hd * dkh, ch * dk + (hd + 1) * dkh)
            q = q_ref[rows, ks].astype(F32) * GLA_SCALE
            kt = kt_ref[ks, rows].astype(F32)
            v = v_ref[rows, vs]
            kt_bf = kt.astype(BF16)
            scores = _dot(q.astype(BF16), kt_bf) * mask_ref[n_lv]
            qd = q * jnp.exp(lg[rows, ks])
            scores = scores + _dot(qd.astype(BF16), kt_bf) * mask_ref[0]
            for lv in range(1, n_lv):
                qd = q * jnp.exp(fwd_all[(lv - 1) * c:lv * c, kcols])
                kd = kt * jnp.exp(rev_all[kcols, (lv - 1) * c:lv * c])
                scores = scores + _dot(qd.astype(BF16), kd.astype(BF16)) * mask_ref[lv]
            q_cum = q * jnp.exp(fwd_all[(n_lv - 1) * c:n_lv * c, kcols])
            out = _dot(q_cum.astype(BF16), state.astype(BF16)) + _dot(scores.astype(BF16), v)
            k_rest = kt * jnp.exp(rev_all[kcols, (n_lv - 1) * c:n_lv * c])
            keep = jnp.exp(rev_all[kcols, n_lv * c:(n_lv + 1) * c])
            keep = jnp.concatenate([keep] * (dvh // c), axis=1)
            state = keep * state + _dot(k_rest.astype(BF16), v)
            o = _rms(out, hn_ref[:, vs])
            og = og_ref[rows, vs].astype(F32)
            a_ref[rows, vs] = (o * (og * (1.0 / (1.0 + jnp.exp(-og))))).astype(BF16)
        state_ref[hd] = state


def _gla_chunk(q, kt, v, og, lg, lgt, head_norm):
    b, s, dk = q.shape
    dv = v.shape[-1]
    c = GLA_TILE * GLA_STEP_CHUNKS
    fwd, rev, masks = _gla_constants()
    row = lambda bi, i: (bi, i, 0)
    colmajor = lambda bi, i: (bi, 0, i)
    return pl.pallas_call(
        _gla_chunk_kernel,
        grid=(b, s // c),
        in_specs=[
            pl.BlockSpec((None, c, dk), row),
            pl.BlockSpec((None, dk, c), colmajor),
            pl.BlockSpec((None, c, dv), row),
            pl.BlockSpec((None, c, dv), row),
            pl.BlockSpec((None, c, dk), row),
            pl.BlockSpec((None, dk, c), colmajor),
            _resident(fwd.shape),
            _resident(rev.shape),
            _resident(masks.shape),
            _resident((1, dv)),
        ],
        out_specs=pl.BlockSpec((None, c, dv), row),
        out_shape=jax.ShapeDtypeStruct((b, s, dv), BF16),
        scratch_shapes=[pltpu.VMEM((GLA_HEADS, GLA_DK_HEAD, GLA_DV_HEAD), F32)],
        compiler_params=_params(("parallel", "arbitrary")),
        name="gla_chunk",
    )(q, kt, v, og, lg, lgt, fwd, rev, masks, head_norm)


def _post_kernel(a_ref, h_ref, wo_ref, g_ref, wup_ref, wdn_ref, o_ref, acc_ref):
    half = TOKEN_TILE // 2
    halves = (slice(0, half), slice(half, TOKEN_TILE))
    h1 = []
    xn = []
    for rows in halves:
        mixed = _dot(a_ref[rows, :], wo_ref[...])
        h1.append(h_ref[rows, :] + _rms(mixed, g_ref[1:2, :]))
        xn.append(_rms(h1[-1], g_ref[2:3, :]).astype(BF16))
    xn_full = jnp.concatenate(xn, axis=0)
    n_chunks = D_FF // FF_CHUNK
    for ci in range(n_chunks):
        cols = slice(ci * FF_CHUNK, (ci + 1) * FF_CHUNK)
        if ci == 0:
            up = jnp.concatenate([_dot(part, wup_ref[:, cols]) for part in xn], axis=0)
        else:
            up = _dot(xn_full, wup_ref[:, cols])
        up = jnp.maximum(up, 0.0)
        act = (up * up).astype(BF16)
        if ci == 0:
            acc_ref[...] = _dot(act, wdn_ref[cols, :])
        elif ci < n_chunks - 1:
            acc_ref[...] += _dot(act, wdn_ref[cols, :])
        else:
            for hi, rows in enumerate(halves):
                ffn = acc_ref[rows, :] + _dot(act[rows, :], wdn_ref[cols, :])
                o_ref[rows, :] = h1[hi] + _rms(ffn, g_ref[3:4, :])


def _post(a, h, w_o, gains, w_up, w_down):
    b, s, d = h.shape
    tm = TOKEN_TILE
    row = lambda bi, i: (bi, i, 0)
    return pl.pallas_call(
        _post_kernel,
        grid=(b, s // tm),
        in_specs=[
            pl.BlockSpec((None, tm, d), row),
            pl.BlockSpec((None, tm, d), row),
            _resident((d, d)),
            _resident((4, d)),
            _resident((d, D_FF)),
            _resident((D_FF, d)),
        ],
        out_specs=pl.BlockSpec((None, tm, d), row),
        out_shape=jax.ShapeDtypeStruct((b, s, d), F32),
        scratch_shapes=[pltpu.VMEM((tm, d), F32)],
        compiler_params=_params(("parallel", "parallel")),
        name="post_ffn",
    )(a, h, w_o, gains, w_up, w_down)


def kernel(x, norm_gains, sb_w_qkv, sb_w_o, conv_w_in, conv_w, conv_w_out, gla_w_in,
           gla_w_gate_up, gla_b_gate, gla_head_norm, gla_w_o, ffn_w_up, ffn_w_down):
    depth = norm_gains.shape[0]
    h = x
    for i in range(depth):
        kind, j = i % 3, i // 3
        gains = norm_gains[i]
        pre_gain = gains[0:1]
        if kind == 0:
            a = _sb_mixer(h, pre_gain, sb_w_qkv[j].astype(BF16))
            w_o = sb_w_o[j]
        elif kind == 1:
            a = _conv_mixer(h, pre_gain, conv_w_in[j].astype(BF16), conv_w[j])
            w_o = conv_w_out[j]
        else:
            n_main = 2 * GLA_DK + 2 * GLA_DV
            w_in = gla_w_in[j]
            w_a = jnp.pad(w_in[:, n_main:], ((0, 0), (0, LANES - GLA_GATE_RANK))).astype(BF16)
            w_gu = jnp.pad(gla_w_gate_up[j], ((0, LANES - GLA_GATE_RANK), (0, 0))).astype(BF16)
            q, kt, v, og, lg, lgt = _gla_proj(h, pre_gain, w_in[:, :n_main].astype(BF16), w_a,
                                              w_gu, gla_b_gate[j][None, :])
            a = _gla_chunk(q, kt, v, og, lg, lgt, gla_head_norm[j].reshape(1, GLA_DV))
            w_o = gla_w_o[j]
        h = _post(a, h, w_o.astype(BF16), gains, ffn_w_up[i].astype(BF16),
                  ffn_w_down[i].astype(BF16))
    return h
```

```python
import functools

import numpy as np
import jax
import jax.numpy as jnp
from jax import lax
from jax.experimental import pallas as pl
from jax.experimental.pallas import tpu as pltpu

F32 = jnp.float32
BF16 = jnp.bfloat16

D_MODEL = 1024
D_FF = 4 * D_MODEL
RMS_EPS = 1e-6

SB_HEADS = 16
SB_HEAD_DIM = D_MODEL // SB_HEADS
SB_SCALE = SB_HEAD_DIM ** -0.5
CONV_WIDTH = 3
GLA_HEADS = 4
GLA_DK = D_MODEL // 2
GLA_DV = D_MODEL
GLA_DK_HEAD = GLA_DK // GLA_HEADS
GLA_DV_HEAD = GLA_DV // GLA_HEADS
GLA_GATE_RANK = 16
GLA_GATE_NORMALIZER = 16.0
GLA_SCALE = GLA_DK_HEAD ** -0.5

LANES = 128
SUBLANES = 8
VMEM_LIMIT_BYTES = 56 * 1024 * 1024

TOKEN_TILE = 512
FF_CHUNK = 1024
SB_TILE = 512
SB_SUB = 128
SB_BAND_SUBS = 3
SB_GROUP = 2
SB_PAD_SUBS = SB_BAND_SUBS - 1
SB_HP_PER_STEP = 2
SB_F32_EXP_ZERO = -105.0
GLA_TILE = 128
GLA_STEP_CHUNKS = 2
GLA_LEVELS = 7


def _rms(x, gain):
    ms = jnp.mean(x * x, axis=-1, keepdims=True)
    return x * lax.rsqrt(ms + RMS_EPS) * gain


def _neg_abs(z):
    bits = lax.bitcast_convert_type(z, jnp.uint32) | jnp.uint32(0x80000000)
    return lax.bitcast_convert_type(bits, F32)


def _log_sigmoid(z):
    return jnp.minimum(z, 0.0) - jnp.log(1.0 + jnp.exp(_neg_abs(z)))


def _split_bf16(x):
    hi = x.astype(BF16)
    lo = (x - hi.astype(F32)).astype(BF16)
    return hi, lo


def _dot(a, b):
    return jnp.dot(a, b, preferred_element_type=F32)


def _resident(shape):
    zeros = (0,) * len(shape)
    return pl.BlockSpec(shape, lambda *_: zeros, pipeline_mode=pl.Buffered(1))


def _params(semantics):
    return pltpu.CompilerParams(dimension_semantics=semantics,
                                vmem_limit_bytes=VMEM_LIMIT_BYTES)


def _sb_suffix_matrix():
    t = SB_SUB
    m = np.arange(t)[:, None]
    j = np.arange(t)[None, :]
    half = np.concatenate([(m > j).astype(np.float32), np.ones((t, t), np.float32)], axis=1)
    return jnp.asarray(np.concatenate([half, half], axis=0), dtype=BF16)


def _sb_layer_kernel(x0_ref, x1_ref, x2_ref, g_ref, w_ref, u_ref, o_ref,
                     q_scr, kt_scr, v_scr, kt_new, v_new, xn_buf, o_scr, acc_ref, c_ref):
    t = SB_TILE
    sub = SB_SUB
    n_rb = t // sub
    band = SB_BAND_SUBS
    pad = SB_PAD_SUBS
    grp = SB_GROUP
    win = band + grp - 1
    pairs = n_rb // grp
    grp_rows = grp * sub
    n_hp = D_MODEL // LANES
    duo = SB_HP_PER_STEP
    i = pl.program_id(1)
    step = pl.program_id(0) * pl.num_programs(1) + i
    gain = g_ref[...]
    u = u_ref[...]

    def project_q(xn, j, slot):
        y = _dot(xn, w_ref[j, :, 0:duo * LANES]) * SB_SCALE
        for e in range(duo):
            q_scr[slot, j * duo + e] = y[:, e * LANES:(e + 1) * LANES].astype(BF16)

    def project_k(xn, j):
        y = _dot(xn, w_ref[j, :, duo * LANES:2 * duo * LANES])
        for e in range(duo):
            for kb in range(n_rb):
                kt_new[j * duo + e, kb] = (
                    y[kb * sub:(kb + 1) * sub, e * LANES:(e + 1) * LANES].T.astype(BF16))

    def project_v(xn, j):
        y = _dot(xn, w_ref[j, :, 2 * duo * LANES:3 * duo * LANES]).astype(BF16)
        for e in range(duo):
            v_new[j * duo + e] = y[:, e * LANES:(e + 1) * LANES]

    @pl.when(i == 0)
    def _():
        kt_scr[:, 0:pad] = jnp.zeros((n_hp, pad, LANES, sub), BF16)
        v_scr[:, 0:pad * sub, :] = jnp.zeros((n_hp, pad * sub, LANES), BF16)

    @pl.when(step == 0)
    def _():
        xn = _rms(x0_ref[...], gain).astype(BF16)

        def first_tile(j, carry):
            project_q(xn, j, 0)
            project_k(xn, j)
            project_v(xn, j)
            return carry

        lax.fori_loop(0, n_hp // duo, first_tile, 0)
        xn_buf[0] = _rms(x1_ref[...], gain).astype(BF16)

    slot = step % 2
    next_slot = (step + 1) % 2
    piece = t // (n_hp // duo)

    lane = lax.broadcasted_iota(jnp.int32, (t, LANES), 1)
    first = lane < SB_HEAD_DIM
    row = lax.broadcasted_iota(jnp.int32, (sub, sub), 0)
    col = lax.broadcasted_iota(jnp.int32, (sub, sub), 1)
    diag = col < row
    no_weight = jnp.zeros((sub, sub), BF16)
    row_blk = lax.broadcasted_iota(jnp.int32, (t, sub), 0) // sub
    not_first_tile = i > 0
    has_keys_further_left = row_blk + i * n_rb >= band

    def chunk(qh, kblk, vblk, c, mask):
        z = _dot(qh, kblk)
        log_beta = _log_sigmoid(z)
        part = jnp.where(mask, log_beta - z, 0.0)
        hi, lo = _split_bf16(part)
        s2 = _dot(jnp.concatenate([hi, lo], axis=1), u)
        w = jnp.where(mask, jnp.exp(log_beta + s2[:, :sub] + c), 0.0)
        return _dot(w.astype(BF16), vblk), c + s2[:, sub:]

    def append_projected(hp):
        for kb in range(n_rb):
            kt_scr[hp, pad + i * n_rb + kb] = kt_new[hp, kb]
        v_scr[hp, pl.ds(pl.multiple_of((pad + i * n_rb) * sub, sub), t), :] = v_new[hp]

    def split_heads(hp):
        q = q_scr[slot, hp]
        zero = jnp.zeros_like(q)
        return jnp.where(first, q, zero), jnp.where(first, zero, q)

    def band_logits(hp, q_heads):
        parts = []
        tiles = {}
        for p in range(pairs):
            first_sub = pad + i * n_rb + p * grp - band + 1
            kwin = jnp.concatenate([kt_scr[hp, first_sub + s] for s in range(win)], axis=1)
            q_both = jnp.concatenate(
                [q_heads[hd][p * grp_rows:(p + 1) * grp_rows] for hd in range(2)], axis=0)
            z = _dot(q_both, kwin)
            for hd in range(2):
                for rr in range(grp):
                    top = hd * grp_rows + rr * sub
                    for s in range(band):
                        zt = z[top:top + sub, (rr + s) * sub:(rr + s + 1) * sub]
                        log_beta = _log_sigmoid(zt)
                        part = log_beta - zt
                        mask = diag if s == band - 1 else None
                        if p * grp + rr + s < band - 1:
                            mask = not_first_tile if mask is None else mask & not_first_tile
                        if mask is not None:
                            part = jnp.where(mask, part, 0.0)
                        hi, lo = _split_bf16(part)
                        tiles[p, hd, rr, s] = (len(parts), log_beta, mask)
                        parts.append(jnp.concatenate([hi, lo], axis=1))
        return jnp.concatenate(parts, axis=0), tiles

    def band_weights(e, hp, s2, tiles, p):
        first_sub = pad + i * n_rb + p * grp - band + 1
        w_rows = []
        for hd in range(2):
            for rr in range(grp):
                r = p * grp + rr
                c = jnp.zeros((sub, sub), F32)
                ws = [None] * band
                for s in reversed(range(band)):
                    index, log_beta, mask = tiles[p, hd, rr, s]
                    blk = s2[index * sub:(index + 1) * sub]
                    w = jnp.exp(log_beta + blk[:, :sub] + c)
                    if mask is not None:
                        w = jnp.where(mask, w, 0.0)
                    ws[s] = w.astype(BF16)
                    c = c + blk[:, sub:]
                c_ref[e, hd, r * sub:(r + 1) * sub, :] = c
                w_rows.append(jnp.concatenate(
                    [no_weight] * rr + ws + [no_weight] * (grp - 1 - rr), axis=1))
        vwin = v_scr[hp, pl.ds(pl.multiple_of(first_sub * sub, sub), win * sub), :]
        pv = _dot(jnp.concatenate(w_rows, axis=0), vwin)
        for hd in range(2):
            acc_ref[e, hd, p * grp_rows:(p + 1) * grp_rows, :] = pv[hd * grp_rows:(hd + 1) * grp_rows]

    def further_left(e, hp, q_heads):
        def more_keys_matter(state):
            k, c_max = state
            return jnp.logical_and(k >= 0, c_max > SB_F32_EXP_ZERO)

        def one_more_sub_block(state):
            k, _ = state
            not_yet_seen = row_blk >= k - i * n_rb + band
            kblk = kt_scr[hp, pad + k]
            vblk = v_scr[hp, pl.ds(pl.multiple_of((pad + k) * sub, sub), sub), :]
            for hd in range(2):
                pv, c = chunk(q_heads[hd], kblk, vblk, c_ref[e, hd], not_yet_seen)
                acc_ref[e, hd] += pv
                c_ref[e, hd] = c
            return k - 1, largest_open_carry()

        def largest_open_carry():
            return jnp.max(jnp.where(has_keys_further_left, c_ref[e], -jnp.inf))

        k_start = i * n_rb + n_rb - 1 - band
        lax.while_loop(more_keys_matter, one_more_sub_block, (k_start, largest_open_carry()))
        o_scr[hp] = jnp.where(first, acc_ref[e, 0], acc_ref[e, 1]).astype(BF16)

    def per_head_pair_duo(j, carry):
        hps = [j * duo + e for e in range(duo)]
        xn_next = xn_buf[slot]
        for hp in hps:
            append_projected(hp)
        q_heads = [split_heads(hp) for hp in hps]
        stacked = [band_logits(hp, q_heads[e]) for e, hp in enumerate(hps)]
        project_q(xn_next, j, next_slot)
        s2 = []
        for e in range(duo):
            s2.append(_dot(stacked[e][0], u))
            if e == 0:
                project_k(xn_next, j)
        for e, hp in enumerate(hps):
            for p in range(pairs):
                band_weights(e, hp, s2[e], stacked[e][1], p)
            if e == 0:
                project_v(xn_next, j)
        rows = pl.ds(pl.multiple_of(j * piece, piece), piece)
        xn_buf[next_slot, rows, :] = _rms(x2_ref[rows, :], gain).astype(BF16)
        for e, hp in enumerate(hps):
            further_left(e, hp, q_heads[e])
        return carry

    lax.fori_loop(0, n_hp // duo, per_head_pair_duo, 0)
    o_ref[...] = jnp.concatenate([o_scr[hp] for hp in range(n_hp)], axis=1)


def _sb_mixer(h, gain, w_qkv):
    b, s, d = h.shape
    t = SB_TILE
    n_hp = d // LANES
    duo = SB_HP_PER_STEP
    n_tiles = s // t
    n_subs = SB_PAD_SUBS + s // SB_SUB
    w_duo = (w_qkv.reshape(d, 3, n_hp // duo, duo * LANES).transpose(2, 0, 1, 3)
             .reshape(n_hp // duo, d, 3 * duo * LANES))

    def tile_after_next(bi, i):
        tile = jnp.minimum(bi * n_tiles + i + 2, b * n_tiles - 1)
        return tile // n_tiles, tile % n_tiles, 0

    return pl.pallas_call(
        _sb_layer_kernel,
        grid=(b, n_tiles),
        in_specs=[
            pl.BlockSpec((None, t, d), lambda bi, i: (0, 0, 0)),
            pl.BlockSpec((None, t, d), lambda bi, i: (0, 1, 0)),
            pl.BlockSpec((None, t, d), tile_after_next),
            _resident((1, d)),
            _resident((n_hp // duo, d, 3 * duo * LANES)),
            _resident((2 * SB_SUB, 2 * SB_SUB)),
        ],
        out_specs=pl.BlockSpec((None, t, d), lambda bi, i: (bi, i, 0)),
        out_shape=jax.ShapeDtypeStruct((b, s, d), BF16),
        scratch_shapes=[
            pltpu.VMEM((2, n_hp, t, LANES), BF16),
            pltpu.VMEM((n_hp, n_subs, LANES, SB_SUB), BF16),
            pltpu.VMEM((n_hp, n_subs * SB_SUB, LANES), BF16),
            pltpu.VMEM((n_hp, t // SB_SUB, LANES, SB_SUB), BF16),
            pltpu.VMEM((n_hp, t, LANES), BF16),
            pltpu.VMEM((2, t, d), BF16),
            pltpu.VMEM((n_hp, t, LANES), BF16),
            pltpu.VMEM((duo, 2, t, LANES), F32),
            pltpu.VMEM((duo, 2, t, SB_SUB), F32),
        ],
        compiler_params=_params(("arbitrary", "arbitrary")),
        name="sb_layer",
    )(h, h, h, gain, w_duo, _sb_suffix_matrix())


def _conv_kernel(x_ref, xp_ref, g_ref, w_ref, cw_ref, a_ref, hbuf_ref):
    d = D_MODEL
    tm = TOKEN_TILE
    i = pl.program_id(1)
    gain = g_ref[...]
    xn = _rms(x_ref[...], gain).astype(BF16)
    xpn = _rms(xp_ref[...], gain).astype(BF16)
    hbuf_ref[SUBLANES:, :] = _dot(xn, w_ref[:, d:2 * d]) * _dot(xn, w_ref[:, 2 * d:3 * d])
    h_prev = _dot(xpn, w_ref[:, d:2 * d]) * _dot(xpn, w_ref[:, 2 * d:3 * d])
    hbuf_ref[0:SUBLANES, :] = jnp.where(i > 0, h_prev, 0.0)
    conv = cw_ref[0:1, :] * hbuf_ref[pl.ds(SUBLANES - 2, tm), :]
    conv = conv + cw_ref[1:2, :] * hbuf_ref[pl.ds(SUBLANES - 1, tm), :]
    conv = conv + cw_ref[2:3, :] * hbuf_ref[pl.ds(SUBLANES, tm), :]
    a_ref[...] = (_dot(xn, w_ref[:, 0:d]) * conv).astype(BF16)


def _conv_mixer(h, gain, w_in, conv_w):
    b, s, d = h.shape
    tm = TOKEN_TILE
    rows_per_tile = tm // SUBLANES
    return pl.pallas_call(
        _conv_kernel,
        grid=(b, s // tm),
        in_specs=[
            pl.BlockSpec((None, tm, d), lambda bi, i: (bi, i, 0)),
            pl.BlockSpec((None, SUBLANES, d),
                         lambda bi, i: (bi, jnp.maximum(i * rows_per_tile - 1, 0), 0)),
            _resident((1, d)),
            _resident((d, 3 * d)),
            _resident((CONV_WIDTH, d)),
        ],
        out_specs=pl.BlockSpec((None, tm, d), lambda bi, i: (bi, i, 0)),
        out_shape=jax.ShapeDtypeStruct((b, s, d), BF16),
        scratch_shapes=[pltpu.VMEM((tm + SUBLANES, d), F32)],
        compiler_params=_params(("parallel", "parallel")),
        name="conv_mixer",
    )(h, h, gain, w_in, conv_w)


def _gla_proj_kernel(x_ref, g_ref, w_ref, wa_ref, wgu_ref, bg_ref,
                     q_ref, kt_ref, v_ref, og_ref, lg_ref, lgt_ref):
    dk, dv = GLA_DK, GLA_DV
    xn = _rms(x_ref[...], g_ref[...]).astype(BF16)
    q_ref[...] = _dot(xn, w_ref[:, 0:dk]).astype(BF16)
    kt_ref[...] = _dot(xn, w_ref[:, dk:2 * dk]).T.astype(BF16)
    v_ref[...] = _dot(xn, w_ref[:, 2 * dk:2 * dk + dv]).astype(BF16)
    og_ref[...] = _dot(xn, w_ref[:, 2 * dk + dv:2 * dk + 2 * dv]).astype(BF16)
    a_low = _dot(xn, wa_ref[...]).astype(BF16)
    pre = _dot(a_low, wgu_ref[...]) + bg_ref[...]
    lg = _log_sigmoid(pre) * (1.0 / GLA_GATE_NORMALIZER)
    lg_ref[...] = lg
    lgt_ref[...] = lg.T


def _gla_proj(h, gain, w_main, w_a, w_gu, b_gate):
    b, s, d = h.shape
    tm = TOKEN_TILE
    dk, dv = GLA_DK, GLA_DV
    row = lambda bi, i: (bi, i, 0)
    colmajor = lambda bi, i: (bi, 0, i)
    return pl.pallas_call(
        _gla_proj_kernel,
        grid=(b, s // tm),
        in_specs=[
            pl.BlockSpec((None, tm, d), row),
            _resident((1, d)),
            _resident((d, 2 * dk + 2 * dv)),
            _resident((d, LANES)),
            _resident((LANES, dk)),
            _resident((1, dk)),
        ],
        out_specs=[
            pl.BlockSpec((None, tm, dk), row),
            pl.BlockSpec((None, dk, tm), colmajor),
            pl.BlockSpec((None, tm, dv), row),
            pl.BlockSpec((None, tm, dv), row),
            pl.BlockSpec((None, tm, dk), row),
            pl.BlockSpec((None, dk, tm), colmajor),
        ],
        out_shape=[
            jax.ShapeDtypeStruct((b, s, dk), BF16),
            jax.ShapeDtypeStruct((b, dk, s), BF16),
            jax.ShapeDtypeStruct((b, s, dv), BF16),
            jax.ShapeDtypeStruct((b, s, dv), BF16),
            jax.ShapeDtypeStruct((b, s, dk), F32),
            jax.ShapeDtypeStruct((b, dk, s), F32),
        ],
        compiler_params=_params(("parallel", "parallel")),
        name="gla_proj",
    )(h, gain, w_main, w_a, w_gu, b_gate)


def _gla_constants():
    c = GLA_TILE
    i = np.arange(c)[:, None]
    m = np.arange(c)[None, :]
    fwd, rev, masks = [], [], []
    for level in range(1, GLA_LEVELS + 1):
        n = 1 << level
        same = (i // n) == (m // n)
        fwd.append((same & (m <= i)).astype(np.float32))
        rev.append((same & (m > i)).astype(np.float32).T)
    rev.append(np.ones((c, c), np.float32))
    for level in range(GLA_LEVELS):
        n = 1 << level
        masks.append(((i // (2 * n) == m // (2 * n)) & ((i // n) % 2 == 1)
                      & ((m // n) % 2 == 0)).astype(np.float32))
    masks.append((i == m).astype(np.float32))
    fwd = np.concatenate(fwd, axis=0)
    rev = np.concatenate(rev, axis=1)
    fwd2 = np.concatenate([fwd, fwd], axis=1)
    rev2 = np.concatenate([rev, rev], axis=0)
    return (jnp.asarray(fwd2, dtype=BF16), jnp.asarray(rev2, dtype=BF16),
            jnp.asarray(np.stack(masks), dtype=F32))


def _gla_chunk_kernel(q_ref, kt_ref, v_ref, og_ref, lg_ref, lgt_ref, fwd_ref, rev_ref,
                      mask_ref, hn_ref, a_ref, state_ref):
    c = GLA_TILE
    n_ch = GLA_STEP_CHUNKS
    dk = GLA_DK
    dkh, dvh = GLA_DK_HEAD, GLA_DV_HEAD
    n_lv = GLA_LEVELS

    @pl.when(pl.program_id(1) == 0)
    def _():
        state_ref[...] = jnp.zeros_like(state_ref)

    lg = lg_ref[...]
    hi, lo = _split_bf16(lg)
    g_cat = jnp.concatenate(
        [jnp.concatenate([hi[ch * c:(ch + 1) * c], lo[ch * c:(ch + 1) * c]], axis=0)
         for ch in range(n_ch)], axis=1)
    fwd_all = _dot(fwd_ref[...], g_cat)
    hi_t, lo_t = _split_bf16(lgt_ref[...])
    gt_cat = jnp.concatenate(
        [jnp.concatenate([hi_t[:, ch * c:(ch + 1) * c], lo_t[:, ch * c:(ch + 1) * c]], axis=1)
         for ch in range(n_ch)], axis=0)
    rev_all = _dot(gt_cat, rev_ref[...])

    for hd in range(GLA_HEADS):
        ks = slice(hd * dkh, (hd + 1) * dkh)
        vs = slice(hd * dvh, (hd + 1) * dvh)
        state = state_ref[hd]
        for ch in range(n_ch):
            rows = slice(ch * c, (ch + 1) * c)
            kcols = slice(ch * dk + hd * dkh, ch * dk + (hd + 1) * dkh)
            q = q_ref[rows, ks].astype(F32) * GLA_SCALE
            kt = kt_ref[ks, rows].astype(F32)
            v = v_ref[rows, vs]
            kt_bf = kt.astype(BF16)
            scores = _dot(q.astype(BF16), kt_bf) * mask_ref[n_lv]
            qd = q * jnp.exp(lg[rows, ks])
            scores = scores + _dot(qd.astype(BF16), kt_bf) * mask_ref[0]
            for lv in range(1, n_lv):
                qd = q * jnp.exp(fwd_all[(lv - 1) * c:lv * c, kcols])
                kd = kt * jnp.exp(rev_all[kcols, (lv - 1) * c:lv * c])
                scores = scores + _dot(qd.astype(BF16), kd.astype(BF16)) * mask_ref[lv]
            q_cum = q * jnp.exp(fwd_all[(n_lv - 1) * c:n_lv * c, kcols])
            out = _dot(q_cum.astype(BF16), state.astype(BF16)) + _dot(scores.astype(BF16), v)
            k_rest = kt * jnp.exp(rev_all[kcols, (n_lv - 1) * c:n_lv * c])
            keep = jnp.exp(rev_all[kcols, n_lv * c:(n_lv + 1) * c])
            keep = jnp.concatenate([keep] * (dvh // c), axis=1)
            state = keep * state + _dot(k_rest.astype(BF16), v)
            o = _rms(out, hn_ref[:, vs])
            og = og_ref[rows, vs].astype(F32)
            a_ref[rows, vs] = (o * (og * (1.0 / (1.0 + jnp.exp(-og))))).astype(BF16)
        state_ref[hd] = state


def _gla_chunk(q, kt, v, og, lg, lgt, head_norm):
    b, s, dk = q.shape
    dv = v.shape[-1]
    c = GLA_TILE * GLA_STEP_CHUNKS
    fwd, rev, masks = _gla_constants()
    row = lambda bi, i: (bi, i, 0)
    colmajor = lambda bi, i: (bi, 0, i)
    return pl.pallas_call(
        _gla_chunk_kernel,
        grid=(b, s // c),
        in_specs=[
            pl.BlockSpec((None, c, dk), row),
            pl.BlockSpec((None, dk, c), colmajor),
            pl.BlockSpec((None, c, dv), row),
            pl.BlockSpec((None, c, dv), row),
            pl.BlockSpec((None, c, dk), row),
            pl.BlockSpec((None, dk, c), colmajor),
            _resident(fwd.shape),
            _resident(rev.shape),
            _resident(masks.shape),
            _resident((1, dv)),
        ],
        out_specs=pl.BlockSpec((None, c, dv), row),
        out_shape=jax.ShapeDtypeStruct((b, s, dv), BF16),
        scratch_shapes=[pltpu.VMEM((GLA_HEADS, GLA_DK_HEAD, GLA_DV_HEAD), F32)],
        compiler_params=_params(("parallel", "arbitrary")),
        name="gla_chunk",
    )(q, kt, v, og, lg, lgt, fwd, rev, masks, head_norm)


def _post_kernel(a_ref, h_ref, wo_ref, g_ref, wup_ref, wdn_ref, o_ref, acc_ref):
    half = TOKEN_TILE // 2
    halves = (slice(0, half), slice(half, TOKEN_TILE))
    h1 = []
    xn = []
    for rows in halves:
        mixed = _dot(a_ref[rows, :], wo_ref[...])
        h1.append(h_ref[rows, :] + _rms(mixed, g_ref[1:2, :]))
        xn.append(_rms(h1[-1], g_ref[2:3, :]).astype(BF16))
    xn_full = jnp.concatenate(xn, axis=0)
    n_chunks = D_FF // FF_CHUNK
    for ci in range(n_chunks):
        cols = slice(ci * FF_CHUNK, (ci + 1) * FF_CHUNK)
        if ci == 0:
            up = jnp.concatenate([_dot(part, wup_ref[:, cols]) for part in xn], axis=0)
        else:
            up = _dot(xn_full, wup_ref[:, cols])
        up = jnp.maximum(up, 0.0)
        act = (up * up).astype(BF16)
        if ci == 0:
            acc_ref[...] = _dot(act, wdn_ref[cols, :])
        elif ci < n_chunks - 1:
            acc_ref[...] += _dot(act, wdn_ref[cols, :])
        else:
            for hi, rows in enumerate(halves):
                ffn = acc_ref[rows, :] + _dot(act[rows, :], wdn_ref[cols, :])
                o_ref[rows, :] = h1[hi] + _rms(ffn, g_ref[3:4, :])


def _post(a, h, w_o, gains, w_up, w_down):
    b, s, d = h.shape
    tm = TOKEN_TILE
    row = lambda bi, i: (bi, i, 0)
    return pl.pallas_call(
        _post_kernel,
        grid=(b, s // tm),
        in_specs=[
            pl.BlockSpec((None, tm, d), row),
            pl.BlockSpec((None, tm, d), row),
            _resident((d, d)),
            _resident((4, d)),
            _resident((d, D_FF)),
            _resident((D_FF, d)),
        ],
        out_specs=pl.BlockSpec((None, tm, d), row),
        out_shape=jax.ShapeDtypeStruct((b, s, d), F32),
        scratch_shapes=[pltpu.VMEM((tm, d), F32)],
        compiler_params=_params(("parallel", "parallel")),
        name="post_ffn",
    )(a, h, w_o, gains, w_up, w_down)


def kernel(x, norm_gains, sb_w_qkv, sb_w_o, conv_w_in, conv_w, conv_w_out, gla_w_in,
           gla_w_gate_up, gla_b_gate, gla_head_norm, gla_w_o, ffn_w_up, ffn_w_down):
    depth = norm_gains.shape[0]
    h = x
    for i in range(depth):
        kind, j = i % 3, i // 3
        gains = norm_gains[i]
        pre_gain = gains[0:1]
        if kind == 0:
            a = _sb_mixer(h, pre_gain, sb_w_qkv[j].astype(BF16))
            w_o = sb_w_o[j]
        elif kind == 1:
            a = _conv_mixer(h, pre_gain, conv_w_in[j].astype(BF16), conv_w[j])
            w_o = conv_w_out[j]
        else:
            n_main = 2 * GLA_DK + 2 * GLA_DV
            w_in = gla_w_in[j]
            w_a = jnp.pad(w_in[:, n_main:], ((0, 0), (0, LANES - GLA_GATE_RANK))).astype(BF16)
            w_gu = jnp.pad(gla_w_gate_up[j], ((0, LANES - GLA_GATE_RANK), (0, 0))).astype(BF16)
            q, kt, v, og, lg, lgt = _gla_proj(h, pre_gain, w_in[:, :n_main].astype(BF16), w_a,
                                              w_gu, gla_b_gate[j][None, :])
            a = _gla_chunk(q, kt, v, og, lg, lgt, gla_head_norm[j].reshape(1, GLA_DV))
            w_o = gla_w_o[j]
        h = _post(a, h, w_o.astype(BF16), gains, ffn_w_up[i].astype(BF16),
                  ffn_w_down[i].astype(BF16))
    return h
```

```python
import math

import numpy as np
import jax
import jax.numpy as jnp
from jax import lax
from jax.experimental import pallas as pl
from jax.experimental.pallas import tpu as pltpu

F32 = jnp.float32
BF16 = jnp.bfloat16

D_MODEL = 1024
D_FF = 4 * D_MODEL
RMS_EPS = 1e-6

SB_HEADS = 16
SB_HEAD_DIM = D_MODEL // SB_HEADS
LOG2E = math.log2(math.e)
SB_Q_SCALE = -(SB_HEAD_DIM ** -0.5) * LOG2E
CONV_WIDTH = 3
GLA_HEADS = 4
GLA_DK = D_MODEL // 2
GLA_DV = D_MODEL
GLA_DK_HEAD = GLA_DK // GLA_HEADS
GLA_DV_HEAD = GLA_DV // GLA_HEADS
GLA_GATE_RANK = 16
GLA_GATE_NORMALIZER = 16.0
GLA_SCALE = GLA_DK_HEAD ** -0.5

LANES = 128
SUBLANES = 8
VMEM_LIMIT_BYTES = 56 * 1024 * 1024

TOKEN_TILE = 512
FF_CHUNK = 1024
SB_TILE = 512
SB_SUB = 128
SB_BAND_SUBS = 3
SB_GROUP = 2
SB_PAD_SUBS = SB_BAND_SUBS - 1
SB_HP_PER_STEP = 2
SB_F32_EXP2_ZERO = -152.0
GLA_TILE = 128
GLA_STEP_CHUNKS = 2
GLA_LEVELS = 7


def _rms(x, gain):
    ms = jnp.mean(x * x, axis=-1, keepdims=True)
    return x * lax.rsqrt(ms + RMS_EPS) * gain


def _neg_abs(z):
    bits = lax.bitcast_convert_type(z, jnp.uint32) | jnp.uint32(0x80000000)
    return lax.bitcast_convert_type(bits, F32)


def _log_sigmoid(z):
    return jnp.minimum(z, 0.0) - jnp.log(1.0 + jnp.exp(_neg_abs(z)))


def _log2_one_minus_sigmoid(y):
    return jnp.minimum(y, 0.0) - jnp.log(1.0 + jnp.exp2(_neg_abs(y))) * LOG2E


def _split_bf16(x):
    hi = x.astype(BF16)
    lo = (x - hi.astype(F32)).astype(BF16)
    return hi, lo


def _dot(a, b):
    return jnp.dot(a, b, preferred_element_type=F32)


def _resident(shape):
    zeros = (0,) * len(shape)
    return pl.BlockSpec(shape, lambda *_: zeros, pipeline_mode=pl.Buffered(1))


def _params(semantics):
    return pltpu.CompilerParams(dimension_semantics=semantics,
                                vmem_limit_bytes=VMEM_LIMIT_BYTES)


def _sb_suffix_matrix():
    t = SB_SUB
    m = np.arange(t)[:, None]
    j = np.arange(t)[None, :]
    half = np.concatenate([(m >= j).astype(np.float32), np.ones((t, t), np.float32)], axis=1)
    return jnp.asarray(np.concatenate([half, half], axis=0), dtype=BF16)


def _sb_layer_kernel(x0_ref, x1_ref, x2_ref, g_ref, w_ref, u_ref, o_ref,
                     q_scr, kt_scr, v_scr, kt_new, v_new, xn_buf, o_scr, acc_ref, c_ref):
    t = SB_TILE
    sub = SB_SUB
    n_rb = t // sub
    band = SB_BAND_SUBS
    pad = SB_PAD_SUBS
    grp = SB_GROUP
    win = band + grp - 1
    pairs = n_rb // grp
    grp_rows = grp * sub
    n_hp = D_MODEL // LANES
    duo = SB_HP_PER_STEP
    i = pl.program_id(1)
    step = pl.program_id(0) * pl.num_programs(1) + i
    gain = g_ref[...]
    u = u_ref[...]

    def project_q(xn, j, slot):
        y = _dot(xn, w_ref[j, :, 0:duo * LANES]) * SB_Q_SCALE
        for e in range(duo):
            q_scr[slot, j * duo + e] = y[:, e * LANES:(e + 1) * LANES].astype(BF16)

    def project_k(xn, j):
        y = _dot(xn, w_ref[j, :, duo * LANES:2 * duo * LANES])
        for e in range(duo):
            for kb in range(n_rb):
                kt_new[j * duo + e, kb] = (
                    y[kb * sub:(kb + 1) * sub, e * LANES:(e + 1) * LANES].T.astype(BF16))

    def project_v(xn, j):
        y = _dot(xn, w_ref[j, :, 2 * duo * LANES:3 * duo * LANES]).astype(BF16)
        for e in range(duo):
            v_new[j * duo + e] = y[:, e * LANES:(e + 1) * LANES]

    @pl.when(i == 0)
    def _():
        kt_scr[:, 0:pad] = jnp.zeros((n_hp, pad, LANES, sub), BF16)
        v_scr[:, 0:pad * sub, :] = jnp.zeros((n_hp, pad * sub, LANES), BF16)

    @pl.when(step == 0)
    def _():
        xn = _rms(x0_ref[...], gain).astype(BF16)

        def first_tile(j, carry):
            project_q(xn, j, 0)
            project_k(xn, j)
            project_v(xn, j)
            return carry

        lax.fori_loop(0, n_hp // duo, first_tile, 0)
        xn_buf[0] = _rms(x1_ref[...], gain).astype(BF16)

    slot = step % 2
    next_slot = (step + 1) % 2
    piece = t // (n_hp // duo)

    lane = lax.broadcasted_iota(jnp.int32, (t, LANES), 1)
    first = lane < SB_HEAD_DIM
    row = lax.broadcasted_iota(jnp.int32, (sub, sub), 0)
    col = lax.broadcasted_iota(jnp.int32, (sub, sub), 1)
    diag = col < row
    no_weight = jnp.zeros((sub, sub), BF16)
    row_blk = lax.broadcasted_iota(jnp.int32, (t, sub), 0) // sub
    not_first_tile = i > 0
    has_keys_further_left = row_blk + i * n_rb >= band

    def chunk(qh, kblk, vblk, c, mask):
        y = _dot(qh, kblk)
        part = jnp.where(mask, _log2_one_minus_sigmoid(y), 0.0)
        hi, lo = _split_bf16(part)
        s2 = _dot(jnp.concatenate([hi, lo], axis=1), u)
        w = jnp.where(mask, jnp.exp2(s2[:, :sub] + c - y), 0.0)
        return _dot(w.astype(BF16), vblk), c + s2[:, sub:]

    def append_projected(hp):
        for kb in range(n_rb):
            kt_scr[hp, pad + i * n_rb + kb] = kt_new[hp, kb]
        v_scr[hp, pl.ds(pl.multiple_of((pad + i * n_rb) * sub, sub), t), :] = v_new[hp]

    def split_heads(hp):
        q = q_scr[slot, hp]
        zero = jnp.zeros_like(q)
        return jnp.where(first, q, zero), jnp.where(first, zero, q)

    def band_logits(hp, q_heads):
        parts = []
        tiles = {}
        for p in range(pairs):
            first_sub = pad + i * n_rb + p * grp - band + 1
            kwin = jnp.concatenate([kt_scr[hp, first_sub + s] for s in range(win)], axis=1)
            q_both = jnp.concatenate(
                [q_heads[hd][p * grp_rows:(p + 1) * grp_rows] for hd in range(2)], axis=0)
            y = _dot(q_both, kwin)
            for hd in range(2):
                for rr in range(grp):
                    top = hd * grp_rows + rr * sub
                    for s in range(band):
                        yt = y[top:top + sub, (rr + s) * sub:(rr + s + 1) * sub]
                        part = _log2_one_minus_sigmoid(yt)
                        mask = diag if s == band - 1 else None
                        if p * grp + rr + s < band - 1:
                            mask = not_first_tile if mask is None else mask & not_first_tile
                        if mask is not None:
                            part = jnp.where(mask, part, 0.0)
                        hi, lo = _split_bf16(part)
                        tiles[p, hd, rr, s] = (len(parts), yt, mask)
                        parts.append(jnp.concatenate([hi, lo], axis=1))
        return jnp.concatenate(parts, axis=0), tiles

    def band_weights(e, hp, s2, tiles, p):
        first_sub = pad + i * n_rb + p * grp - band + 1
        w_rows = []
        for hd in range(2):
            for rr in range(grp):
                r = p * grp + rr
                c = jnp.zeros((sub, sub), F32)
                ws = [None] * band
                for s in reversed(range(band)):
                    index, yt, mask = tiles[p, hd, rr, s]
                    blk = s2[index * sub:(index + 1) * sub]
                    w = jnp.exp2(blk[:, :sub] + c - yt)
                    if mask is not None:
                        w = jnp.where(mask, w, 0.0)
                    ws[s] = w.astype(BF16)
                    c = c + blk[:, sub:]
                c_ref[e, hd, r * sub:(r + 1) * sub, :] = c
                w_rows.append(jnp.concatenate(
                    [no_weight] * rr + ws + [no_weight] * (grp - 1 - rr), axis=1))
        vwin = v_scr[hp, pl.ds(pl.multiple_of(first_sub * sub, sub), win * sub), :]
        pv = _dot(jnp.concatenate(w_rows, axis=0), vwin)
        for hd in range(2):
            acc_ref[e, hd, p * grp_rows:(p + 1) * grp_rows, :] = pv[hd * grp_rows:(hd + 1) * grp_rows]

    def further_left(e, hp, q_heads):
        def more_keys_matter(state):
            k, c_max = state
            return jnp.logical_and(k >= 0, c_max > SB_F32_EXP2_ZERO)

        def one_more_sub_block(state):
            k, _ = state
            not_yet_seen = row_blk >= k - i * n_rb + band
            kblk = kt_scr[hp, pad + k]
            vblk = v_scr[hp, pl.ds(pl.multiple_of((pad + k) * sub, sub), sub), :]
            for hd in range(2):
                pv, c = chunk(q_heads[hd], kblk, vblk, c_ref[e, hd], not_yet_seen)
                acc_ref[e, hd] += pv
                c_ref[e, hd] = c
            return k - 1, largest_open_carry()

        def largest_open_carry():
            return jnp.max(jnp.where(has_keys_further_left, c_ref[e], -jnp.inf))

        k_start = i * n_rb + n_rb - 1 - band
        lax.while_loop(more_keys_matter, one_more_sub_block, (k_start, largest_open_carry()))
        o_scr[hp] = jnp.where(first, acc_ref[e, 0], acc_ref[e, 1]).astype(BF16)

    def per_head_pair_duo(j, carry):
        hps = [j * duo + e for e in range(duo)]
        xn_next = xn_buf[slot]
        for hp in hps:
            append_projected(hp)
        q_heads = [split_heads(hp) for hp in hps]
        stacked = [band_logits(hp, q_heads[e]) for e, hp in enumerate(hps)]
        project_q(xn_next, j, next_slot)
        project_k(xn_next, j)
        s2 = []
        for e in range(duo):
            s2.append(_dot(stacked[e][0], u))
            if e == 0:
                project_v(xn_next, j)
        for e, hp in enumerate(hps):
            for p in range(pairs):
                band_weights(e, hp, s2[e], stacked[e][1], p)
        rows = pl.ds(pl.multiple_of(j * piece, piece), piece)
        xn_buf[next_slot, rows, :] = _rms(x2_ref[rows, :], gain).astype(BF16)
        for e, hp in enumerate(hps):
            further_left(e, hp, q_heads[e])
        return carry

    lax.fori_loop(0, n_hp // duo, per_head_pair_duo, 0)
    o_ref[...] = jnp.concatenate([o_scr[hp] for hp in range(n_hp)], axis=1)


def _sb_mixer(h, gain, w_qkv):
    b, s, d = h.shape
    t = SB_TILE
    n_hp = d // LANES
    duo = SB_HP_PER_STEP
    n_tiles = s // t
    n_subs = SB_PAD_SUBS + s // SB_SUB
    w_duo = (w_qkv.reshape(d, 3, n_hp // duo, duo * LANES).transpose(2, 0, 1, 3)
             .reshape(n_hp // duo, d, 3 * duo * LANES))

    def tile_after_next(bi, i):
        tile = jnp.minimum(bi * n_tiles + i + 2, b * n_tiles - 1)
        return tile // n_tiles, tile % n_tiles, 0

    return pl.pallas_call(
        _sb_layer_kernel,
        grid=(b, n_tiles),
        in_specs=[
            pl.BlockSpec((None, t, d), lambda bi, i: (0, 0, 0)),
            pl.BlockSpec((None, t, d), lambda bi, i: (0, 1, 0)),
            pl.BlockSpec((None, t, d), tile_after_next),
            _resident((1, d)),
            _resident((n_hp // duo, d, 3 * duo * LANES)),
            _resident((2 * SB_SUB, 2 * SB_SUB)),
        ],
        out_specs=pl.BlockSpec((None, t, d), lambda bi, i: (bi, i, 0)),
        out_shape=jax.ShapeDtypeStruct((b, s, d), BF16),
        scratch_shapes=[
            pltpu.VMEM((2, n_hp, t, LANES), BF16),
            pltpu.VMEM((n_hp, n_subs, LANES, SB_SUB), BF16),
            pltpu.VMEM((n_hp, n_subs * SB_SUB, LANES), BF16),
            pltpu.VMEM((n_hp, t // SB_SUB, LANES, SB_SUB), BF16),
            pltpu.VMEM((n_hp, t, LANES), BF16),
            pltpu.VMEM((2, t, d), BF16),
            pltpu.VMEM((n_hp, t, LANES), BF16),
            pltpu.VMEM((duo, 2, t, LANES), F32),
            pltpu.VMEM((duo, 2, t, SB_SUB), F32),
        ],
        compiler_params=_params(("arbitrary", "arbitrary")),
        name="sb_layer",
    )(h, h, h, gain, w_duo, _sb_suffix_matrix())


def _conv_kernel(x_ref, xp_ref, g_ref, w_ref, cw_ref, a_ref, hbuf_ref):
    d = D_MODEL
    tm = TOKEN_TILE
    i = pl.program_id(1)
    gain = g_ref[...]
    xn = _rms(x_ref[...], gain).astype(BF16)
    xpn = _rms(xp_ref[...], gain).astype(BF16)
    hbuf_ref[SUBLANES:, :] = _dot(xn, w_ref[:, d:2 * d]) * _dot(xn, w_ref[:, 2 * d:3 * d])
    h_prev = _dot(xpn, w_ref[:, d:2 * d]) * _dot(xpn, w_ref[:, 2 * d:3 * d])
    hbuf_ref[0:SUBLANES, :] = jnp.where(i > 0, h_prev, 0.0)
    conv = cw_ref[0:1, :] * hbuf_ref[pl.ds(SUBLANES - 2, tm), :]
    conv = conv + cw_ref[1:2, :] * hbuf_ref[pl.ds(SUBLANES - 1, tm), :]
    conv = conv + cw_ref[2:3, :] * hbuf_ref[pl.ds(SUBLANES, tm), :]
    a_ref[...] = (_dot(xn, w_ref[:, 0:d]) * conv).astype(BF16)


def _conv_mixer(h, gain, w_in, conv_w):
    b, s, d = h.shape
    tm = TOKEN_TILE
    rows_per_tile = tm // SUBLANES
    return pl.pallas_call(
        _conv_kernel,
        grid=(b, s // tm),
        in_specs=[
            pl.BlockSpec((None, tm, d), lambda bi, i: (bi, i, 0)),
            pl.BlockSpec((None, SUBLANES, d),
                         lambda bi, i: (bi, jnp.maximum(i * rows_per_tile - 1, 0), 0)),
            _resident((1, d)),
            _resident((d, 3 * d)),
            _resident((CONV_WIDTH, d)),
        ],
        out_specs=pl.BlockSpec((None, tm, d), lambda bi, i: (bi, i, 0)),
        out_shape=jax.ShapeDtypeStruct((b, s, d), BF16),
        scratch_shapes=[pltpu.VMEM((tm + SUBLANES, d), F32)],
        compiler_params=_params(("parallel", "parallel")),
        name="conv_mixer",
    )(h, h, gain, w_in, conv_w)


def _gla_proj_kernel(x_ref, g_ref, w_ref, wa_ref, wgu_ref, bg_ref,
                     q_ref, kt_ref, v_ref, og_ref, lg_ref, lgt_ref):
    dk, dv = GLA_DK, GLA_DV
    xn = _rms(x_ref[...], g_ref[...]).astype(BF16)
    q_ref[...] = _dot(xn, w_ref[:, 0:dk]).astype(BF16)
    kt_ref[...] = _dot(xn, w_ref[:, dk:2 * dk]).T.astype(BF16)
    v_ref[...] = _dot(xn, w_ref[:, 2 * dk:2 * dk + dv]).astype(BF16)
    og_ref[...] = _dot(xn, w_ref[:, 2 * dk + dv:2 * dk + 2 * dv]).astype(BF16)
    a_low = _dot(xn, wa_ref[...]).astype(BF16)
    pre = _dot(a_low, wgu_ref[...]) + bg_ref[...]
    lg = _log_sigmoid(pre) * (1.0 / GLA_GATE_NORMALIZER)
    lg_ref[...] = lg
    lgt_ref[...] = lg.T


def _gla_proj(h, gain, w_main, w_a, w_gu, b_gate):
    b, s, d = h.shape
    tm = TOKEN_TILE
    dk, dv = GLA_DK, GLA_DV
    row = lambda bi, i: (bi, i, 0)
    colmajor = lambda bi, i: (bi, 0, i)
    return pl.pallas_call(
        _gla_proj_kernel,
        grid=(b, s // tm),
        in_specs=[
            pl.BlockSpec((None, tm, d), row),
            _resident((1, d)),
            _resident((d, 2 * dk + 2 * dv)),
            _resident((d, LANES)),
            _resident((LANES, dk)),
            _resident((1, dk)),
        ],
        out_specs=[
            pl.BlockSpec((None, tm, dk), row),
            pl.BlockSpec((None, dk, tm), colmajor),
            pl.BlockSpec((None, tm, dv), row),
            pl.BlockSpec((None, tm, dv), row),
            pl.BlockSpec((None, tm, dk), row),
            pl.BlockSpec((None, dk, tm), colmajor),
        ],
        out_shape=[
            jax.ShapeDtypeStruct((b, s, dk), BF16),
            jax.ShapeDtypeStruct((b, dk, s), BF16),
            jax.ShapeDtypeStruct((b, s, dv), BF16),
            jax.ShapeDtypeStruct((b, s, dv), BF16),
            jax.ShapeDtypeStruct((b, s, dk), F32),
            jax.ShapeDtypeStruct((b, dk, s), F32),
        ],
        compiler_params=_params(("parallel", "parallel")),
        name="gla_proj",
    )(h, gain, w_main, w_a, w_gu, b_gate)


def _gla_constants():
    c = GLA_TILE
    i = np.arange(c)[:, None]
    m = np.arange(c)[None, :]
    fwd, rev, masks = [], [], []
    for level in range(1, GLA_LEVELS + 1):
        n = 1 << level
        same = (i // n) == (m // n)
        fwd.append((same & (m <= i)).astype(np.float32))
        rev.append((same & (m > i)).astype(np.float32).T)
    rev.append(np.ones((c, c), np.float32))
    for level in range(GLA_LEVELS):
        n = 1 << level
        masks.append(((i // (2 * n) == m // (2 * n)) & ((i // n) % 2 == 1)
                      & ((m // n) % 2 == 0)).astype(np.float32))
    masks.append((i == m).astype(np.float32))
    fwd = np.concatenate(fwd, axis=0)
    rev = np.concatenate(rev, axis=1)
    fwd2 = np.concatenate([fwd, fwd], axis=1)
    rev2 = np.concatenate([rev, rev], axis=0)
    return (jnp.asarray(fwd2, dtype=BF16), jnp.asarray(rev2, dtype=BF16),
            jnp.asarray(np.stack(masks), dtype=F32))


def _gla_chunk_kernel(q_ref, kt_ref, v_ref, og_ref, lg_ref, lgt_ref, fwd_ref, rev_ref,
                      mask_ref, hn_ref, a_ref, state_ref):
    c = GLA_TILE
    n_ch = GLA_STEP_CHUNKS
    dk = GLA_DK
    dkh, dvh = GLA_DK_HEAD, GLA_DV_HEAD
    n_lv = GLA_LEVELS

    @pl.when(pl.program_id(1) == 0)
    def _():
        state_ref[...] = jnp.zeros_like(state_ref)

    lg = lg_ref[...]
    hi, lo = _split_bf16(lg)
    g_cat = jnp.concatenate(
        [jnp.concatenate([hi[ch * c:(ch + 1) * c], lo[ch * c:(ch + 1) * c]], axis=0)
         for ch in range(n_ch)], axis=1)
    fwd_all = _dot(fwd_ref[...], g_cat)
    hi_t, lo_t = _split_bf16(lgt_ref[...])
    gt_cat = jnp.concatenate(
        [jnp.concatenate([hi_t[:, ch * c:(ch + 1) * c], lo_t[:, ch * c:(ch + 1) * c]], axis=1)
         for ch in range(n_ch)], axis=0)
    rev_all = _dot(gt_cat, rev_ref[...])

    for hd in range(GLA_HEADS):
        ks = slice(hd * dkh, (hd + 1) * dkh)
        vs = slice(hd * dvh, (hd + 1) * dvh)
        state = state_ref[hd]
        for ch in range(n_ch):
            rows = slice(ch * c, (ch + 1) * c)
            kcols = slice(ch * dk + hd * dkh, ch * dk + (hd + 1) * dkh)
            q = q_ref[rows, ks].astype(F32) * GLA_SCALE
            kt = kt_ref[ks, rows].astype(F32)
            v = v_ref[rows, vs]
            kt_bf = kt.astype(BF16)
            scores = _dot(q.astype(BF16), kt_bf) * mask_ref[n_lv]
            qd = q * jnp.exp(lg[rows, ks])
            scores = scores + _dot(qd.astype(BF16), kt_bf) * mask_ref[0]
            for lv in range(1, n_lv):
                qd = q * jnp.exp(fwd_all[(lv - 1) * c:lv * c, kcols])
                kd = kt * jnp.exp(rev_all[kcols, (lv - 1) * c:lv * c])
                scores = scores + _dot(qd.astype(BF16), kd.astype(BF16)) * mask_ref[lv]
            q_cum = q * jnp.exp(fwd_all[(n_lv - 1) * c:n_lv * c, kcols])
            out = _dot(q_cum.astype(BF16), state.astype(BF16)) + _dot(scores.astype(BF16), v)
            k_rest = kt * jnp.exp(rev_all[kcols, (n_lv - 1) * c:n_lv * c])
            keep = jnp.exp(rev_all[kcols, n_lv * c:(n_lv + 1) * c])
            keep = jnp.concatenate([keep] * (dvh // c), axis=1)
            state = keep * state + _dot(k_rest.astype(BF16), v)
            o = _rms(out, hn_ref[:, vs])
            og = og_ref[rows, vs].astype(F32)
            a_ref[rows, vs] = (o * (og * (1.0 / (1.0 + jnp.exp(-og))))).astype(BF16)
        state_ref[hd] = state


def _gla_chunk(q, kt, v, og, lg, lgt, head_norm):
    b, s, dk = q.shape
    dv = v.shape[-1]
    c = GLA_TILE * GLA_STEP_CHUNKS
    fwd, rev, masks = _gla_constants()
    row = lambda bi, i: (bi, i, 0)
    colmajor = lambda bi, i: (bi, 0, i)
    return pl.pallas_call(
        _gla_chunk_kernel,
        grid=(b, s // c),
        in_specs=[
            pl.BlockSpec((None, c, dk), row),
            pl.BlockSpec((None, dk, c), colmajor),
            pl.BlockSpec((None, c, dv), row),
            pl.BlockSpec((None, c, dv), row),
            pl.BlockSpec((None, c, dk), row),
            pl.BlockSpec((None, dk, c), colmajor),
            _resident(fwd.shape),
            _resident(rev.shape),
            _resident(masks.shape),
            _resident((1, dv)),
        ],
        out_specs=pl.BlockSpec((None, c, dv), row),
        out_shape=jax.ShapeDtypeStruct((b, s, dv), BF16),
        scratch_shapes=[pltpu.VMEM((GLA_HEADS, GLA_DK_HEAD, GLA_DV_HEAD), F32)],
        compiler_params=_params(("parallel", "arbitrary")),
        name="gla_chunk",
    )(q, kt, v, og, lg, lgt, fwd, rev, masks, head_norm)


def _post_kernel(a_ref, h_ref, wo_ref, g_ref, wup_ref, wdn_ref, o_ref, acc_ref):
    half = TOKEN_TILE // 2
    halves = (slice(0, half), slice(half, TOKEN_TILE))
    h1 = []
    xn = []
    for rows in halves:
        mixed = _dot(a_ref[rows, :], wo_ref[...])
        h1.append(h_ref[rows, :] + _rms(mixed, g_ref[1:2, :]))
        xn.append(_rms(h1[-1], g_ref[2:3, :]).astype(BF16))
    xn_full = jnp.concatenate(xn, axis=0)
    n_chunks = D_FF // FF_CHUNK
    for ci in range(n_chunks):
        cols = slice(ci * FF_CHUNK, (ci + 1) * FF_CHUNK)
        if ci == 0:
            up = jnp.concatenate([_dot(part, wup_ref[:, cols]) for part in xn], axis=0)
        else:
            up = _dot(xn_full, wup_ref[:, cols])
        up = jnp.maximum(up, 0.0)
        act = (up * up).astype(BF16)
        if ci == 0:
            acc_ref[...] = _dot(act, wdn_ref[cols, :])
        elif ci < n_chunks - 1:
            acc_ref[...] += _dot(act, wdn_ref[cols, :])
        else:
            for hi, rows in enumerate(halves):
                ffn = acc_ref[rows, :] + _dot(act[rows, :], wdn_ref[cols, :])
                o_ref[rows, :] = h1[hi] + _rms(ffn, g_ref[3:4, :])


def _post(a, h, w_o, gains, w_up, w_down):
    b, s, d = h.shape
    tm = TOKEN_TILE
    row = lambda bi, i: (bi, i, 0)
    return pl.pallas_call(
        _post_kernel,
        grid=(b, s // tm),
        in_specs=[
            pl.BlockSpec((None, tm, d), row),
            pl.BlockSpec((None, tm, d), row),
            _resident((d, d)),
            _resident((4, d)),
            _resident((d, D_FF)),
            _resident((D_FF, d)),
        ],
        out_specs=pl.BlockSpec((None, tm, d), row),
        out_shape=jax.ShapeDtypeStruct((b, s, d), F32),
        scratch_shapes=[pltpu.VMEM((tm, d), F32)],
        compiler_params=_params(("parallel", "parallel")),
        name="post_ffn",
    )(a, h, w_o, gains, w_up, w_down)


def kernel(x, norm_gains, sb_w_qkv, sb_w_o, conv_w_in, conv_w, conv_w_out, gla_w_in,
           gla_w_gate_up, gla_b_gate, gla_head_norm, gla_w_o, ffn_w_up, ffn_w_down):
    depth = norm_gains.shape[0]
    h = x
    for i in range(depth):
        kind, j = i % 3, i // 3
        gains = norm_gains[i]
        pre_gain = gains[0:1]
        if kind == 0:
            a = _sb_mixer(h, pre_gain, sb_w_qkv[j].astype(BF16))
            w_o = sb_w_o[j]
        elif kind == 1:
            a = _conv_mixer(h, pre_gain, conv_w_in[j].astype(BF16), conv_w[j])
            w_o = conv_w_out[j]
        else:
            n_main = 2 * GLA_DK + 2 * GLA_DV
            w_in = gla_w_in[j]
            w_a = jnp.pad(w_in[:, n_main:], ((0, 0), (0, LANES - GLA_GATE_RANK))).astype(BF16)
            w_gu = jnp.pad(gla_w_gate_up[j], ((0, LANES - GLA_GATE_RANK), (0, 0))).astype(BF16)
            q, kt, v, og, lg, lgt = _gla_proj(h, pre_gain, w_in[:, :n_main].astype(BF16), w_a,
                                              w_gu, gla_b_gate[j][None, :])
            a = _gla_chunk(q, kt, v, og, lg, lgt, gla_head_norm[j].reshape(1, GLA_DV))
            w_o = gla_w_o[j]
        h = _post(a, h, w_o.astype(BF16), gains, ffn_w_up[i].astype(BF16),
                  ffn_w_down[i].astype(BF16))
    return h
```

```python
import math

import numpy as np
import jax
import jax.numpy as jnp
from jax import lax
from jax.experimental import pallas as pl
from jax.experimental.pallas import tpu as pltpu

F32 = jnp.float32
BF16 = jnp.bfloat16

D_MODEL = 1024
D_FF = 4 * D_MODEL
RMS_EPS = 1e-6

SB_HEADS = 16
SB_HEAD_DIM = D_MODEL // SB_HEADS
LOG2E = math.log2(math.e)
SB_Q_SCALE = -(SB_HEAD_DIM ** -0.5) * LOG2E
CONV_WIDTH = 3
GLA_HEADS = 4
GLA_DK = D_MODEL // 2
GLA_DV = D_MODEL
GLA_DK_HEAD = GLA_DK // GLA_HEADS
GLA_DV_HEAD = GLA_DV // GLA_HEADS
GLA_GATE_RANK = 16
GLA_GATE_NORMALIZER = 16.0
GLA_SCALE = GLA_DK_HEAD ** -0.5

LANES = 128
SUBLANES = 8
VMEM_LIMIT_BYTES = 58 * 1024 * 1024

TOKEN_TILE = 512
FF_CHUNK = 1024
SB_TILE = 512
SB_SUB = 128
SB_BAND_SUBS = 3
SB_GROUP = 2
SB_PAD_SUBS = SB_BAND_SUBS - 1
SB_HP_PER_STEP = 2
SB_F32_EXP2_ZERO = -152.0
GLA_TILE = 128
GLA_STEP_CHUNKS = 2
GLA_LEVELS = 7


def _rms(x, gain):
    ms = jnp.mean(x * x, axis=-1, keepdims=True)
    return x * lax.rsqrt(ms + RMS_EPS) * gain


def _neg_abs(z):
    bits = lax.bitcast_convert_type(z, jnp.uint32) | jnp.uint32(0x80000000)
    return lax.bitcast_convert_type(bits, F32)


def _log_sigmoid(z):
    return jnp.minimum(z, 0.0) - jnp.log(1.0 + jnp.exp(_neg_abs(z)))


def _log2_one_minus_sigmoid(y):
    return jnp.minimum(y, 0.0) - jnp.log(1.0 + jnp.exp2(_neg_abs(y))) * LOG2E


def _split_bf16(x):
    hi = x.astype(BF16)
    lo = (x - hi.astype(F32)).astype(BF16)
    return hi, lo


def _dot(a, b):
    return jnp.dot(a, b, preferred_element_type=F32)


def _resident(shape):
    zeros = (0,) * len(shape)
    return pl.BlockSpec(shape, lambda *_: zeros, pipeline_mode=pl.Buffered(1))


def _params(semantics):
    return pltpu.CompilerParams(dimension_semantics=semantics,
                                vmem_limit_bytes=VMEM_LIMIT_BYTES)


def _sb_suffix_matrix():
    t = SB_SUB
    m = np.arange(t)[:, None]
    j = np.arange(t)[None, :]
    half = np.concatenate([(m >= j).astype(np.float32), np.ones((t, t), np.float32)], axis=1)
    return jnp.asarray(np.concatenate([half, half], axis=0), dtype=BF16)


def _sb_layer_kernel(x0_ref, x1_ref, x2_ref, g_ref, w_ref, u_ref, o_ref,
                     q_scr, kt_scr, v_scr, kt_new, v_new, xn_buf, o_scr, acc_ref, c_ref):
    t = SB_TILE
    sub = SB_SUB
    n_rb = t // sub
    band = SB_BAND_SUBS
    pad = SB_PAD_SUBS
    grp = SB_GROUP
    win = band + grp - 1
    pairs = n_rb // grp
    grp_rows = grp * sub
    n_hp = D_MODEL // LANES
    duo = SB_HP_PER_STEP
    i = pl.program_id(1)
    step = pl.program_id(0) * pl.num_programs(1) + i
    gain = g_ref[...]
    u = u_ref[...]

    def project_q(xn, j, slot):
        y = _dot(xn, w_ref[j, :, 0:duo * LANES]) * SB_Q_SCALE
        for e in range(duo):
            q_scr[slot, j * duo + e] = y[:, e * LANES:(e + 1) * LANES].astype(BF16)

    def project_k(xn, j):
        y = _dot(xn, w_ref[j, :, duo * LANES:2 * duo * LANES])
        for e in range(duo):
            for kb in range(n_rb):
                kt_new[j * duo + e, kb] = (
                    y[kb * sub:(kb + 1) * sub, e * LANES:(e + 1) * LANES].T.astype(BF16))

    def project_v(xn, j):
        y = _dot(xn, w_ref[j, :, 2 * duo * LANES:3 * duo * LANES]).astype(BF16)
        for e in range(duo):
            v_new[j * duo + e] = y[:, e * LANES:(e + 1) * LANES]

    @pl.when(i == 0)
    def _():
        kt_scr[:, 0:pad] = jnp.zeros((n_hp, pad, LANES, sub), BF16)
        v_scr[:, 0:pad * sub, :] = jnp.zeros((n_hp, pad * sub, LANES), BF16)

    @pl.when(step == 0)
    def _():
        xn = _rms(x0_ref[...], gain).astype(BF16)

        def first_tile(j, carry):
            project_q(xn, j, 0)
            project_k(xn, j)
            project_v(xn, j)
            return carry

        lax.fori_loop(0, n_hp // duo, first_tile, 0)
        xn_buf[0] = _rms(x1_ref[...], gain).astype(BF16)

    slot = step % 2
    next_slot = (step + 1) % 2
    piece = t // (n_hp // duo)

    lane = lax.broadcasted_iota(jnp.int32, (t, LANES), 1)
    first = lane < SB_HEAD_DIM
    first_grp = lax.broadcasted_iota(jnp.int32, (grp_rows, LANES), 1) < SB_HEAD_DIM
    row = lax.broadcasted_iota(jnp.int32, (sub, sub), 0)
    col = lax.broadcasted_iota(jnp.int32, (sub, sub), 1)
    diag = col < row
    no_weight = jnp.zeros((sub, sub), BF16)
    row_blk = lax.broadcasted_iota(jnp.int32, (t, sub), 0) // sub
    not_first_tile = i > 0
    has_keys_further_left = row_blk + i * n_rb >= band

    def chunk(qh, kblk, vblk, c, mask):
        y = _dot(qh, kblk)
        part = jnp.where(mask, _log2_one_minus_sigmoid(y), 0.0)
        hi, lo = _split_bf16(part)
        s2 = _dot(jnp.concatenate([hi, lo], axis=1), u)
        w = jnp.where(mask, jnp.exp2(s2[:, :sub] + c - y), 0.0)
        return _dot(w.astype(BF16), vblk), c + s2[:, sub:]

    def append_projected(hp):
        for kb in range(n_rb):
            kt_scr[hp, pad + i * n_rb + kb] = kt_new[hp, kb]
        v_scr[hp, pl.ds(pl.multiple_of((pad + i * n_rb) * sub, sub), t), :] = v_new[hp]

    def split_heads(hp):
        q = q_scr[slot, hp]
        zero = jnp.zeros_like(q)
        return jnp.where(first, q, zero), jnp.where(first, zero, q)

    def band_logits(hp, q_heads):
        parts = []
        tiles = {}
        for p in range(pairs):
            first_sub = pad + i * n_rb + p * grp - band + 1
            kwin = jnp.concatenate([kt_scr[hp, first_sub + s] for s in range(win)], axis=1)
            q_both = jnp.concatenate(
                [q_heads[hd][p * grp_rows:(p + 1) * grp_rows] for hd in range(2)], axis=0)
            y = _dot(q_both, kwin)
            for hd in range(2):
                for rr in range(grp):
                    top = hd * grp_rows + rr * sub
                    for s in range(band):
                        yt = y[top:top + sub, (rr + s) * sub:(rr + s + 1) * sub]
                        part = _log2_one_minus_sigmoid(yt)
                        mask = diag if s == band - 1 else None
                        if p * grp + rr + s < band - 1:
                            mask = not_first_tile if mask is None else mask & not_first_tile
                        if mask is not None:
                            part = jnp.where(mask, part, 0.0)
                        hi, lo = _split_bf16(part)
                        tiles[p, hd, rr, s] = (len(parts), yt, mask)
                        parts.append(jnp.concatenate([hi, lo], axis=1))
        return jnp.concatenate(parts, axis=0), tiles

    def band_weights(hp, s2, tiles, p):
        first_sub = pad + i * n_rb + p * grp - band + 1
        w_rows = []
        carries = []
        for hd in range(2):
            for rr in range(grp):
                r = p * grp + rr
                c = jnp.zeros((sub, sub), F32)
                ws = [None] * band
                for s in reversed(range(band)):
                    index, yt, mask = tiles[p, hd, rr, s]
                    blk = s2[index * sub:(index + 1) * sub]
                    w = jnp.exp2(blk[:, :sub] + c - yt)
                    if mask is not None:
                        w = jnp.where(mask, w, 0.0)
                    ws[s] = w.astype(BF16)
                    c = c + blk[:, sub:]
                c_ref[hp, hd, r * sub:(r + 1) * sub, :] = c
                carries.append(c)
                w_rows.append(jnp.concatenate(
                    [no_weight] * rr + ws + [no_weight] * (grp - 1 - rr), axis=1))
        vwin = v_scr[hp, pl.ds(pl.multiple_of(first_sub * sub, sub), win * sub), :]
        pv = _dot(jnp.concatenate(w_rows, axis=0), vwin)
        rows = slice(p * grp_rows, (p + 1) * grp_rows)
        for hd in range(2):
            acc_ref[hp, hd, rows, :] = pv[hd * grp_rows:(hd + 1) * grp_rows]
        o_scr[hp, rows, :] = jnp.where(first_grp, pv[:grp_rows], pv[grp_rows:]).astype(BF16)
        return jnp.concatenate(
            [jnp.maximum(carries[rr], carries[grp + rr]) for rr in range(grp)], axis=0)

    def largest_open_carry(c):
        return jnp.max(jnp.where(has_keys_further_left, c, -jnp.inf))

    def further_left(hp):
        q_heads = split_heads(hp)

        def more_keys_matter(state):
            k, c_max = state
            return jnp.logical_and(k >= 0, c_max > SB_F32_EXP2_ZERO)

        def one_more_sub_block(state):
            k, _ = state
            not_yet_seen = row_blk >= k - i * n_rb + band
            kblk = kt_scr[hp, pad + k]
            vblk = v_scr[hp, pl.ds(pl.multiple_of((pad + k) * sub, sub), sub), :]
            for hd in range(2):
                pv, c = chunk(q_heads[hd], kblk, vblk, c_ref[hp, hd], not_yet_seen)
                acc_ref[hp, hd] += pv
                c_ref[hp, hd] = c
            return k - 1, largest_open_carry(c_ref[hp])

        k_start = i * n_rb + n_rb - 1 - band
        lax.while_loop(more_keys_matter, one_more_sub_block,
                       (k_start, largest_open_carry(c_ref[hp])))
        o_scr[hp] = jnp.where(first, acc_ref[hp, 0], acc_ref[hp, 1]).astype(BF16)

    def per_head_pair_duo(j, largest_carry):
        hps = [j * duo + e for e in range(duo)]
        xn_next = xn_buf[slot]
        for hp in hps:
            append_projected(hp)
        q_heads = [split_heads(hp) for hp in hps]
        stacked = [band_logits(hp, q_heads[e]) for e, hp in enumerate(hps)]
        project_q(xn_next, j, next_slot)
        project_k(xn_next, j)
        s2 = []
        for e in range(duo):
            s2.append(_dot(stacked[e][0], u))
            if e == 0:
                project_v(xn_next, j)
        for e, hp in enumerate(hps):
            carry = [band_weights(hp, s2[e], stacked[e][1], p) for p in range(pairs)]
            largest_carry = jnp.maximum(largest_carry, jnp.concatenate(carry, axis=0))
        rows = pl.ds(pl.multiple_of(j * piece, piece), piece)
        xn_buf[next_slot, rows, :] = _rms(x2_ref[rows, :], gain).astype(BF16)
        return largest_carry

    largest_carry = lax.fori_loop(0, n_hp // duo, per_head_pair_duo,
                                  jnp.full((t, sub), -jnp.inf, F32))

    @pl.when(largest_open_carry(largest_carry) > SB_F32_EXP2_ZERO)
    def _():
        def visit(hp, carry):
            further_left(hp)
            return carry

        lax.fori_loop(0, n_hp, visit, 0)

    o_ref[...] = jnp.concatenate([o_scr[hp] for hp in range(n_hp)], axis=1)


def _sb_mixer(h, gain, w_qkv):
    b, s, d = h.shape
    t = SB_TILE
    n_hp = d // LANES
    duo = SB_HP_PER_STEP
    n_tiles = s // t
    n_subs = SB_PAD_SUBS + s // SB_SUB
    w_duo = (w_qkv.reshape(d, 3, n_hp // duo, duo * LANES).transpose(2, 0, 1, 3)
             .reshape(n_hp // duo, d, 3 * duo * LANES))

    def tile_after_next(bi, i):
        tile = jnp.minimum(bi * n_tiles + i + 2, b * n_tiles - 1)
        return tile // n_tiles, tile % n_tiles, 0

    return pl.pallas_call(
        _sb_layer_kernel,
        grid=(b, n_tiles),
        in_specs=[
            pl.BlockSpec((None, t, d), lambda bi, i: (0, 0, 0), pipeline_mode=pl.Buffered(1)),
            pl.BlockSpec((None, t, d), lambda bi, i: (0, 1, 0), pipeline_mode=pl.Buffered(1)),
            pl.BlockSpec((None, t, d), tile_after_next),
            _resident((1, d)),
            _resident((n_hp // duo, d, 3 * duo * LANES)),
            _resident((2 * SB_SUB, 2 * SB_SUB)),
        ],
        out_specs=pl.BlockSpec((None, t, d), lambda bi, i: (bi, i, 0)),
        out_shape=jax.ShapeDtypeStruct((b, s, d), BF16),
        scratch_shapes=[
            pltpu.VMEM((2, n_hp, t, LANES), BF16),
            pltpu.VMEM((n_hp, n_subs, LANES, SB_SUB), BF16),
            pltpu.VMEM((n_hp, n_subs * SB_SUB, LANES), BF16),
            pltpu.VMEM((n_hp, t // SB_SUB, LANES, SB_SUB), BF16),
            pltpu.VMEM((n_hp, t, LANES), BF16),
            pltpu.VMEM((2, t, d), BF16),
            pltpu.VMEM((n_hp, t, LANES), BF16),
            pltpu.VMEM((n_hp, 2, t, LANES), F32),
            pltpu.VMEM((n_hp, 2, t, SB_SUB), F32),
        ],
        compiler_params=_params(("arbitrary", "arbitrary")),
        name="sb_layer",
    )(h, h, h, gain, w_duo, _sb_suffix_matrix())


def _conv_kernel(x_ref, xp_ref, g_ref, w_ref, cw_ref, a_ref, hbuf_ref):
    d = D_MODEL
    tm = TOKEN_TILE
    i = pl.program_id(1)
    gain = g_ref[...]
    xn = _rms(x_ref[...], gain).astype(BF16)
    xpn = _rms(xp_ref[...], gain).astype(BF16)
    hbuf_ref[SUBLANES:, :] = _dot(xn, w_ref[:, d:2 * d]) * _dot(xn, w_ref[:, 2 * d:3 * d])
    h_prev = _dot(xpn, w_ref[:, d:2 * d]) * _dot(xpn, w_ref[:, 2 * d:3 * d])
    hbuf_ref[0:SUBLANES, :] = jnp.where(i > 0, h_prev, 0.0)
    conv = cw_ref[0:1, :] * hbuf_ref[pl.ds(SUBLANES - 2, tm), :]
    conv = conv + cw_ref[1:2, :] * hbuf_ref[pl.ds(SUBLANES - 1, tm), :]
    conv = conv + cw_ref[2:3, :] * hbuf_ref[pl.ds(SUBLANES, tm), :]
    a_ref[...] = (_dot(xn, w_ref[:, 0:d]) * conv).astype(BF16)


def _conv_mixer(h, gain, w_in, conv_w):
    b, s, d = h.shape
    tm = TOKEN_TILE
    rows_per_tile = tm // SUBLANES
    return pl.pallas_call(
        _conv_kernel,
        grid=(b, s // tm),
        in_specs=[
            pl.BlockSpec((None, tm, d), lambda bi, i: (bi, i, 0)),
            pl.BlockSpec((None, SUBLANES, d),
                         lambda bi, i: (bi, jnp.maximum(i * rows_per_tile - 1, 0), 0)),
            _resident((1, d)),
            _resident((d, 3 * d)),
            _resident((CONV_WIDTH, d)),
        ],
        out_specs=pl.BlockSpec((None, tm, d), lambda bi, i: (bi, i, 0)),
        out_shape=jax.ShapeDtypeStruct((b, s, d), BF16),
        scratch_shapes=[pltpu.VMEM((tm + SUBLANES, d), F32)],
        compiler_params=_params(("parallel", "parallel")),
        name="conv_mixer",
    )(h, h, gain, w_in, conv_w)


def _gla_proj_kernel(x_ref, g_ref, w_ref, wa_ref, wgu_ref, bg_ref,
                     q_ref, kt_ref, v_ref, og_ref, lg_ref, lgt_ref):
    dk, dv = GLA_DK, GLA_DV
    xn = _rms(x_ref[...], g_ref[...]).astype(BF16)
    q_ref[...] = _dot(xn, w_ref[:, 0:dk]).astype(BF16)
    kt_ref[...] = _dot(xn, w_ref[:, dk:2 * dk]).T.astype(BF16)
    v_ref[...] = _dot(xn, w_ref[:, 2 * dk:2 * dk + dv]).astype(BF16)
    og_ref[...] = _dot(xn, w_ref[:, 2 * dk + dv:2 * dk + 2 * dv]).astype(BF16)
    a_low = _dot(xn, wa_ref[...]).astype(BF16)
    pre = _dot(a_low, wgu_ref[...]) + bg_ref[...]
    lg = _log_sigmoid(pre) * (1.0 / GLA_GATE_NORMALIZER)
    lg_ref[...] = lg
    lgt_ref[...] = lg.T


def _gla_proj(h, gain, w_main, w_a, w_gu, b_gate):
    b, s, d = h.shape
    tm = TOKEN_TILE
    dk, dv = GLA_DK, GLA_DV
    row = lambda bi, i: (bi, i, 0)
    colmajor = lambda bi, i: (bi, 0, i)
    return pl.pallas_call(
        _gla_proj_kernel,
        grid=(b, s // tm),
        in_specs=[
            pl.BlockSpec((None, tm, d), row),
            _resident((1, d)),
            _resident((d, 2 * dk + 2 * dv)),
            _resident((d, LANES)),
            _resident((LANES, dk)),
            _resident((1, dk)),
        ],
        out_specs=[
            pl.BlockSpec((None, tm, dk), row),
            pl.BlockSpec((None, dk, tm), colmajor),
            pl.BlockSpec((None, tm, dv), row),
            pl.BlockSpec((None, tm, dv), row),
            pl.BlockSpec((None, tm, dk), row),
            pl.BlockSpec((None, dk, tm), colmajor),
        ],
        out_shape=[
            jax.ShapeDtypeStruct((b, s, dk), BF16),
            jax.ShapeDtypeStruct((b, dk, s), BF16),
            jax.ShapeDtypeStruct((b, s, dv), BF16),
            jax.ShapeDtypeStruct((b, s, dv), BF16),
            jax.ShapeDtypeStruct((b, s, dk), F32),
            jax.ShapeDtypeStruct((b, dk, s), F32),
        ],
        compiler_params=_params(("parallel", "parallel")),
        name="gla_proj",
    )(h, gain, w_main, w_a, w_gu, b_gate)


def _gla_constants():
    c = GLA_TILE
    i = np.arange(c)[:, None]
    m = np.arange(c)[None, :]
    fwd, rev, masks = [], [], []
    for level in range(1, GLA_LEVELS + 1):
        n = 1 << level
        same = (i // n) == (m // n)
        fwd.append((same & (m <= i)).astype(np.float32))
        rev.append((same & (m > i)).astype(np.float32).T)
    rev.append(np.ones((c, c), np.float32))
    for level in range(GLA_LEVELS):
        n = 1 << level
        masks.append(((i // (2 * n) == m // (2 * n)) & ((i // n) % 2 == 1)
                      & ((m // n) % 2 == 0)).astype(np.float32))
    masks.append((i == m).astype(np.float32))
    fwd = np.concatenate(fwd, axis=0)
    rev = np.concatenate(rev, axis=1)
    fwd2 = np.concatenate([fwd, fwd], axis=1)
    rev2 = np.concatenate([rev, rev], axis=0)
    return (jnp.asarray(fwd2, dtype=BF16), jnp.asarray(rev2, dtype=BF16),
            jnp.asarray(np.stack(masks), dtype=F32))


def _gla_chunk_kernel(q_ref, kt_ref, v_ref, og_ref, lg_ref, lgt_ref, fwd_ref, rev_ref,
                      mask_ref, hn_ref, a_ref, state_ref):
    c = GLA_TILE
    n_ch = GLA_STEP_CHUNKS
    dk = GLA_DK
    dkh, dvh = GLA_DK_HEAD, GLA_DV_HEAD
    n_lv = GLA_LEVELS

    @pl.when(pl.program_id(1) == 0)
    def _():
        state_ref[...] = jnp.zeros_like(state_ref)

    lg = lg_ref[...]
    hi, lo = _split_bf16(lg)
    g_cat = jnp.concatenate(
        [jnp.concatenate([hi[ch * c:(ch + 1) * c], lo[ch * c:(ch + 1) * c]], axis=0)
         for ch in range(n_ch)], axis=1)
    fwd_all = _dot(fwd_ref[...], g_cat)
    hi_t, lo_t = _split_bf16(lgt_ref[...])
    gt_cat = jnp.concatenate(
        [jnp.concatenate([hi_t[:, ch * c:(ch + 1) * c], lo_t[:, ch * c:(ch + 1) * c]], axis=1)
         for ch in range(n_ch)], axis=0)
    rev_all = _dot(gt_cat, rev_ref[...])

    for hd in range(GLA_HEADS):
        ks = slice(hd * dkh, (hd + 1) * dkh)
        vs = slice(hd * dvh, (hd + 1) * dvh)
        state = state_ref[hd]
        for ch in range(n_ch):
            rows = slice(ch * c, (ch + 1) * c)
            kcols = slice(ch * dk + hd * dkh, ch * dk + (hd + 1) * dkh)
            q = q_ref[rows, ks].astype(F32) * GLA_SCALE
            kt = kt_ref[ks, rows].astype(F32)
            v = v_ref[rows, vs]
            kt_bf = kt.astype(BF16)
            scores = _dot(q.astype(BF16), kt_bf) * mask_ref[n_lv]
            qd = q * jnp.exp(lg[rows, ks])
            scores = scores + _dot(qd.astype(BF16), kt_bf) * mask_ref[0]
            for lv in range(1, n_lv):
                qd = q * jnp.exp(fwd_all[(lv - 1) * c:lv * c, kcols])
                kd = kt * jnp.exp(rev_all[kcols, (lv - 1) * c:lv * c])
                scores = scores + _dot(qd.astype(BF16), kd.astype(BF16)) * mask_ref[lv]
            q_cum = q * jnp.exp(fwd_all[(n_lv - 1) * c:n_lv * c, kcols])
            out = _dot(q_cum.astype(BF16), state.astype(BF16)) + _dot(scores.astype(BF16), v)
            k_rest = kt * jnp.exp(rev_all[kcols, (n_lv - 1) * c:n_lv * c])
            keep = jnp.exp(rev_all[kcols, n_lv * c:(n_lv + 1) * c])
            keep = jnp.concatenate([keep] * (dvh // c), axis=1)
            state = keep * state + _dot(k_rest.astype(BF16), v)
            o = _rms(out, hn_ref[:, vs])
            og = og_ref[rows, vs].astype(F32)
            a_ref[rows, vs] = (o * (og * (1.0 / (1.0 + jnp.exp(-og))))).astype(BF16)
        state_ref[hd] = state


def _gla_chunk(q, kt, v, og, lg, lgt, head_norm):
    b, s, dk = q.shape
    dv = v.shape[-1]
    c = GLA_TILE * GLA_STEP_CHUNKS
    fwd, rev, masks = _gla_constants()
    row = lambda bi, i: (bi, i, 0)
    colmajor = lambda bi, i: (bi, 0, i)
    return pl.pallas_call(
        _gla_chunk_kernel,
        grid=(b, s // c),
        in_specs=[
            pl.BlockSpec((None, c, dk), row),
            pl.BlockSpec((None, dk, c), colmajor),
            pl.BlockSpec((None, c, dv), row),
            pl.BlockSpec((None, c, dv), row),
            pl.BlockSpec((None, c, dk), row),
            pl.BlockSpec((None, dk, c), colmajor),
            _resident(fwd.shape),
            _resident(rev.shape),
            _resident(masks.shape),
            _resident((1, dv)),
        ],
        out_specs=pl.BlockSpec((None, c, dv), row),
        out_shape=jax.ShapeDtypeStruct((b, s, dv), BF16),
        scratch_shapes=[pltpu.VMEM((GLA_HEADS, GLA_DK_HEAD, GLA_DV_HEAD), F32)],
        compiler_params=_params(("parallel", "arbitrary")),
        name="gla_chunk",
    )(q, kt, v, og, lg, lgt, fwd, rev, masks, head_norm)


def _post_kernel(a_ref, h_ref, wo_ref, g_ref, wup_ref, wdn_ref, o_ref, acc_ref):
    half = TOKEN_TILE // 2
    halves = (slice(0, half), slice(half, TOKEN_TILE))
    h1 = []
    xn = []
    for rows in halves:
        mixed = _dot(a_ref[rows, :], wo_ref[...])
        h1.append(h_ref[rows, :] + _rms(mixed, g_ref[1:2, :]))
        xn.append(_rms(h1[-1], g_ref[2:3, :]).astype(BF16))
    xn_full = jnp.concatenate(xn, axis=0)
    n_chunks = D_FF // FF_CHUNK
    for ci in range(n_chunks):
        cols = slice(ci * FF_CHUNK, (ci + 1) * FF_CHUNK)
        if ci == 0:
            up = jnp.concatenate([_dot(part, wup_ref[:, cols]) for part in xn], axis=0)
        else:
            up = _dot(xn_full, wup_ref[:, cols])
        up = jnp.maximum(up, 0.0)
        act = (up * up).astype(BF16)
        if ci == 0:
            acc_ref[...] = _dot(act, wdn_ref[cols, :])
        elif ci < n_chunks - 1:
            acc_ref[...] += _dot(act, wdn_ref[cols, :])
        else:
            for hi, rows in enumerate(halves):
                ffn = acc_ref[rows, :] + _dot(act[rows, :], wdn_ref[cols, :])
                o_ref[rows, :] = h1[hi] + _rms(ffn, g_ref[3:4, :])


def _post(a, h, w_o, gains, w_up, w_down):
    b, s, d = h.shape
    tm = TOKEN_TILE
    row = lambda bi, i: (bi, i, 0)
    return pl.pallas_call(
        _post_kernel,
        grid=(b, s // tm),
        in_specs=[
            pl.BlockSpec((None, tm, d), row),
            pl.BlockSpec((None, tm, d), row),
            _resident((d, d)),
            _resident((4, d)),
            _resident((d, D_FF)),
            _resident((D_FF, d)),
        ],
        out_specs=pl.BlockSpec((None, tm, d), row),
        out_shape=jax.ShapeDtypeStruct((b, s, d), F32),
        scratch_shapes=[pltpu.VMEM((tm, d), F32)],
        compiler_params=_params(("parallel", "parallel")),
        name="post_ffn",
    )(a, h, w_o, gains, w_up, w_down)


def kernel(x, norm_gains, sb_w_qkv, sb_w_o, conv_w_in, conv_w, conv_w_out, gla_w_in,
           gla_w_gate_up, gla_b_gate, gla_head_norm, gla_w_o, ffn_w_up, ffn_w_down):
    depth = norm_gains.shape[0]
    h = x
    for i in range(depth):
        kind, j = i % 3, i // 3
        gains = norm_gains[i]
        pre_gain = gains[0:1]
        if kind == 0:
            a = _sb_mixer(h, pre_gain, sb_w_qkv[j].astype(BF16))
            w_o = sb_w_o[j]
        elif kind == 1:
            a = _conv_mixer(h, pre_gain, conv_w_in[j].astype(BF16), conv_w[j])
            w_o = conv_w_out[j]
        else:
            n_main = 2 * GLA_DK + 2 * GLA_DV
            w_in = gla_w_in[j]
            w_a = jnp.pad(w_in[:, n_main:], ((0, 0), (0, LANES - GLA_GATE_RANK))).astype(BF16)
            w_gu = jnp.pad(gla_w_gate_up[j], ((0, LANES - GLA_GATE_RANK), (0, 0))).astype(BF16)
            q, kt, v, og, lg, lgt = _gla_proj(h, pre_gain, w_in[:, :n_main].astype(BF16), w_a,
                                              w_gu, gla_b_gate[j][None, :])
            a = _gla_chunk(q, kt, v, og, lg, lgt, gla_head_norm[j].reshape(1, GLA_DV))
            w_o = gla_w_o[j]
        h = _post(a, h, w_o.astype(BF16), gains, ffn_w_up[i].astype(BF16),
                  ffn_w_down[i].astype(BF16))
    return h
```

```python
import math

import numpy as np
import jax
import jax.numpy as jnp
from jax import lax
from jax.experimental import pallas as pl
from jax.experimental.pallas import tpu as pltpu

F32 = jnp.float32
BF16 = jnp.bfloat16

D_MODEL = 1024
D_FF = 4 * D_MODEL
RMS_EPS = 1e-6

SB_HEADS = 16
SB_HEAD_DIM = D_MODEL // SB_HEADS
LOG2E = math.log2(math.e)
SB_Q_SCALE = -(SB_HEAD_DIM ** -0.5) * LOG2E
CONV_WIDTH = 3
GLA_HEADS = 4
GLA_DK = D_MODEL // 2
GLA_DV = D_MODEL
GLA_DK_HEAD = GLA_DK // GLA_HEADS
GLA_DV_HEAD = GLA_DV // GLA_HEADS
GLA_GATE_RANK = 16
GLA_GATE_NORMALIZER = 16.0
GLA_SCALE = GLA_DK_HEAD ** -0.5

LANES = 128
SUBLANES = 8
VMEM_LIMIT_BYTES = 58 * 1024 * 1024

TOKEN_TILE = 512
FF_CHUNK = 1024
SB_TILE = 512
SB_SUB = 128
SB_BAND_SUBS = 3
SB_GROUP = 2
SB_PAD_SUBS = SB_BAND_SUBS - 1
SB_HP_PER_STEP = 2
SB_F32_EXP2_ZERO = -152.0
GLA_TILE = 128
GLA_STEP_CHUNKS = 2
GLA_LEVELS = 7


def _rms(x, gain):
    ms = jnp.mean(x * x, axis=-1, keepdims=True)
    return x * lax.rsqrt(ms + RMS_EPS) * gain


def _neg_abs(z):
    bits = lax.bitcast_convert_type(z, jnp.uint32) | jnp.uint32(0x80000000)
    return lax.bitcast_convert_type(bits, F32)


def _log_sigmoid(z):
    return jnp.minimum(z, 0.0) - jnp.log(1.0 + jnp.exp(_neg_abs(z)))


def _log2_one_minus_sigmoid(y):
    return jnp.minimum(y, 0.0) - jnp.log(1.0 + jnp.exp2(_neg_abs(y))) * LOG2E


def _split_bf16(x):
    hi = x.astype(BF16)
    lo = (x - hi.astype(F32)).astype(BF16)
    return hi, lo


def _dot(a, b):
    return jnp.dot(a, b, preferred_element_type=F32)


def _resident(shape):
    zeros = (0,) * len(shape)
    return pl.BlockSpec(shape, lambda *_: zeros, pipeline_mode=pl.Buffered(1))


def _params(semantics):
    return pltpu.CompilerParams(dimension_semantics=semantics,
                                vmem_limit_bytes=VMEM_LIMIT_BYTES)


def _sb_suffix_matrix():
    t = SB_SUB
    m = np.arange(t)[:, None]
    j = np.arange(t)[None, :]
    half = np.concatenate([(m >= j).astype(np.float32), np.ones((t, t), np.float32)], axis=1)
    return jnp.asarray(np.concatenate([half, half], axis=0), dtype=BF16)


def _sb_layer_kernel(x0_ref, x1_ref, x2_ref, g_ref, w_ref, u_ref, o_ref,
                     q_scr, kt_scr, v_scr, kt_new, v_new, xn_buf, o_scr, acc_ref, c_ref):
    t = SB_TILE
    sub = SB_SUB
    n_rb = t // sub
    band = SB_BAND_SUBS
    pad = SB_PAD_SUBS
    grp = SB_GROUP
    win = band + grp - 1
    pairs = n_rb // grp
    grp_rows = grp * sub
    n_hp = D_MODEL // LANES
    duo = SB_HP_PER_STEP
    i = pl.program_id(1)
    step = pl.program_id(0) * pl.num_programs(1) + i
    gain = g_ref[...]
    u = u_ref[...]

    def project_q(xn, j, slot):
        y = _dot(xn, w_ref[j, :, 0:duo * LANES]) * SB_Q_SCALE
        for e in range(duo):
            q_scr[slot, j * duo + e] = y[:, e * LANES:(e + 1) * LANES].astype(BF16)

    def project_k(xn, j):
        y = _dot(xn, w_ref[j, :, duo * LANES:2 * duo * LANES])
        for e in range(duo):
            for kb in range(n_rb):
                kt_new[j * duo + e, kb] = (
                    y[kb * sub:(kb + 1) * sub, e * LANES:(e + 1) * LANES].T.astype(BF16))

    def project_v(xn, j):
        y = _dot(xn, w_ref[j, :, 2 * duo * LANES:3 * duo * LANES]).astype(BF16)
        for e in range(duo):
            v_new[j * duo + e] = y[:, e * LANES:(e + 1) * LANES]

    @pl.when(i == 0)
    def _():
        kt_scr[:, 0:pad] = jnp.zeros((n_hp, pad, LANES, sub), BF16)
        v_scr[:, 0:pad * sub, :] = jnp.zeros((n_hp, pad * sub, LANES), BF16)

    @pl.when(step == 0)
    def _():
        xn = _rms(x0_ref[...], gain).astype(BF16)

        def first_tile(j, carry):
            project_q(xn, j, 0)
            project_k(xn, j)
            project_v(xn, j)
            return carry

        lax.fori_loop(0, n_hp // duo, first_tile, 0)
        xn_buf[0] = _rms(x1_ref[...], gain).astype(BF16)

    slot = step % 2
    next_slot = (step + 1) % 2
    piece = t // (n_hp // duo)

    lane = lax.broadcasted_iota(jnp.int32, (t, LANES), 1)
    first = lane < SB_HEAD_DIM
    first_grp = lax.broadcasted_iota(jnp.int32, (grp_rows, LANES), 1) < SB_HEAD_DIM
    row = lax.broadcasted_iota(jnp.int32, (sub, sub), 0)
    col = lax.broadcasted_iota(jnp.int32, (sub, sub), 1)
    diag = col < row
    no_weight = jnp.zeros((sub, sub), BF16)
    row_blk = lax.broadcasted_iota(jnp.int32, (t, sub), 0) // sub
    not_first_tile = i > 0

    def append_projected(hp):
        for kb in range(n_rb):
            kt_scr[hp, pad + i * n_rb + kb] = kt_new[hp, kb]
        v_scr[hp, pl.ds(pl.multiple_of((pad + i * n_rb) * sub, sub), t), :] = v_new[hp]

    def split_heads(hp):
        q = q_scr[slot, hp]
        zero = jnp.zeros_like(q)
        return jnp.where(first, q, zero), jnp.where(first, zero, q)

    def band_logits(hp, q_heads):
        parts = []
        tiles = {}
        for p in range(pairs):
            first_sub = pad + i * n_rb + p * grp - band + 1
            kwin = jnp.concatenate([kt_scr[hp, first_sub + s] for s in range(win)], axis=1)
            q_both = jnp.concatenate(
                [q_heads[hd][p * grp_rows:(p + 1) * grp_rows] for hd in range(2)], axis=0)
            y = _dot(q_both, kwin)
            for hd in range(2):
                for rr in range(grp):
                    top = hd * grp_rows + rr * sub
                    for s in range(band):
                        yt = y[top:top + sub, (rr + s) * sub:(rr + s + 1) * sub]
                        part = _log2_one_minus_sigmoid(yt)
                        mask = diag if s == band - 1 else None
                        if p * grp + rr + s < band - 1:
                            mask = not_first_tile if mask is None else mask & not_first_tile
                        if mask is not None:
                            part = jnp.where(mask, part, 0.0)
                        hi, lo = _split_bf16(part)
                        tiles[p, hd, rr, s] = (len(parts), yt, mask)
                        parts.append(jnp.concatenate([hi, lo], axis=1))
        return jnp.concatenate(parts, axis=0), tiles

    def band_weights(hp, s2, tiles, p):
        first_sub = pad + i * n_rb + p * grp - band + 1
        w_rows = []
        carries = []
        for hd in range(2):
            for rr in range(grp):
                r = p * grp + rr
                c = jnp.zeros((sub, sub), F32)
                ws = [None] * band
                for s in reversed(range(band)):
                    index, yt, mask = tiles[p, hd, rr, s]
                    blk = s2[index * sub:(index + 1) * sub]
                    w = jnp.exp2(blk[:, :sub] + c - yt)
                    if mask is not None:
                        w = jnp.where(mask, w, 0.0)
                    ws[s] = w.astype(BF16)
                    c = c + blk[:, sub:]
                c_ref[hp, hd, r * sub:(r + 1) * sub, :] = c
                carries.append(c)
                w_rows.append(jnp.concatenate(
                    [no_weight] * rr + ws + [no_weight] * (grp - 1 - rr), axis=1))
        vwin = v_scr[hp, pl.ds(pl.multiple_of(first_sub * sub, sub), win * sub), :]
        pv = _dot(jnp.concatenate(w_rows, axis=0), vwin)
        rows = slice(p * grp_rows, (p + 1) * grp_rows)
        for hd in range(2):
            acc_ref[hp, hd, rows, :] = pv[hd * grp_rows:(hd + 1) * grp_rows]
        o_scr[hp, rows, :] = jnp.where(first_grp, pv[:grp_rows], pv[grp_rows:]).astype(BF16)
        return jnp.concatenate(
            [jnp.maximum(carries[rr], carries[grp + rr]) for rr in range(grp)], axis=0)

    def largest_open_carry(c, m):
        has_keys = row_blk + i * n_rb - band + 1 - m >= 0
        return jnp.max(jnp.where(has_keys, c, -jnp.inf))

    def extend_band(hp, q_heads, m):
        parts = []
        tiles = {}
        starts = []
        for p in range(pairs):
            first_sub = i * n_rb + p * grp - band + 1 - m
            start = jnp.maximum(pad + first_sub, 0)
            starts.append(start)
            kwin = jnp.concatenate([kt_scr[hp, start + s] for s in range(grp)], axis=1)
            q_both = jnp.concatenate(
                [q_heads[hd][p * grp_rows:(p + 1) * grp_rows] for hd in range(2)], axis=0)
            y = _dot(q_both, kwin)
            for hd in range(2):
                for rr in range(grp):
                    top = hd * grp_rows + rr * sub
                    yt = y[top:top + sub, rr * sub:(rr + 1) * sub]
                    in_sequence = first_sub + rr >= 0
                    part = jnp.where(in_sequence, _log2_one_minus_sigmoid(yt), 0.0)
                    hi, lo = _split_bf16(part)
                    tiles[p, hd, rr] = (len(parts), yt, in_sequence)
                    parts.append(jnp.concatenate([hi, lo], axis=1))
        s2 = _dot(jnp.concatenate(parts, axis=0), u)
        for p in range(pairs):
            w_rows = []
            for hd in range(2):
                for rr in range(grp):
                    rows = slice((p * grp + rr) * sub, (p * grp + rr + 1) * sub)
                    index, yt, in_sequence = tiles[p, hd, rr]
                    blk = s2[index * sub:(index + 1) * sub]
                    c = c_ref[hp, hd, rows, :]
                    w = jnp.where(in_sequence, jnp.exp2(blk[:, :sub] + c - yt), 0.0).astype(BF16)
                    c_ref[hp, hd, rows, :] = c + blk[:, sub:]
                    w_rows.append(jnp.concatenate(
                        [no_weight] * rr + [w] + [no_weight] * (grp - 1 - rr), axis=1))
            vwin = v_scr[hp, pl.ds(pl.multiple_of(starts[p] * sub, sub), grp * sub), :]
            pv = _dot(jnp.concatenate(w_rows, axis=0), vwin)
            rows = slice(p * grp_rows, (p + 1) * grp_rows)
            for hd in range(2):
                acc_ref[hp, hd, rows, :] += pv[hd * grp_rows:(hd + 1) * grp_rows]

    def further_left(hp):
        q_heads = split_heads(hp)
        m_last = i * n_rb + n_rb - band

        def more_keys_matter(state):
            m, c_max = state
            return jnp.logical_and(m <= m_last, c_max > SB_F32_EXP2_ZERO)

        def one_more_sub_block(state):
            m, _ = state
            extend_band(hp, q_heads, m)
            return m + 1, largest_open_carry(c_ref[hp], m + 1)

        lax.while_loop(more_keys_matter, one_more_sub_block,
                       (1, largest_open_carry(c_ref[hp], 1)))
        o_scr[hp] = jnp.where(first, acc_ref[hp, 0], acc_ref[hp, 1]).astype(BF16)

    def per_head_pair_duo(j, largest_carry):
        hps = [j * duo + e for e in range(duo)]
        xn_next = xn_buf[slot]
        for hp in hps:
            append_projected(hp)
        q_heads = [split_heads(hp) for hp in hps]
        stacked = [band_logits(hp, q_heads[e]) for e, hp in enumerate(hps)]
        project_q(xn_next, j, next_slot)
        project_k(xn_next, j)
        s2 = []
        for e in range(duo):
            s2.append(_dot(stacked[e][0], u))
            if e == 0:
                project_v(xn_next, j)
        for e, hp in enumerate(hps):
            carry = [band_weights(hp, s2[e], stacked[e][1], p) for p in range(pairs)]
            largest_carry = jnp.maximum(largest_carry, jnp.concatenate(carry, axis=0))
        rows = pl.ds(pl.multiple_of(j * piece, piece), piece)
        xn_buf[next_slot, rows, :] = _rms(x2_ref[rows, :], gain).astype(BF16)
        return largest_carry

    largest_carry = lax.fori_loop(0, n_hp // duo, per_head_pair_duo,
                                  jnp.full((t, sub), -jnp.inf, F32))

    @pl.when(largest_open_carry(largest_carry, 1) > SB_F32_EXP2_ZERO)
    def _():
        def visit(hp, carry):
            further_left(hp)
            return carry

        lax.fori_loop(0, n_hp, visit, 0)

    o_ref[...] = jnp.concatenate([o_scr[hp] for hp in range(n_hp)], axis=1)


def _sb_mixer(h, gain, w_qkv):
    b, s, d = h.shape
    t = SB_TILE
    n_hp = d // LANES
    duo = SB_HP_PER_STEP
    n_tiles = s // t
    n_subs = SB_PAD_SUBS + s // SB_SUB
    w_duo = (w_qkv.reshape(d, 3, n_hp // duo, duo * LANES).transpose(2, 0, 1, 3)
             .reshape(n_hp // duo, d, 3 * duo * LANES))

    def tile_after_next(bi, i):
        tile = jnp.minimum(bi * n_tiles + i + 2, b * n_tiles - 1)
        return tile // n_tiles, tile % n_tiles, 0

    return pl.pallas_call(
        _sb_layer_kernel,
        grid=(b, n_tiles),
        in_specs=[
            pl.BlockSpec((None, t, d), lambda bi, i: (0, 0, 0), pipeline_mode=pl.Buffered(1)),
            pl.BlockSpec((None, t, d), lambda bi, i: (0, 1, 0), pipeline_mode=pl.Buffered(1)),
            pl.BlockSpec((None, t, d), tile_after_next),
            _resident((1, d)),
            _resident((n_hp // duo, d, 3 * duo * LANES)),
            _resident((2 * SB_SUB, 2 * SB_SUB)),
        ],
        out_specs=pl.BlockSpec((None, t, d), lambda bi, i: (bi, i, 0)),
        out_shape=jax.ShapeDtypeStruct((b, s, d), BF16),
        scratch_shapes=[
            pltpu.VMEM((2, n_hp, t, LANES), BF16),
            pltpu.VMEM((n_hp, n_subs, LANES, SB_SUB), BF16),
            pltpu.VMEM((n_hp, n_subs * SB_SUB, LANES), BF16),
            pltpu.VMEM((n_hp, t // SB_SUB, LANES, SB_SUB), BF16),
            pltpu.VMEM((n_hp, t, LANES), BF16),
            pltpu.VMEM((2, t, d), BF16),
            pltpu.VMEM((n_hp, t, LANES), BF16),
            pltpu.VMEM((n_hp, 2, t, LANES), F32),
            pltpu.VMEM((n_hp, 2, t, SB_SUB), F32),
        ],
        compiler_params=_params(("arbitrary", "arbitrary")),
        name="sb_layer",
    )(h, h, h, gain, w_duo, _sb_suffix_matrix())


def _conv_kernel(x_ref, xp_ref, g_ref, w_ref, cw_ref, a_ref, hbuf_ref):
    d = D_MODEL
    tm = TOKEN_TILE
    i = pl.program_id(1)
    gain = g_ref[...]
    xn = _rms(x_ref[...], gain).astype(BF16)
    xpn = _rms(xp_ref[...], gain).astype(BF16)
    hbuf_ref[SUBLANES:, :] = _dot(xn, w_ref[:, d:2 * d]) * _dot(xn, w_ref[:, 2 * d:3 * d])
    h_prev = _dot(xpn, w_ref[:, d:2 * d]) * _dot(xpn, w_ref[:, 2 * d:3 * d])
    hbuf_ref[0:SUBLANES, :] = jnp.where(i > 0, h_prev, 0.0)
    conv = cw_ref[0:1, :] * hbuf_ref[pl.ds(SUBLANES - 2, tm), :]
    conv = conv + cw_ref[1:2, :] * hbuf_ref[pl.ds(SUBLANES - 1, tm), :]
    conv = conv + cw_ref[2:3, :] * hbuf_ref[pl.ds(SUBLANES, tm), :]
    a_ref[...] = (_dot(xn, w_ref[:, 0:d]) * conv).astype(BF16)


def _conv_mixer(h, gain, w_in, conv_w):
    b, s, d = h.shape
    tm = TOKEN_TILE
    rows_per_tile = tm // SUBLANES
    return pl.pallas_call(
        _conv_kernel,
        grid=(b, s // tm),
        in_specs=[
            pl.BlockSpec((None, tm, d), lambda bi, i: (bi, i, 0)),
            pl.BlockSpec((None, SUBLANES, d),
                         lambda bi, i: (bi, jnp.maximum(i * rows_per_tile - 1, 0), 0)),
            _resident((1, d)),
            _resident((d, 3 * d)),
            _resident((CONV_WIDTH, d)),
        ],
        out_specs=pl.BlockSpec((None, tm, d), lambda bi, i: (bi, i, 0)),
        out_shape=jax.ShapeDtypeStruct((b, s, d), BF16),
        scratch_shapes=[pltpu.VMEM((tm + SUBLANES, d), F32)],
        compiler_params=_params(("parallel", "parallel")),
        name="conv_mixer",
    )(h, h, gain, w_in, conv_w)


def _gla_proj_kernel(x_ref, g_ref, w_ref, wa_ref, wgu_ref, bg_ref,
                     q_ref, kt_ref, v_ref, og_ref, lg_ref, lgt_ref):
    dk, dv = GLA_DK, GLA_DV
    xn = _rms(x_ref[...], g_ref[...]).astype(BF16)
    q_ref[...] = _dot(xn, w_ref[:, 0:dk]).astype(BF16)
    kt_ref[...] = _dot(xn, w_ref[:, dk:2 * dk]).T.astype(BF16)
    v_ref[...] = _dot(xn, w_ref[:, 2 * dk:2 * dk + dv]).astype(BF16)
    og_ref[...] = _dot(xn, w_ref[:, 2 * dk + dv:2 * dk + 2 * dv]).astype(BF16)
    a_low = _dot(xn, wa_ref[...]).astype(BF16)
    pre = _dot(a_low, wgu_ref[...]) + bg_ref[...]
    lg = _log_sigmoid(pre) * (1.0 / GLA_GATE_NORMALIZER)
    lg_ref[...] = lg
    lgt_ref[...] = lg.T


def _gla_proj(h, gain, w_main, w_a, w_gu, b_gate):
    b, s, d = h.shape
    tm = TOKEN_TILE
    dk, dv = GLA_DK, GLA_DV
    row = lambda bi, i: (bi, i, 0)
    colmajor = lambda bi, i: (bi, 0, i)
    return pl.pallas_call(
        _gla_proj_kernel,
        grid=(b, s // tm),
        in_specs=[
            pl.BlockSpec((None, tm, d), row),
            _resident((1, d)),
            _resident((d, 2 * dk + 2 * dv)),
            _resident((d, LANES)),
            _resident((LANES, dk)),
            _resident((1, dk)),
        ],
        out_specs=[
            pl.BlockSpec((None, tm, dk), row),
            pl.BlockSpec((None, dk, tm), colmajor),
            pl.BlockSpec((None, tm, dv), row),
            pl.BlockSpec((None, tm, dv), row),
            pl.BlockSpec((None, tm, dk), row),
            pl.BlockSpec((None, dk, tm), colmajor),
        ],
        out_shape=[
            jax.ShapeDtypeStruct((b, s, dk), BF16),
            jax.ShapeDtypeStruct((b, dk, s), BF16),
            jax.ShapeDtypeStruct((b, s, dv), BF16),
            jax.ShapeDtypeStruct((b, s, dv), BF16),
            jax.ShapeDtypeStruct((b, s, dk), F32),
            jax.ShapeDtypeStruct((b, dk, s), F32),
        ],
        compiler_params=_params(("parallel", "parallel")),
        name="gla_proj",
    )(h, gain, w_main, w_a, w_gu, b_gate)


def _gla_constants():
    c = GLA_TILE
    i = np.arange(c)[:, None]
    m = np.arange(c)[None, :]
    fwd, rev, masks = [], [], []
    for level in range(1, GLA_LEVELS + 1):
        n = 1 << level
        same = (i // n) == (m // n)
        fwd.append((same & (m <= i)).astype(np.float32))
        rev.append((same & (m > i)).astype(np.float32).T)
    rev.append(np.ones((c, c), np.float32))
    for level in range(GLA_LEVELS):
        n = 1 << level
        masks.append(((i // (2 * n) == m // (2 * n)) & ((i // n) % 2 == 1)
                      & ((m // n) % 2 == 0)).astype(np.float32))
    masks.append((i == m).astype(np.float32))
    fwd = np.concatenate(fwd, axis=0)
    rev = np.concatenate(rev, axis=1)
    fwd2 = np.concatenate([fwd, fwd], axis=1)
    rev2 = np.concatenate([rev, rev], axis=0)
    return (jnp.asarray(fwd2, dtype=BF16), jnp.asarray(rev2, dtype=BF16),
            jnp.asarray(np.stack(masks), dtype=F32))


def _gla_chunk_kernel(q_ref, kt_ref, v_ref, og_ref, lg_ref, lgt_ref, fwd_ref, rev_ref,
                      mask_ref, hn_ref, a_ref, state_ref):
    c = GLA_TILE
    n_ch = GLA_STEP_CHUNKS
    dk = GLA_DK
    dkh, dvh = GLA_DK_HEAD, GLA_DV_HEAD
    n_lv = GLA_LEVELS

    @pl.when(pl.program_id(1) == 0)
    def _():
        state_ref[...] = jnp.zeros_like(state_ref)

    lg = lg_ref[...]
    hi, lo = _split_bf16(lg)
    g_cat = jnp.concatenate(
        [jnp.concatenate([hi[ch * c:(ch + 1) * c], lo[ch * c:(ch + 1) * c]], axis=0)
         for ch in range(n_ch)], axis=1)
    fwd_all = _dot(fwd_ref[...], g_cat)
    hi_t, lo_t = _split_bf16(lgt_ref[...])
    gt_cat = jnp.concatenate(
        [jnp.concatenate([hi_t[:, ch * c:(ch + 1) * c], lo_t[:, ch * c:(ch + 1) * c]], axis=1)
         for ch in range(n_ch)], axis=0)
    rev_all = _dot(gt_cat, rev_ref[...])

    for hd in range(GLA_HEADS):
        ks = slice(hd * dkh, (hd + 1) * dkh)
        vs = slice(hd * dvh, (hd + 1) * dvh)
        state = state_ref[hd]
        for ch in range(n_ch):
            rows = slice(ch * c, (ch + 1) * c)
            kcols = slice(ch * dk + hd * dkh, ch * dk + (hd + 1) * dkh)
            q = q_ref[rows, ks].astype(F32) * GLA_SCALE
            kt = kt_ref[ks, rows].astype(F32)
            v = v_ref[rows, vs]
            kt_bf = kt.astype(BF16)
            scores = _dot(q.astype(BF16), kt_bf) * mask_ref[n_lv]
            qd = q * jnp.exp(lg[rows, ks])
            scores = scores + _dot(qd.astype(BF16), kt_bf) * mask_ref[0]
            for lv in range(1, n_lv):
                qd = q * jnp.exp(fwd_all[(lv - 1) * c:lv * c, kcols])
                kd = kt * jnp.exp(rev_all[kcols, (lv - 1) * c:lv * c])
                scores = scores + _dot(qd.astype(BF16), kd.astype(BF16)) * mask_ref[lv]
            q_cum = q * jnp.exp(fwd_all[(n_lv - 1) * c:n_lv * c, kcols])
            out = _dot(jnp.concatenate([q_cum.astype(BF16), scores.astype(BF16)], axis=1),
                       jnp.concatenate([state.astype(BF16), v], axis=0))
            k_rest = kt * jnp.exp(rev_all[kcols, (n_lv - 1) * c:n_lv * c])
            keep = jnp.exp(rev_all[kcols, n_lv * c:(n_lv + 1) * c])
            keep = jnp.concatenate([keep] * (dvh // c), axis=1)
            state = keep * state + _dot(k_rest.astype(BF16), v)
            o = _rms(out, hn_ref[:, vs])
            og = og_ref[rows, vs].astype(F32)
            a_ref[rows, vs] = (o * (og * (1.0 / (1.0 + jnp.exp(-og))))).astype(BF16)
        state_ref[hd] = state


def _gla_chunk(q, kt, v, og, lg, lgt, head_norm):
    b, s, dk = q.shape
    dv = v.shape[-1]
    c = GLA_TILE * GLA_STEP_CHUNKS
    fwd, rev, masks = _gla_constants()
    row = lambda bi, i: (bi, i, 0)
    colmajor = lambda bi, i: (bi, 0, i)
    return pl.pallas_call(
        _gla_chunk_kernel,
        grid=(b, s // c),
        in_specs=[
            pl.BlockSpec((None, c, dk), row),
            pl.BlockSpec((None, dk, c), colmajor),
            pl.BlockSpec((None, c, dv), row),
            pl.BlockSpec((None, c, dv), row),
            pl.BlockSpec((None, c, dk), row),
            pl.BlockSpec((None, dk, c), colmajor),
            _resident(fwd.shape),
            _resident(rev.shape),
            _resident(masks.shape),
            _resident((1, dv)),
        ],
        out_specs=pl.BlockSpec((None, c, dv), row),
        out_shape=jax.ShapeDtypeStruct((b, s, dv), BF16),
        scratch_shapes=[pltpu.VMEM((GLA_HEADS, GLA_DK_HEAD, GLA_DV_HEAD), F32)],
        compiler_params=_params(("parallel", "arbitrary")),
        name="gla_chunk",
    )(q, kt, v, og, lg, lgt, fwd, rev, masks, head_norm)


def _post_kernel(a_ref, h_ref, wo_ref, g_ref, wup_ref, wdn_ref, o_ref, acc_ref):
    half = TOKEN_TILE // 2
    halves = (slice(0, half), slice(half, TOKEN_TILE))
    h1 = []
    xn = []
    for rows in halves:
        mixed = _dot(a_ref[rows, :], wo_ref[...])
        h1.append(h_ref[rows, :] + _rms(mixed, g_ref[1:2, :]))
        xn.append(_rms(h1[-1], g_ref[2:3, :]).astype(BF16))
    xn_full = jnp.concatenate(xn, axis=0)
    n_chunks = D_FF // FF_CHUNK
    for ci in range(n_chunks):
        cols = slice(ci * FF_CHUNK, (ci + 1) * FF_CHUNK)
        if ci == 0:
            up = jnp.concatenate([_dot(part, wup_ref[:, cols]) for part in xn], axis=0)
        else:
            up = _dot(xn_full, wup_ref[:, cols])
        up = jnp.maximum(up, 0.0)
        act = (up * up).astype(BF16)
        if ci == 0:
            acc_ref[...] = _dot(act, wdn_ref[cols, :])
        elif ci < n_chunks - 1:
            acc_ref[...] += _dot(act, wdn_ref[cols, :])
        else:
            for hi, rows in enumerate(halves):
                ffn = acc_ref[rows, :] + _dot(act[rows, :], wdn_ref[cols, :])
                o_ref[rows, :] = h1[hi] + _rms(ffn, g_ref[3:4, :])


def _post(a, h, w_o, gains, w_up, w_down):
    b, s, d = h.shape
    tm = TOKEN_TILE
    row = lambda bi, i: (bi, i, 0)
    return pl.pallas_call(
        _post_kernel,
        grid=(b, s // tm),
        in_specs=[
            pl.BlockSpec((None, tm, d), row),
            pl.BlockSpec((None, tm, d), row),
            _resident((d, d)),
            _resident((4, d)),
            _resident((d, D_FF)),
            _resident((D_FF, d)),
        ],
        out_specs=pl.BlockSpec((None, tm, d), row),
        out_shape=jax.ShapeDtypeStruct((b, s, d), F32),
        scratch_shapes=[pltpu.VMEM((tm, d), F32)],
        compiler_params=_params(("parallel", "parallel")),
        name="post_ffn",
    )(a, h, w_o, gains, w_up, w_down)


def kernel(x, norm_gains, sb_w_qkv, sb_w_o, conv_w_in, conv_w, conv_w_out, gla_w_in,
           gla_w_gate_up, gla_b_gate, gla_head_norm, gla_w_o, ffn_w_up, ffn_w_down):
    depth = norm_gains.shape[0]
    h = x
    for i in range(depth):
        kind, j = i % 3, i // 3
        gains = norm_gains[i]
        pre_gain = gains[0:1]
        if kind == 0:
            a = _sb_mixer(h, pre_gain, sb_w_qkv[j].astype(BF16))
            w_o = sb_w_o[j]
        elif kind == 1:
            a = _conv_mixer(h, pre_gain, conv_w_in[j].astype(BF16), conv_w[j])
            w_o = conv_w_out[j]
        else:
            n_main = 2 * GLA_DK + 2 * GLA_DV
            w_in = gla_w_in[j]
            w_a = jnp.pad(w_in[:, n_main:], ((0, 0), (0, LANES - GLA_GATE_RANK))).astype(BF16)
            w_gu = jnp.pad(gla_w_gate_up[j], ((0, LANES - GLA_GATE_RANK), (0, 0))).astype(BF16)
            q, kt, v, og, lg, lgt = _gla_proj(h, pre_gain, w_in[:, :n_main].astype(BF16), w_a,
                                              w_gu, gla_b_gate[j][None, :])
            a = _gla_chunk(q, kt, v, og, lg, lgt, gla_head_norm[j].reshape(1, GLA_DV))
            w_o = gla_w_o[j]
        h = _post(a, h, w_o.astype(BF16), gains, ffn_w_up[i].astype(BF16),
                  ffn_w_down[i].astype(BF16))
    return h
```

```python
import math

import numpy as np
import jax
import jax.numpy as jnp
from jax import lax
from jax.experimental import pallas as pl
from jax.experimental.pallas import tpu as pltpu

F32 = jnp.float32
BF16 = jnp.bfloat16

D_MODEL = 1024
D_FF = 4 * D_MODEL
RMS_EPS = 1e-6

SB_HEADS = 16
SB_HEAD_DIM = D_MODEL // SB_HEADS
LOG2E = math.log2(math.e)
SB_Q_SCALE = -(SB_HEAD_DIM ** -0.5) * LOG2E
CONV_WIDTH = 3
GLA_HEADS = 4
GLA_DK = D_MODEL // 2
GLA_DV = D_MODEL
GLA_DK_HEAD = GLA_DK // GLA_HEADS
GLA_DV_HEAD = GLA_DV // GLA_HEADS
GLA_GATE_RANK = 16
GLA_GATE_NORMALIZER = 16.0
GLA_SCALE = GLA_DK_HEAD ** -0.5

LANES = 128
SUBLANES = 8
VMEM_LIMIT_BYTES = 58 * 1024 * 1024

TOKEN_TILE = 512
FF_CHUNK = 1024
SB_TILE = 512
SB_SUB = 128
SB_BAND_SUBS = 3
SB_GROUP = 2
SB_PAD_SUBS = SB_BAND_SUBS - 1
SB_HP_PER_STEP = 2
SB_F32_EXP2_ZERO = -152.0
GLA_TILE = 128
GLA_STEP_CHUNKS = 2
GLA_LEVELS = 7


def _rms(x, gain):
    ms = jnp.mean(x * x, axis=-1, keepdims=True)
    return x * lax.rsqrt(ms + RMS_EPS) * gain


def _neg_abs(z):
    bits = lax.bitcast_convert_type(z, jnp.uint32) | jnp.uint32(0x80000000)
    return lax.bitcast_convert_type(bits, F32)


def _log_sigmoid(z):
    return jnp.minimum(z, 0.0) - jnp.log(1.0 + jnp.exp(_neg_abs(z)))


def _log2_one_minus_sigmoid(y):
    return jnp.minimum(y, 0.0) - jnp.log(1.0 + jnp.exp2(_neg_abs(y))) * LOG2E


def _split_bf16(x):
    hi = x.astype(BF16)
    lo = (x - hi.astype(F32)).astype(BF16)
    return hi, lo


def _dot(a, b):
    return jnp.dot(a, b, preferred_element_type=F32)


def _resident(shape):
    zeros = (0,) * len(shape)
    return pl.BlockSpec(shape, lambda *_: zeros, pipeline_mode=pl.Buffered(1))


def _params(semantics):
    return pltpu.CompilerParams(dimension_semantics=semantics,
                                vmem_limit_bytes=VMEM_LIMIT_BYTES)


def _sb_suffix_matrix():
    t = SB_SUB
    m = np.arange(t)[:, None]
    j = np.arange(t)[None, :]
    half = np.concatenate([(m >= j).astype(np.float32), np.ones((t, t), np.float32)], axis=1)
    return jnp.asarray(np.concatenate([half, half], axis=0), dtype=BF16)


def _sb_layer_kernel(x0_ref, x1_ref, x2_ref, g_ref, w_ref, u_ref, o_ref,
                     q_scr, kt_scr, v_scr, kt_new, v_new, xn_buf, o_scr, acc_ref, c_ref):
    t = SB_TILE
    sub = SB_SUB
    n_rb = t // sub
    band = SB_BAND_SUBS
    pad = SB_PAD_SUBS
    grp = SB_GROUP
    win = band + grp - 1
    pairs = n_rb // grp
    grp_rows = grp * sub
    n_hp = D_MODEL // LANES
    duo = SB_HP_PER_STEP
    i = pl.program_id(1)
    step = pl.program_id(0) * pl.num_programs(1) + i
    gain = g_ref[...]
    u = u_ref[...]

    def project_q(xn, j, slot):
        y = _dot(xn, w_ref[j, :, 0:duo * LANES]) * SB_Q_SCALE
        for e in range(duo):
            q_scr[slot, j * duo + e] = y[:, e * LANES:(e + 1) * LANES].astype(BF16)

    def project_k(xn, j):
        y = _dot(xn, w_ref[j, :, duo * LANES:2 * duo * LANES])
        for e in range(duo):
            for kb in range(n_rb):
                kt_new[j * duo + e, kb] = (
                    y[kb * sub:(kb + 1) * sub, e * LANES:(e + 1) * LANES].T.astype(BF16))

    def project_v(xn, j):
        y = _dot(xn, w_ref[j, :, 2 * duo * LANES:3 * duo * LANES]).astype(BF16)
        for e in range(duo):
            v_new[j * duo + e] = y[:, e * LANES:(e + 1) * LANES]

    @pl.when(i == 0)
    def _():
        kt_scr[:, 0:pad] = jnp.zeros((n_hp, pad, LANES, sub), BF16)
        v_scr[:, 0:pad * sub, :] = jnp.zeros((n_hp, pad * sub, LANES), BF16)

    @pl.when(step == 0)
    def _():
        xn = _rms(x0_ref[...], gain).astype(BF16)

        def first_tile(j, carry):
            project_q(xn, j, 0)
            project_k(xn, j)
            project_v(xn, j)
            return carry

        lax.fori_loop(0, n_hp // duo, first_tile, 0)
        xn_buf[0] = _rms(x1_ref[...], gain).astype(BF16)

    slot = step % 2
    next_slot = (step + 1) % 2
    piece = t // (n_hp // duo)

    lane = lax.broadcasted_iota(jnp.int32, (t, LANES), 1)
    first = lane < SB_HEAD_DIM
    first_grp = lax.broadcasted_iota(jnp.int32, (grp_rows, LANES), 1) < SB_HEAD_DIM
    row = lax.broadcasted_iota(jnp.int32, (sub, sub), 0)
    col = lax.broadcasted_iota(jnp.int32, (sub, sub), 1)
    diag = col < row
    no_weight = jnp.zeros((sub, sub), BF16)
    row_blk = lax.broadcasted_iota(jnp.int32, (t, sub), 0) // sub
    not_first_tile = i > 0

    def append_projected(hp):
        for kb in range(n_rb):
            kt_scr[hp, pad + i * n_rb + kb] = kt_new[hp, kb]
        v_scr[hp, pl.ds(pl.multiple_of((pad + i * n_rb) * sub, sub), t), :] = v_new[hp]

    def split_heads(hp):
        q = q_scr[slot, hp]
        zero = jnp.zeros_like(q)
        return jnp.where(first, q, zero), jnp.where(first, zero, q)

    def band_logits(hp, q_heads):
        parts = []
        tiles = {}
        for p in range(pairs):
            first_sub = pad + i * n_rb + p * grp - band + 1
            kwin = jnp.concatenate([kt_scr[hp, first_sub + s] for s in range(win)], axis=1)
            q_both = jnp.concatenate(
                [q_heads[hd][p * grp_rows:(p + 1) * grp_rows] for hd in range(2)], axis=0)
            y = _dot(q_both, kwin)
            for hd in range(2):
                for rr in range(grp):
                    top = hd * grp_rows + rr * sub
                    for s in range(band):
                        yt = y[top:top + sub, (rr + s) * sub:(rr + s + 1) * sub]
                        part = _log2_one_minus_sigmoid(yt)
                        mask = diag if s == band - 1 else None
                        if p * grp + rr + s < band - 1:
                            mask = not_first_tile if mask is None else mask & not_first_tile
                        if mask is not None:
                            part = jnp.where(mask, part, 0.0)
                        hi, lo = _split_bf16(part)
                        tiles[p, hd, rr, s] = (len(parts), yt, mask)
                        parts.append(jnp.concatenate([hi, lo], axis=1))
        return jnp.concatenate(parts, axis=0), tiles

    def band_weights(hp, s2, tiles, p):
        first_sub = pad + i * n_rb + p * grp - band + 1
        w_rows = []
        carries = []
        for hd in range(2):
            for rr in range(grp):
                r = p * grp + rr
                c = jnp.zeros((sub, sub), F32)
                ws = [None] * band
                for s in reversed(range(band)):
                    index, yt, mask = tiles[p, hd, rr, s]
                    blk = s2[index * sub:(index + 1) * sub]
                    w = jnp.exp2(blk[:, :sub] + c - yt)
                    if mask is not None:
                        w = jnp.where(mask, w, 0.0)
                    ws[s] = w.astype(BF16)
                    c = c + blk[:, sub:]
                c_ref[hp, hd, r * sub:(r + 1) * sub, :] = c
                carries.append(c)
                w_rows.append(jnp.concatenate(
                    [no_weight] * rr + ws + [no_weight] * (grp - 1 - rr), axis=1))
        vwin = v_scr[hp, pl.ds(pl.multiple_of(first_sub * sub, sub), win * sub), :]
        pv = _dot(jnp.concatenate(w_rows, axis=0), vwin)
        rows = slice(p * grp_rows, (p + 1) * grp_rows)
        for hd in range(2):
            acc_ref[hp, hd, rows, :] = pv[hd * grp_rows:(hd + 1) * grp_rows]
        o_scr[hp, rows, :] = jnp.where(first_grp, pv[:grp_rows], pv[grp_rows:]).astype(BF16)
        return jnp.concatenate(
            [jnp.maximum(carries[rr], carries[grp + rr]) for rr in range(grp)], axis=0)

    def largest_open_carry(c, m):
        has_keys = row_blk + i * n_rb - band + 1 - m >= 0
        return jnp.max(jnp.where(has_keys, c, -jnp.inf))

    def extend_band(hp, q_heads, m):
        parts = []
        tiles = {}
        starts = []
        for p in range(pairs):
            first_sub = i * n_rb + p * grp - band + 1 - m
            start = jnp.maximum(pad + first_sub, 0)
            starts.append(start)
            kwin = jnp.concatenate([kt_scr[hp, start + s] for s in range(grp)], axis=1)
            q_both = jnp.concatenate(
                [q_heads[hd][p * grp_rows:(p + 1) * grp_rows] for hd in range(2)], axis=0)
            y = _dot(q_both, kwin)
            for hd in range(2):
                for rr in range(grp):
                    top = hd * grp_rows + rr * sub
                    yt = y[top:top + sub, rr * sub:(rr + 1) * sub]
                    in_sequence = first_sub + rr >= 0
                    part = jnp.where(in_sequence, _log2_one_minus_sigmoid(yt), 0.0)
                    hi, lo = _split_bf16(part)
                    tiles[p, hd, rr] = (len(parts), yt, in_sequence)
                    parts.append(jnp.concatenate([hi, lo], axis=1))
        s2 = _dot(jnp.concatenate(parts, axis=0), u)
        for p in range(pairs):
            w_rows = []
            for hd in range(2):
                for rr in range(grp):
                    rows = slice((p * grp + rr) * sub, (p * grp + rr + 1) * sub)
                    index, yt, in_sequence = tiles[p, hd, rr]
                    blk = s2[index * sub:(index + 1) * sub]
                    c = c_ref[hp, hd, rows, :]
                    w = jnp.where(in_sequence, jnp.exp2(blk[:, :sub] + c - yt), 0.0).astype(BF16)
                    c_ref[hp, hd, rows, :] = c + blk[:, sub:]
                    w_rows.append(jnp.concatenate(
                        [no_weight] * rr + [w] + [no_weight] * (grp - 1 - rr), axis=1))
            vwin = v_scr[hp, pl.ds(pl.multiple_of(starts[p] * sub, sub), grp * sub), :]
            pv = _dot(jnp.concatenate(w_rows, axis=0), vwin)
            rows = slice(p * grp_rows, (p + 1) * grp_rows)
            for hd in range(2):
                acc_ref[hp, hd, rows, :] += pv[hd * grp_rows:(hd + 1) * grp_rows]

    def further_left(hp):
        q_heads = split_heads(hp)
        m_last = i * n_rb + n_rb - band

        def more_keys_matter(state):
            m, c_max = state
            return jnp.logical_and(m <= m_last, c_max > SB_F32_EXP2_ZERO)

        def one_more_sub_block(state):
            m, _ = state
            extend_band(hp, q_heads, m)
            return m + 1, largest_open_carry(c_ref[hp], m + 1)

        lax.while_loop(more_keys_matter, one_more_sub_block,
                       (1, largest_open_carry(c_ref[hp], 1)))
        o_scr[hp] = jnp.where(first, acc_ref[hp, 0], acc_ref[hp, 1]).astype(BF16)

    def per_head_pair_duo(j, largest_carry):
        hps = [j * duo + e for e in range(duo)]
        xn_next = xn_buf[slot]
        for hp in hps:
            append_projected(hp)
        q_heads = [split_heads(hp) for hp in hps]
        stacked = [band_logits(hp, q_heads[e]) for e, hp in enumerate(hps)]
        project_q(xn_next, j, next_slot)
        project_k(xn_next, j)
        s2 = []
        for e in range(duo):
            s2.append(_dot(stacked[e][0], u))
            if e == 0:
                project_v(xn_next, j)
        for e, hp in enumerate(hps):
            carry = [band_weights(hp, s2[e], stacked[e][1], p) for p in range(pairs)]
            largest_carry = jnp.maximum(largest_carry, jnp.concatenate(carry, axis=0))
        rows = pl.ds(pl.multiple_of(j * piece, piece), piece)
        xn_buf[next_slot, rows, :] = _rms(x2_ref[rows, :], gain).astype(BF16)
        return largest_carry

    largest_carry = lax.fori_loop(0, n_hp // duo, per_head_pair_duo,
                                  jnp.full((t, sub), -jnp.inf, F32))

    @pl.when(largest_open_carry(largest_carry, 1) > SB_F32_EXP2_ZERO)
    def _():
        def visit(hp, carry):
            further_left(hp)
            return carry

        lax.fori_loop(0, n_hp, visit, 0)

    o_ref[...] = jnp.concatenate([o_scr[hp] for hp in range(n_hp)], axis=1)


def _sb_mixer(h, gain, w_qkv):
    b, s, d = h.shape
    t = SB_TILE
    n_hp = d // LANES
    duo = SB_HP_PER_STEP
    n_tiles = s // t
    n_subs = SB_PAD_SUBS + s // SB_SUB
    w_duo = (w_qkv.reshape(d, 3, n_hp // duo, duo * LANES).transpose(2, 0, 1, 3)
             .reshape(n_hp // duo, d, 3 * duo * LANES))

    def tile_after_next(bi, i):
        tile = jnp.minimum(bi * n_tiles + i + 2, b * n_tiles - 1)
        return tile // n_tiles, tile % n_tiles, 0

    return pl.pallas_call(
        _sb_layer_kernel,
        grid=(b, n_tiles),
        in_specs=[
            pl.BlockSpec((None, t, d), lambda bi, i: (0, 0, 0), pipeline_mode=pl.Buffered(1)),
            pl.BlockSpec((None, t, d), lambda bi, i: (0, 1, 0), pipeline_mode=pl.Buffered(1)),
            pl.BlockSpec((None, t, d), tile_after_next),
            _resident((1, d)),
            _resident((n_hp // duo, d, 3 * duo * LANES)),
            _resident((2 * SB_SUB, 2 * SB_SUB)),
        ],
        out_specs=pl.BlockSpec((None, t, d), lambda bi, i: (bi, i, 0)),
        out_shape=jax.ShapeDtypeStruct((b, s, d), BF16),
        scratch_shapes=[
            pltpu.VMEM((2, n_hp, t, LANES), BF16),
            pltpu.VMEM((n_hp, n_subs, LANES, SB_SUB), BF16),
            pltpu.VMEM((n_hp, n_subs * SB_SUB, LANES), BF16),
            pltpu.VMEM((n_hp, t // SB_SUB, LANES, SB_SUB), BF16),
            pltpu.VMEM((n_hp, t, LANES), BF16),
            pltpu.VMEM((2, t, d), BF16),
            pltpu.VMEM((n_hp, t, LANES), BF16),
            pltpu.VMEM((n_hp, 2, t, LANES), F32),
            pltpu.VMEM((n_hp, 2, t, SB_SUB), F32),
        ],
        compiler_params=_params(("arbitrary", "arbitrary")),
        name="sb_layer",
    )(h, h, h, gain, w_duo, _sb_suffix_matrix())


def _conv_kernel(x_ref, xp_ref, g_ref, w_ref, cw_ref, a_ref, hbuf_ref):
    d = D_MODEL
    tm = TOKEN_TILE
    i = pl.program_id(1)
    gain = g_ref[...]
    xn = _rms(x_ref[...], gain).astype(BF16)
    xpn = _rms(xp_ref[...], gain).astype(BF16)
    hbuf_ref[SUBLANES:, :] = _dot(xn, w_ref[:, d:2 * d]) * _dot(xn, w_ref[:, 2 * d:3 * d])
    h_prev = _dot(xpn, w_ref[:, d:2 * d]) * _dot(xpn, w_ref[:, 2 * d:3 * d])
    hbuf_ref[0:SUBLANES, :] = jnp.where(i > 0, h_prev, 0.0)
    conv = cw_ref[0:1, :] * hbuf_ref[pl.ds(SUBLANES - 2, tm), :]
    conv = conv + cw_ref[1:2, :] * hbuf_ref[pl.ds(SUBLANES - 1, tm), :]
    conv = conv + cw_ref[2:3, :] * hbuf_ref[pl.ds(SUBLANES, tm), :]
    a_ref[...] = (_dot(xn, w_ref[:, 0:d]) * conv).astype(BF16)


def _conv_mixer(h, gain, w_in, conv_w):
    b, s, d = h.shape
    tm = TOKEN_TILE
    rows_per_tile = tm // SUBLANES
    return pl.pallas_call(
        _conv_kernel,
        grid=(b, s // tm),
        in_specs=[
            pl.BlockSpec((None, tm, d), lambda bi, i: (bi, i, 0)),
            pl.BlockSpec((None, SUBLANES, d),
                         lambda bi, i: (bi, jnp.maximum(i * rows_per_tile - 1, 0), 0)),
            _resident((1, d)),
            _resident((d, 3 * d)),
            _resident((CONV_WIDTH, d)),
        ],
        out_specs=pl.BlockSpec((None, tm, d), lambda bi, i: (bi, i, 0)),
        out_shape=jax.ShapeDtypeStruct((b, s, d), BF16),
        scratch_shapes=[pltpu.VMEM((tm + SUBLANES, d), F32)],
        compiler_params=_params(("parallel", "parallel")),
        name="conv_mixer",
    )(h, h, gain, w_in, conv_w)


def _gla_proj_kernel(x_ref, g_ref, w_ref, wa_ref, wgu_ref, bg_ref,
                     q_ref, kt_ref, v_ref, og_ref, lg_ref, lgt_ref):
    dk, dv = GLA_DK, GLA_DV
    xn = _rms(x_ref[...], g_ref[...]).astype(BF16)
    q_ref[...] = _dot(xn, w_ref[:, 0:dk]).astype(BF16)
    kt_ref[...] = _dot(xn, w_ref[:, dk:2 * dk]).T.astype(BF16)
    v_ref[...] = _dot(xn, w_ref[:, 2 * dk:2 * dk + dv]).astype(BF16)
    og_ref[...] = _dot(xn, w_ref[:, 2 * dk + dv:2 * dk + 2 * dv]).astype(BF16)
    a_low = _dot(xn, wa_ref[...]).astype(BF16)
    pre = _dot(a_low, wgu_ref[...]) + bg_ref[...]
    lg = _log_sigmoid(pre) * (1.0 / GLA_GATE_NORMALIZER)
    lg_ref[...] = lg
    lgt_ref[...] = lg.T


def _gla_proj(h, gain, w_main, w_a, w_gu, b_gate):
    b, s, d = h.shape
    tm = TOKEN_TILE
    dk, dv = GLA_DK, GLA_DV
    row = lambda bi, i: (bi, i, 0)
    colmajor = lambda bi, i: (bi, 0, i)
    return pl.pallas_call(
        _gla_proj_kernel,
        grid=(b, s // tm),
        in_specs=[
            pl.BlockSpec((None, tm, d), row),
            _resident((1, d)),
            _resident((d, 2 * dk + 2 * dv)),
            _resident((d, LANES)),
            _resident((LANES, dk)),
            _resident((1, dk)),
        ],
        out_specs=[
            pl.BlockSpec((None, tm, dk), row),
            pl.BlockSpec((None, dk, tm), colmajor),
            pl.BlockSpec((None, tm, dv), row),
            pl.BlockSpec((None, tm, dv), row),
            pl.BlockSpec((None, tm, dk), row),
            pl.BlockSpec((None, dk, tm), colmajor),
        ],
        out_shape=[
            jax.ShapeDtypeStruct((b, s, dk), BF16),
            jax.ShapeDtypeStruct((b, dk, s), BF16),
            jax.ShapeDtypeStruct((b, s, dv), BF16),
            jax.ShapeDtypeStruct((b, s, dv), BF16),
            jax.ShapeDtypeStruct((b, s, dk), F32),
            jax.ShapeDtypeStruct((b, dk, s), F32),
        ],
        compiler_params=_params(("parallel", "parallel")),
        name="gla_proj",
    )(h, gain, w_main, w_a, w_gu, b_gate)


def _gla_constants():
    c = GLA_TILE
    i = np.arange(c)[:, None]
    m = np.arange(c)[None, :]
    fwd, rev, masks = [], [], []
    for level in range(1, GLA_LEVELS + 1):
        n = 1 << level
        same = (i // n) == (m // n)
        fwd.append((same & (m <= i)).astype(np.float32))
        rev.append((same & (m > i)).astype(np.float32).T)
    rev.append(np.ones((c, c), np.float32))
    for level in range(GLA_LEVELS):
        n = 1 << level
        masks.append(((i // (2 * n) == m // (2 * n)) & ((i // n) % 2 == 1)
                      & ((m // n) % 2 == 0)).astype(np.float32))
    masks.append((i == m).astype(np.float32))
    fwd = np.concatenate(fwd, axis=0)
    rev = np.concatenate(rev, axis=1)
    fwd2 = np.concatenate([fwd, fwd], axis=1)
    rev2 = np.concatenate([rev, rev], axis=0)
    return (jnp.asarray(fwd2, dtype=BF16), jnp.asarray(rev2, dtype=BF16),
            jnp.asarray(np.stack(masks), dtype=F32))


def _gla_chunk_kernel(q_ref, kt_ref, v_ref, og_ref, lg_ref, lgt_ref, fwd_ref, rev_ref,
                      mask_ref, hn_ref, a_ref, state_ref):
    c = GLA_TILE
    n_ch = GLA_STEP_CHUNKS
    dk = GLA_DK
    dkh, dvh = GLA_DK_HEAD, GLA_DV_HEAD
    n_lv = GLA_LEVELS

    @pl.when(pl.program_id(1) == 0)
    def _():
        state_ref[...] = jnp.zeros_like(state_ref)

    lg = lg_ref[...]
    hi, lo = _split_bf16(lg)
    g_cat = jnp.concatenate(
        [jnp.concatenate([hi[ch * c:(ch + 1) * c], lo[ch * c:(ch + 1) * c]], axis=0)
         for ch in range(n_ch)], axis=1)
    fwd_all = _dot(fwd_ref[...], g_cat)
    hi_t, lo_t = _split_bf16(lgt_ref[...])
    gt_cat = jnp.concatenate(
        [jnp.concatenate([hi_t[:, ch * c:(ch + 1) * c], lo_t[:, ch * c:(ch + 1) * c]], axis=1)
         for ch in range(n_ch)], axis=0)
    rev_all = _dot(gt_cat, rev_ref[...])

    for hd in range(GLA_HEADS):
        ks = slice(hd * dkh, (hd + 1) * dkh)
        vs = slice(hd * dvh, (hd + 1) * dvh)
        state = state_ref[hd]
        for ch in range(n_ch):
            rows = slice(ch * c, (ch + 1) * c)
            kcols = slice(ch * dk + hd * dkh, ch * dk + (hd + 1) * dkh)
            q = q_ref[rows, ks].astype(F32) * GLA_SCALE
            kt = kt_ref[ks, rows].astype(F32)
            v = v_ref[rows, vs]
            kt_bf = kt.astype(BF16)
            scores = _dot(q.astype(BF16), kt_bf) * mask_ref[n_lv]
            qd = q * jnp.exp(lg[rows, ks])
            scores = scores + _dot(qd.astype(BF16), kt_bf) * mask_ref[0]
            for lv in range(1, n_lv):
                qd = q * jnp.exp(fwd_all[(lv - 1) * c:lv * c, kcols])
                kd = kt * jnp.exp(rev_all[kcols, (lv - 1) * c:lv * c])
                scores = scores + _dot(qd.astype(BF16), kd.astype(BF16)) * mask_ref[lv]
            q_cum = q * jnp.exp(fwd_all[(n_lv - 1) * c:n_lv * c, kcols])
            out = _dot(jnp.concatenate([q_cum.astype(BF16), scores.astype(BF16)], axis=1),
                       jnp.concatenate([state.astype(BF16), v], axis=0))
            k_rest = kt * jnp.exp(rev_all[kcols, (n_lv - 1) * c:n_lv * c])
            keep = jnp.exp(rev_all[kcols, n_lv * c:(n_lv + 1) * c])
            keep = jnp.concatenate([keep] * (dvh // c), axis=1)
            state = keep * state + _dot(k_rest.astype(BF16), v)
            o = _rms(out, hn_ref[:, vs])
            og = og_ref[rows, vs].astype(F32)
            a_ref[rows, vs] = (o * (og * (1.0 / (1.0 + jnp.exp(-og))))).astype(BF16)
        state_ref[hd] = state


def _gla_chunk(q, kt, v, og, lg, lgt, head_norm):
    b, s, dk = q.shape
    dv = v.shape[-1]
    c = GLA_TILE * GLA_STEP_CHUNKS
    fwd, rev, masks = _gla_constants()
    row = lambda bi, i: (bi, i, 0)
    colmajor = lambda bi, i: (bi, 0, i)
    return pl.pallas_call(
        _gla_chunk_kernel,
        grid=(b, s // c),
        in_specs=[
            pl.BlockSpec((None, c, dk), row),
            pl.BlockSpec((None, dk, c), colmajor),
            pl.BlockSpec((None, c, dv), row),
            pl.BlockSpec((None, c, dv), row),
            pl.BlockSpec((None, c, dk), row),
            pl.BlockSpec((None, dk, c), colmajor),
            _resident(fwd.shape),
            _resident(rev.shape),
            _resident(masks.shape),
            _resident((1, dv)),
        ],
        out_specs=pl.BlockSpec((None, c, dv), row),
        out_shape=jax.ShapeDtypeStruct((b, s, dv), BF16),
        scratch_shapes=[pltpu.VMEM((GLA_HEADS, GLA_DK_HEAD, GLA_DV_HEAD), F32)],
        compiler_params=_params(("parallel", "arbitrary")),
        name="gla_chunk",
    )(q, kt, v, og, lg, lgt, fwd, rev, masks, head_norm)


def _post_kernel(a_ref, h_ref, wo_ref, g_ref, wup_ref, wdn_ref, o_ref, acc_ref):
    half = TOKEN_TILE // 2
    halves = (slice(0, half), slice(half, TOKEN_TILE))
    w_o = wo_ref[...].astype(BF16)
    h1 = []
    xn = []
    for rows in halves:
        mixed = _dot(a_ref[rows, :], w_o)
        h1.append(h_ref[rows, :] + _rms(mixed, g_ref[1:2, :]))
        xn.append(_rms(h1[-1], g_ref[2:3, :]).astype(BF16))
    xn_full = jnp.concatenate(xn, axis=0)
    n_chunks = D_FF // FF_CHUNK
    for ci in range(n_chunks):
        cols = slice(ci * FF_CHUNK, (ci + 1) * FF_CHUNK)
        w_up = wup_ref[:, cols].astype(BF16)
        w_down = wdn_ref[cols, :].astype(BF16)
        if ci == 0:
            up = jnp.concatenate([_dot(part, w_up) for part in xn], axis=0)
        else:
            up = _dot(xn_full, w_up)
        up = jnp.maximum(up, 0.0)
        act = (up * up).astype(BF16)
        if ci == 0:
            acc_ref[...] = _dot(act, w_down)
        elif ci < n_chunks - 1:
            acc_ref[...] += _dot(act, w_down)
        else:
            for hi, rows in enumerate(halves):
                ffn = acc_ref[rows, :] + _dot(act[rows, :], w_down)
                o_ref[rows, :] = h1[hi] + _rms(ffn, g_ref[3:4, :])


def _layer_slice(shape, index):
    zeros = (0,) * len(shape)
    return pl.BlockSpec((None,) + tuple(shape), lambda *_: (index,) + zeros,
                        pipeline_mode=pl.Buffered(1))


def _post(a, h, w_o, mixer_index, gains, w_up, w_down, layer):
    b, s, d = h.shape
    tm = TOKEN_TILE
    row = lambda bi, i: (bi, i, 0)
    return pl.pallas_call(
        _post_kernel,
        grid=(b, s // tm),
        in_specs=[
            pl.BlockSpec((None, tm, d), row),
            pl.BlockSpec((None, tm, d), row),
            _layer_slice((d, d), mixer_index),
            _layer_slice((4, d), layer),
            _layer_slice((d, D_FF), layer),
            _layer_slice((D_FF, d), layer),
        ],
        out_specs=pl.BlockSpec((None, tm, d), row),
        out_shape=jax.ShapeDtypeStruct((b, s, d), F32),
        scratch_shapes=[pltpu.VMEM((tm, d), F32)],
        compiler_params=_params(("parallel", "parallel")),
        name="post_ffn",
    )(a, h, w_o, gains, w_up, w_down)


def kernel(x, norm_gains, sb_w_qkv, sb_w_o, conv_w_in, conv_w, conv_w_out, gla_w_in,
           gla_w_gate_up, gla_b_gate, gla_head_norm, gla_w_o, ffn_w_up, ffn_w_down):
    depth = norm_gains.shape[0]
    h = x
    for i in range(depth):
        kind, j = i % 3, i // 3
        gains = norm_gains[i]
        pre_gain = gains[0:1]
        if kind == 0:
            a = _sb_mixer(h, pre_gain, sb_w_qkv[j].astype(BF16))
            w_o = sb_w_o
        elif kind == 1:
            a = _conv_mixer(h, pre_gain, conv_w_in[j].astype(BF16), conv_w[j])
            w_o = conv_w_out
        else:
            n_main = 2 * GLA_DK + 2 * GLA_DV
            w_in = gla_w_in[j]
            w_a = jnp.pad(w_in[:, n_main:], ((0, 0), (0, LANES - GLA_GATE_RANK))).astype(BF16)
            w_gu = jnp.pad(gla_w_gate_up[j], ((0, LANES - GLA_GATE_RANK), (0, 0))).astype(BF16)
            q, kt, v, og, lg, lgt = _gla_proj(h, pre_gain, w_in[:, :n_main].astype(BF16), w_a,
                                              w_gu, gla_b_gate[j][None, :])
            a = _gla_chunk(q, kt, v, og, lg, lgt, gla_head_norm[j].reshape(1, GLA_DV))
            w_o = gla_w_o
        h = _post(a, h, w_o, j, norm_gains, ffn_w_up, ffn_w_down, i)
    return h
```

```python
import math

import numpy as np
import jax
import jax.numpy as jnp
from jax import lax
from jax.experimental import pallas as pl
from jax.experimental.pallas import tpu as pltpu

F32 = jnp.float32
BF16 = jnp.bfloat16

D_MODEL = 1024
D_FF = 4 * D_MODEL
RMS_EPS = 1e-6

SB_HEADS = 16
SB_HEAD_DIM = D_MODEL // SB_HEADS
LOG2E = math.log2(math.e)
SB_Q_SCALE = -(SB_HEAD_DIM ** -0.5) * LOG2E
CONV_WIDTH = 3
GLA_HEADS = 4
GLA_DK = D_MODEL // 2
GLA_DV = D_MODEL
GLA_DK_HEAD = GLA_DK // GLA_HEADS
GLA_DV_HEAD = GLA_DV // GLA_HEADS
GLA_GATE_RANK = 16
GLA_GATE_NORMALIZER = 16.0
GLA_SCALE = GLA_DK_HEAD ** -0.5

LANES = 128
SUBLANES = 8
VMEM_LIMIT_BYTES = 58 * 1024 * 1024

TOKEN_TILE = 512
FF_CHUNK = 1024
SB_TILE = 512
SB_SUB = 128
SB_BAND_SUBS = 3
SB_GROUP = 2
SB_PAD_SUBS = SB_BAND_SUBS - 1
SB_HP_PER_STEP = 2
SB_F32_EXP2_ZERO = -152.0
GLA_TILE = 128
GLA_STEP_CHUNKS = 2
GLA_LEVELS = 7


def _rms(x, gain):
    ms = jnp.mean(x * x, axis=-1, keepdims=True)
    return x * lax.rsqrt(ms + RMS_EPS) * gain


def _neg_abs(z):
    bits = lax.bitcast_convert_type(z, jnp.uint32) | jnp.uint32(0x80000000)
    return lax.bitcast_convert_type(bits, F32)


def _log_sigmoid(z):
    return jnp.minimum(z, 0.0) - jnp.log(1.0 + jnp.exp(_neg_abs(z)))


def _log2_one_minus_sigmoid(y):
    return jnp.minimum(y, 0.0) - jnp.log(1.0 + jnp.exp2(_neg_abs(y))) * LOG2E


def _split_bf16(x):
    hi = x.astype(BF16)
    lo = (x - hi.astype(F32)).astype(BF16)
    return hi, lo


def _dot(a, b):
    return jnp.dot(a, b, preferred_element_type=F32)


def _resident(shape):
    zeros = (0,) * len(shape)
    return pl.BlockSpec(shape, lambda *_: zeros, pipeline_mode=pl.Buffered(1))


def _params(semantics):
    return pltpu.CompilerParams(dimension_semantics=semantics,
                                vmem_limit_bytes=VMEM_LIMIT_BYTES)


def _sb_suffix_matrix():
    t = SB_SUB
    m = np.arange(t)[:, None]
    j = np.arange(t)[None, :]
    half = np.concatenate([(m >= j).astype(np.float32), np.ones((t, t), np.float32)], axis=1)
    return jnp.asarray(np.concatenate([half, half], axis=0), dtype=BF16)


def _sb_layer_kernel(x0_ref, x1_ref, x2_ref, g_ref, w_ref, u_ref, o_ref,
                     q_scr, kt_scr, v_scr, kt_new, v_new, xn_buf, o_scr, acc_ref, c_ref):
    t = SB_TILE
    sub = SB_SUB
    n_rb = t // sub
    band = SB_BAND_SUBS
    pad = SB_PAD_SUBS
    grp = SB_GROUP
    win = band + grp - 1
    pairs = n_rb // grp
    grp_rows = grp * sub
    n_hp = D_MODEL // LANES
    duo = SB_HP_PER_STEP
    i = pl.program_id(1)
    step = pl.program_id(0) * pl.num_programs(1) + i
    gain = g_ref[...]
    u = u_ref[...]

    def weight_columns(which, j):
        width = duo * LANES
        return w_ref[:, pl.ds(pl.multiple_of(which * D_MODEL + j * width, width), width)]

    def project_q(xn, j, slot):
        y = _dot(xn, weight_columns(0, j)) * SB_Q_SCALE
        for e in range(duo):
            q_scr[slot, j * duo + e] = y[:, e * LANES:(e + 1) * LANES].astype(BF16)

    def project_k(xn, j):
        y = _dot(xn, weight_columns(1, j))
        for e in range(duo):
            for kb in range(n_rb):
                kt_new[j * duo + e, kb] = (
                    y[kb * sub:(kb + 1) * sub, e * LANES:(e + 1) * LANES].T.astype(BF16))

    def project_v(xn, j):
        y = _dot(xn, weight_columns(2, j)).astype(BF16)
        for e in range(duo):
            v_new[j * duo + e] = y[:, e * LANES:(e + 1) * LANES]

    @pl.when(i == 0)
    def _():
        kt_scr[:, 0:pad] = jnp.zeros((n_hp, pad, LANES, sub), BF16)
        v_scr[:, 0:pad * sub, :] = jnp.zeros((n_hp, pad * sub, LANES), BF16)

    @pl.when(step == 0)
    def _():
        xn = _rms(x0_ref[...], gain).astype(BF16)

        def first_tile(j, carry):
            project_q(xn, j, 0)
            project_k(xn, j)
            project_v(xn, j)
            return carry

        lax.fori_loop(0, n_hp // duo, first_tile, 0)
        xn_buf[0] = _rms(x1_ref[...], gain).astype(BF16)

    slot = step % 2
    next_slot = (step + 1) % 2
    piece = t // (n_hp // duo)

    lane = lax.broadcasted_iota(jnp.int32, (t, LANES), 1)
    first = lane < SB_HEAD_DIM
    first_grp = lax.broadcasted_iota(jnp.int32, (grp_rows, LANES), 1) < SB_HEAD_DIM
    row = lax.broadcasted_iota(jnp.int32, (sub, sub), 0)
    col = lax.broadcasted_iota(jnp.int32, (sub, sub), 1)
    diag = col < row
    no_weight = jnp.zeros((sub, sub), BF16)
    row_blk = lax.broadcasted_iota(jnp.int32, (t, sub), 0) // sub
    not_first_tile = i > 0

    def append_projected(hp):
        for kb in range(n_rb):
            kt_scr[hp, pad + i * n_rb + kb] = kt_new[hp, kb]
        v_scr[hp, pl.ds(pl.multiple_of((pad + i * n_rb) * sub, sub), t), :] = v_new[hp]

    def split_heads(hp):
        q = q_scr[slot, hp]
        zero = jnp.zeros_like(q)
        return jnp.where(first, q, zero), jnp.where(first, zero, q)

    def band_logits(hp, q_heads):
        parts = []
        tiles = {}
        for p in range(pairs):
            first_sub = pad + i * n_rb + p * grp - band + 1
            kwin = jnp.concatenate([kt_scr[hp, first_sub + s] for s in range(win)], axis=1)
            q_both = jnp.concatenate(
                [q_heads[hd][p * grp_rows:(p + 1) * grp_rows] for hd in range(2)], axis=0)
            y = _dot(q_both, kwin)
            for hd in range(2):
                for rr in range(grp):
                    top = hd * grp_rows + rr * sub
                    for s in range(band):
                        yt = y[top:top + sub, (rr + s) * sub:(rr + s + 1) * sub]
                        part = _log2_one_minus_sigmoid(yt)
                        mask = diag if s == band - 1 else None
                        if p * grp + rr + s < band - 1:
                            mask = not_first_tile if mask is None else mask & not_first_tile
                        if mask is not None:
                            part = jnp.where(mask, part, 0.0)
                        hi, lo = _split_bf16(part)
                        tiles[p, hd, rr, s] = (len(parts), yt, mask)
                        parts.append(jnp.concatenate([hi, lo], axis=1))
        return jnp.concatenate(parts, axis=0), tiles

    def band_weights(hp, s2, tiles, p):
        first_sub = pad + i * n_rb + p * grp - band + 1
        w_rows = []
        carries = []
        for hd in range(2):
            for rr in range(grp):
                r = p * grp + rr
                c = jnp.zeros((sub, sub), F32)
                ws = [None] * band
                for s in reversed(range(band)):
                    index, yt, mask = tiles[p, hd, rr, s]
                    blk = s2[index * sub:(index + 1) * sub]
                    w = jnp.exp2(blk[:, :sub] + c - yt)
                    if mask is not None:
                        w = jnp.where(mask, w, 0.0)
                    ws[s] = w.astype(BF16)
                    c = c + blk[:, sub:]
                c_ref[hp, hd, r * sub:(r + 1) * sub, :] = c
                carries.append(c)
                w_rows.append(jnp.concatenate(
                    [no_weight] * rr + ws + [no_weight] * (grp - 1 - rr), axis=1))
        vwin = v_scr[hp, pl.ds(pl.multiple_of(first_sub * sub, sub), win * sub), :]
        pv = _dot(jnp.concatenate(w_rows, axis=0), vwin)
        rows = slice(p * grp_rows, (p + 1) * grp_rows)
        for hd in range(2):
            acc_ref[hp, hd, rows, :] = pv[hd * grp_rows:(hd + 1) * grp_rows]
        o_scr[hp, rows, :] = jnp.where(first_grp, pv[:grp_rows], pv[grp_rows:]).astype(BF16)
        return jnp.concatenate(
            [jnp.maximum(carries[rr], carries[grp + rr]) for rr in range(grp)], axis=0)

    def largest_open_carry(c, m):
        has_keys = row_blk + i * n_rb - band + 1 - m >= 0
        return jnp.max(jnp.where(has_keys, c, -jnp.inf))

    def extend_band(hp, q_heads, m):
        parts = []
        tiles = {}
        starts = []
        for p in range(pairs):
            first_sub = i * n_rb + p * grp - band + 1 - m
            start = jnp.maximum(pad + first_sub, 0)
            starts.append(start)
            kwin = jnp.concatenate([kt_scr[hp, start + s] for s in range(grp)], axis=1)
            q_both = jnp.concatenate(
                [q_heads[hd][p * grp_rows:(p + 1) * grp_rows] for hd in range(2)], axis=0)
            y = _dot(q_both, kwin)
            for hd in range(2):
                for rr in range(grp):
                    top = hd * grp_rows + rr * sub
                    yt = y[top:top + sub, rr * sub:(rr + 1) * sub]
                    in_sequence = first_sub + rr >= 0
                    part = jnp.where(in_sequence, _log2_one_minus_sigmoid(yt), 0.0)
                    hi, lo = _split_bf16(part)
                    tiles[p, hd, rr] = (len(parts), yt, in_sequence)
                    parts.append(jnp.concatenate([hi, lo], axis=1))
        s2 = _dot(jnp.concatenate(parts, axis=0), u)
        for p in range(pairs):
            w_rows = []
            for hd in range(2):
                for rr in range(grp):
                    rows = slice((p * grp + rr) * sub, (p * grp + rr + 1) * sub)
                    index, yt, in_sequence = tiles[p, hd, rr]
                    blk = s2[index * sub:(index + 1) * sub]
                    c = c_ref[hp, hd, rows, :]
                    w = jnp.where(in_sequence, jnp.exp2(blk[:, :sub] + c - yt), 0.0).astype(BF16)
                    c_ref[hp, hd, rows, :] = c + blk[:, sub:]
                    w_rows.append(jnp.concatenate(
                        [no_weight] * rr + [w] + [no_weight] * (grp - 1 - rr), axis=1))
            vwin = v_scr[hp, pl.ds(pl.multiple_of(starts[p] * sub, sub), grp * sub), :]
            pv = _dot(jnp.concatenate(w_rows, axis=0), vwin)
            rows = slice(p * grp_rows, (p + 1) * grp_rows)
            for hd in range(2):
                acc_ref[hp, hd, rows, :] += pv[hd * grp_rows:(hd + 1) * grp_rows]

    def further_left(hp):
        q_heads = split_heads(hp)
        m_last = i * n_rb + n_rb - band

        def more_keys_matter(state):
            m, c_max = state
            return jnp.logical_and(m <= m_last, c_max > SB_F32_EXP2_ZERO)

        def one_more_sub_block(state):
            m, _ = state
            extend_band(hp, q_heads, m)
            return m + 1, largest_open_carry(c_ref[hp], m + 1)

        lax.while_loop(more_keys_matter, one_more_sub_block,
                       (1, largest_open_carry(c_ref[hp], 1)))
        o_scr[hp] = jnp.where(first, acc_ref[hp, 0], acc_ref[hp, 1]).astype(BF16)

    def per_head_pair_duo(j, largest_carry):
        hps = [j * duo + e for e in range(duo)]
        xn_next = xn_buf[slot]
        for hp in hps:
            append_projected(hp)
        q_heads = [split_heads(hp) for hp in hps]
        stacked = [band_logits(hp, q_heads[e]) for e, hp in enumerate(hps)]
        project_q(xn_next, j, next_slot)
        project_k(xn_next, j)
        s2 = []
        for e in range(duo):
            s2.append(_dot(stacked[e][0], u))
            if e == 0:
                project_v(xn_next, j)
        for e, hp in enumerate(hps):
            carry = [band_weights(hp, s2[e], stacked[e][1], p) for p in range(pairs)]
            largest_carry = jnp.maximum(largest_carry, jnp.concatenate(carry, axis=0))
        rows = pl.ds(pl.multiple_of(j * piece, piece), piece)
        xn_buf[next_slot, rows, :] = _rms(x2_ref[rows, :], gain).astype(BF16)
        return largest_carry

    largest_carry = lax.fori_loop(0, n_hp // duo, per_head_pair_duo,
                                  jnp.full((t, sub), -jnp.inf, F32))

    @pl.when(largest_open_carry(largest_carry, 1) > SB_F32_EXP2_ZERO)
    def _():
        def visit(hp, carry):
            further_left(hp)
            return carry

        lax.fori_loop(0, n_hp, visit, 0)

    o_ref[...] = jnp.concatenate([o_scr[hp] for hp in range(n_hp)], axis=1)


def _sb_mixer(h, gain, w_qkv):
    b, s, d = h.shape
    t = SB_TILE
    n_hp = d // LANES
    duo = SB_HP_PER_STEP
    n_tiles = s // t
    n_subs = SB_PAD_SUBS + s // SB_SUB

    def tile_after_next(bi, i):
        tile = jnp.minimum(bi * n_tiles + i + 2, b * n_tiles - 1)
        return tile // n_tiles, tile % n_tiles, 0

    return pl.pallas_call(
        _sb_layer_kernel,
        grid=(b, n_tiles),
        in_specs=[
            pl.BlockSpec((None, t, d), lambda bi, i: (0, 0, 0), pipeline_mode=pl.Buffered(1)),
            pl.BlockSpec((None, t, d), lambda bi, i: (0, 1, 0), pipeline_mode=pl.Buffered(1)),
            pl.BlockSpec((None, t, d), tile_after_next),
            _resident((1, d)),
            _resident((d, 3 * d)),
            _resident((2 * SB_SUB, 2 * SB_SUB)),
        ],
        out_specs=pl.BlockSpec((None, t, d), lambda bi, i: (bi, i, 0)),
        out_shape=jax.ShapeDtypeStruct((b, s, d), BF16),
        scratch_shapes=[
            pltpu.VMEM((2, n_hp, t, LANES), BF16),
            pltpu.VMEM((n_hp, n_subs, LANES, SB_SUB), BF16),
            pltpu.VMEM((n_hp, n_subs * SB_SUB, LANES), BF16),
            pltpu.VMEM((n_hp, t // SB_SUB, LANES, SB_SUB), BF16),
            pltpu.VMEM((n_hp, t, LANES), BF16),
            pltpu.VMEM((2, t, d), BF16),
            pltpu.VMEM((n_hp, t, LANES), BF16),
            pltpu.VMEM((n_hp, 2, t, LANES), F32),
            pltpu.VMEM((n_hp, 2, t, SB_SUB), F32),
        ],
        compiler_params=_params(("arbitrary", "arbitrary")),
        name="sb_layer",
    )(h, h, h, gain, w_qkv, _sb_suffix_matrix())


def _conv_kernel(x_ref, xp_ref, g_ref, w_ref, cw_ref, a_ref, hbuf_ref):
    d = D_MODEL
    tm = TOKEN_TILE
    i = pl.program_id(1)
    gain = g_ref[...]
    xn = _rms(x_ref[...], gain).astype(BF16)
    xpn = _rms(xp_ref[...], gain).astype(BF16)
    w_c = w_ref[:, d:2 * d].astype(BF16)
    w_u = w_ref[:, 2 * d:3 * d].astype(BF16)
    hbuf_ref[SUBLANES:, :] = _dot(xn, w_c) * _dot(xn, w_u)
    h_prev = _dot(xpn, w_c) * _dot(xpn, w_u)
    hbuf_ref[0:SUBLANES, :] = jnp.where(i > 0, h_prev, 0.0)
    conv = cw_ref[0:1, :] * hbuf_ref[pl.ds(SUBLANES - 2, tm), :]
    conv = conv + cw_ref[1:2, :] * hbuf_ref[pl.ds(SUBLANES - 1, tm), :]
    conv = conv + cw_ref[2:3, :] * hbuf_ref[pl.ds(SUBLANES, tm), :]
    a_ref[...] = (_dot(xn, w_ref[:, 0:d].astype(BF16)) * conv).astype(BF16)


def _conv_mixer(h, gain, w_in, conv_w, index):
    b, s, d = h.shape
    tm = TOKEN_TILE
    rows_per_tile = tm // SUBLANES
    return pl.pallas_call(
        _conv_kernel,
        grid=(b, s // tm),
        in_specs=[
            pl.BlockSpec((None, tm, d), lambda bi, i: (bi, i, 0)),
            pl.BlockSpec((None, SUBLANES, d),
                         lambda bi, i: (bi, jnp.maximum(i * rows_per_tile - 1, 0), 0)),
            _resident((1, d)),
            _layer_slice((d, 3 * d), index),
            _layer_slice((CONV_WIDTH, d), index),
        ],
        out_specs=pl.BlockSpec((None, tm, d), lambda bi, i: (bi, i, 0)),
        out_shape=jax.ShapeDtypeStruct((b, s, d), BF16),
        scratch_shapes=[pltpu.VMEM((tm + SUBLANES, d), F32)],
        compiler_params=_params(("parallel", "parallel")),
        name="conv_mixer",
    )(h, h, gain, w_in, conv_w)


def _gla_proj_kernel(x_ref, g_ref, w_ref, wa_ref, wgu_ref, bg_ref,
                     q_ref, kt_ref, v_ref, og_ref, lg_ref, lgt_ref):
    dk, dv = GLA_DK, GLA_DV
    xn = _rms(x_ref[...], g_ref[...]).astype(BF16)
    q_ref[...] = _dot(xn, w_ref[:, 0:dk].astype(BF16)).astype(BF16)
    kt_ref[...] = _dot(xn, w_ref[:, dk:2 * dk].astype(BF16)).T.astype(BF16)
    v_ref[...] = _dot(xn, w_ref[:, 2 * dk:2 * dk + dv].astype(BF16)).astype(BF16)
    og_ref[...] = _dot(xn, w_ref[:, 2 * dk + dv:2 * dk + 2 * dv].astype(BF16)).astype(BF16)
    a_low = _dot(xn, wa_ref[...]).astype(BF16)
    pre = _dot(a_low, wgu_ref[...]) + bg_ref[...]
    lg = _log_sigmoid(pre) * (1.0 / GLA_GATE_NORMALIZER)
    lg_ref[...] = lg
    lgt_ref[...] = lg.T


def _gla_proj(h, gain, w_in, index, w_a, w_gu, b_gate):
    b, s, d = h.shape
    tm = TOKEN_TILE
    dk, dv = GLA_DK, GLA_DV
    row = lambda bi, i: (bi, i, 0)
    colmajor = lambda bi, i: (bi, 0, i)
    return pl.pallas_call(
        _gla_proj_kernel,
        grid=(b, s // tm),
        in_specs=[
            pl.BlockSpec((None, tm, d), row),
            _resident((1, d)),
            _layer_slice((d, 2 * dk + 2 * dv), index),
            _resident((d, LANES)),
            _resident((LANES, dk)),
            _resident((1, dk)),
        ],
        out_specs=[
            pl.BlockSpec((None, tm, dk), row),
            pl.BlockSpec((None, dk, tm), colmajor),
            pl.BlockSpec((None, tm, dv), row),
            pl.BlockSpec((None, tm, dv), row),
            pl.BlockSpec((None, tm, dk), row),
            pl.BlockSpec((None, dk, tm), colmajor),
        ],
        out_shape=[
            jax.ShapeDtypeStruct((b, s, dk), BF16),
            jax.ShapeDtypeStruct((b, dk, s), BF16),
            jax.ShapeDtypeStruct((b, s, dv), BF16),
            jax.ShapeDtypeStruct((b, s, dv), BF16),
            jax.ShapeDtypeStruct((b, s, dk), F32),
            jax.ShapeDtypeStruct((b, dk, s), F32),
        ],
        compiler_params=_params(("parallel", "parallel")),
        name="gla_proj",
    )(h, gain, w_in, w_a, w_gu, b_gate)


def _gla_constants():
    c = GLA_TILE
    i = np.arange(c)[:, None]
    m = np.arange(c)[None, :]
    fwd, rev, masks = [], [], []
    for level in range(1, GLA_LEVELS + 1):
        n = 1 << level
        same = (i // n) == (m // n)
        fwd.append((same & (m <= i)).astype(np.float32))
        rev.append((same & (m > i)).astype(np.float32).T)
    rev.append(np.ones((c, c), np.float32))
    for level in range(GLA_LEVELS):
        n = 1 << level
        masks.append(((i // (2 * n) == m // (2 * n)) & ((i // n) % 2 == 1)
                      & ((m // n) % 2 == 0)).astype(np.float32))
    masks.append((i == m).astype(np.float32))
    fwd = np.concatenate(fwd, axis=0)
    rev = np.concatenate(rev, axis=1)
    fwd2 = np.concatenate([fwd, fwd], axis=1)
    rev2 = np.concatenate([rev, rev], axis=0)
    return (jnp.asarray(fwd2, dtype=BF16), jnp.asarray(rev2, dtype=BF16),
            jnp.asarray(np.stack(masks), dtype=F32))


def _gla_chunk_kernel(q_ref, kt_ref, v_ref, og_ref, lg_ref, lgt_ref, fwd_ref, rev_ref,
                      mask_ref, hn_ref, a_ref, state_ref):
    c = GLA_TILE
    n_ch = GLA_STEP_CHUNKS
    dk = GLA_DK
    dkh, dvh = GLA_DK_HEAD, GLA_DV_HEAD
    n_lv = GLA_LEVELS

    @pl.when(pl.program_id(1) == 0)
    def _():
        state_ref[...] = jnp.zeros_like(state_ref)

    lg = lg_ref[...]
    hi, lo = _split_bf16(lg)
    g_cat = jnp.concatenate(
        [jnp.concatenate([hi[ch * c:(ch + 1) * c], lo[ch * c:(ch + 1) * c]], axis=0)
         for ch in range(n_ch)], axis=1)
    fwd_all = _dot(fwd_ref[...], g_cat)
    hi_t, lo_t = _split_bf16(lgt_ref[...])
    gt_cat = jnp.concatenate(
        [jnp.concatenate([hi_t[:, ch * c:(ch + 1) * c], lo_t[:, ch * c:(ch + 1) * c]], axis=1)
         for ch in range(n_ch)], axis=0)
    rev_all = _dot(gt_cat, rev_ref[...])

    for hd in range(GLA_HEADS):
        ks = slice(hd * dkh, (hd + 1) * dkh)
        vs = slice(hd * dvh, (hd + 1) * dvh)
        state = state_ref[hd]
        for ch in range(n_ch):
            rows = slice(ch * c, (ch + 1) * c)
            kcols = slice(ch * dk + hd * dkh, ch * dk + (hd + 1) * dkh)
            q = q_ref[rows, ks].astype(F32) * GLA_SCALE
            kt = kt_ref[ks, rows].astype(F32)
            v = v_ref[rows, vs]
            kt_bf = kt.astype(BF16)
            scores = _dot(q.astype(BF16), kt_bf) * mask_ref[n_lv]
            qd = q * jnp.exp(lg[rows, ks])
            scores = scores + _dot(qd.astype(BF16), kt_bf) * mask_ref[0]
            for lv in range(1, n_lv):
                qd = q * jnp.exp(fwd_all[(lv - 1) * c:lv * c, kcols])
                kd = kt * jnp.exp(rev_all[kcols, (lv - 1) * c:lv * c])
                scores = scores + _dot(qd.astype(BF16), kd.astype(BF16)) * mask_ref[lv]
            q_cum = q * jnp.exp(fwd_all[(n_lv - 1) * c:n_lv * c, kcols])
            out = _dot(jnp.concatenate([q_cum.astype(BF16), scores.astype(BF16)], axis=1),
                       jnp.concatenate([state.astype(BF16), v], axis=0))
            k_rest = kt * jnp.exp(rev_all[kcols, (n_lv - 1) * c:n_lv * c])
            keep = jnp.exp(rev_all[kcols, n_lv * c:(n_lv + 1) * c])
            keep = jnp.concatenate([keep] * (dvh // c), axis=1)
            state = keep * state + _dot(k_rest.astype(BF16), v)
            o = _rms(out, hn_ref[:, vs])
            og = og_ref[rows, vs].astype(F32)
            a_ref[rows, vs] = (o * (og * (1.0 / (1.0 + jnp.exp(-og))))).astype(BF16)
        state_ref[hd] = state


def _gla_chunk(q, kt, v, og, lg, lgt, head_norm):
    b, s, dk = q.shape
    dv = v.shape[-1]
    c = GLA_TILE * GLA_STEP_CHUNKS
    fwd, rev, masks = _gla_constants()
    row = lambda bi, i: (bi, i, 0)
    colmajor = lambda bi, i: (bi, 0, i)
    return pl.pallas_call(
        _gla_chunk_kernel,
        grid=(b, s // c),
        in_specs=[
            pl.BlockSpec((None, c, dk), row),
            pl.BlockSpec((None, dk, c), colmajor),
            pl.BlockSpec((None, c, dv), row),
            pl.BlockSpec((None, c, dv), row),
            pl.BlockSpec((None, c, dk), row),
            pl.BlockSpec((None, dk, c), colmajor),
            _resident(fwd.shape),
            _resident(rev.shape),
            _resident(masks.shape),
            _resident((1, dv)),
        ],
        out_specs=pl.BlockSpec((None, c, dv), row),
        out_shape=jax.ShapeDtypeStruct((b, s, dv), BF16),
        scratch_shapes=[pltpu.VMEM((GLA_HEADS, GLA_DK_HEAD, GLA_DV_HEAD), F32)],
        compiler_params=_params(("parallel", "arbitrary")),
        name="gla_chunk",
    )(q, kt, v, og, lg, lgt, fwd, rev, masks, head_norm)


def _post_kernel(a_ref, h_ref, wo_ref, g_ref, wup_ref, wdn_ref, o_ref, acc_ref):
    half = TOKEN_TILE // 2
    halves = (slice(0, half), slice(half, TOKEN_TILE))
    w_o = wo_ref[...].astype(BF16)
    h1 = []
    xn = []
    for rows in halves:
        mixed = _dot(a_ref[rows, :], w_o)
        h1.append(h_ref[rows, :] + _rms(mixed, g_ref[1:2, :]))
        xn.append(_rms(h1[-1], g_ref[2:3, :]).astype(BF16))
    xn_full = jnp.concatenate(xn, axis=0)
    n_chunks = D_FF // FF_CHUNK
    for ci in range(n_chunks):
        cols = slice(ci * FF_CHUNK, (ci + 1) * FF_CHUNK)
        w_up = wup_ref[:, cols].astype(BF16)
        w_down = wdn_ref[cols, :].astype(BF16)
        if ci == 0:
            up = jnp.concatenate([_dot(part, w_up) for part in xn], axis=0)
        else:
            up = _dot(xn_full, w_up)
        up = jnp.maximum(up, 0.0)
        act = (up * up).astype(BF16)
        if ci == 0:
            acc_ref[...] = _dot(act, w_down)
        elif ci < n_chunks - 1:
            acc_ref[...] += _dot(act, w_down)
        else:
            for hi, rows in enumerate(halves):
                ffn = acc_ref[rows, :] + _dot(act[rows, :], w_down)
                o_ref[rows, :] = h1[hi] + _rms(ffn, g_ref[3:4, :])


def _layer_slice(shape, index):
    zeros = (0,) * len(shape)
    return pl.BlockSpec((None,) + tuple(shape), lambda *_: (index,) + zeros,
                        pipeline_mode=pl.Buffered(1))


def _post(a, h, w_o, mixer_index, gains, w_up, w_down, layer):
    b, s, d = h.shape
    tm = TOKEN_TILE
    row = lambda bi, i: (bi, i, 0)
    return pl.pallas_call(
        _post_kernel,
        grid=(b, s // tm),
        in_specs=[
            pl.BlockSpec((None, tm, d), row),
            pl.BlockSpec((None, tm, d), row),
            _layer_slice((d, d), mixer_index),
            _layer_slice((4, d), layer),
            _layer_slice((d, D_FF), layer),
            _layer_slice((D_FF, d), layer),
        ],
        out_specs=pl.BlockSpec((None, tm, d), row),
        out_shape=jax.ShapeDtypeStruct((b, s, d), F32),
        scratch_shapes=[pltpu.VMEM((tm, d), F32)],
        compiler_params=_params(("parallel", "parallel")),
        name="post_ffn",
    )(a, h, w_o, gains, w_up, w_down)


def kernel(x, norm_gains, sb_w_qkv, sb_w_o, conv_w_in, conv_w, conv_w_out, gla_w_in,
           gla_w_gate_up, gla_b_gate, gla_head_norm, gla_w_o, ffn_w_up, ffn_w_down):
    depth = norm_gains.shape[0]
    h = x
    for i in range(depth):
        kind, j = i % 3, i // 3
        gains = norm_gains[i]
        pre_gain = gains[0:1]
        if kind == 0:
            a = _sb_mixer(h, pre_gain, sb_w_qkv[j].astype(BF16))
            w_o = sb_w_o
        elif kind == 1:
            a = _conv_mixer(h, pre_gain, conv_w_in, conv_w, j)
            w_o = conv_w_out
        else:
            n_main = 2 * GLA_DK + 2 * GLA_DV
            w_in = gla_w_in[j]
            w_a = jnp.pad(w_in[:, n_main:], ((0, 0), (0, LANES - GLA_GATE_RANK))).astype(BF16)
            w_gu = jnp.pad(gla_w_gate_up[j], ((0, LANES - GLA_GATE_RANK), (0, 0))).astype(BF16)
            q, kt, v, og, lg, lgt = _gla_proj(h, pre_gain, gla_w_in, j, w_a, w_gu,
                                              gla_b_gate[j][None, :])
            a = _gla_chunk(q, kt, v, og, lg, lgt, gla_head_norm[j].reshape(1, GLA_DV))
            w_o = gla_w_o
        h = _post(a, h, w_o, j, norm_gains, ffn_w_up, ffn_w_down, i)
    return h
```

```python
import math

import numpy as np
import jax
import jax.numpy as jnp
from jax import lax
from jax.experimental import pallas as pl
from jax.experimental.pallas import tpu as pltpu

F32 = jnp.float32
BF16 = jnp.bfloat16

D_MODEL = 1024
D_FF = 4 * D_MODEL
RMS_EPS = 1e-6

SB_HEADS = 16
SB_HEAD_DIM = D_MODEL // SB_HEADS
LOG2E = math.log2(math.e)
SB_Q_SCALE = -(SB_HEAD_DIM ** -0.5) * LOG2E
CONV_WIDTH = 3
GLA_HEADS = 4
GLA_DK = D_MODEL // 2
GLA_DV = D_MODEL
GLA_DK_HEAD = GLA_DK // GLA_HEADS
GLA_DV_HEAD = GLA_DV // GLA_HEADS
GLA_GATE_RANK = 16
GLA_GATE_NORMALIZER = 16.0
GLA_SCALE = GLA_DK_HEAD ** -0.5

LANES = 128
SUBLANES = 8
VMEM_LIMIT_BYTES = 58 * 1024 * 1024

TOKEN_TILE = 512
FF_CHUNK = 1024
SB_TILE = 512
SB_SUB = 128
SB_BAND_SUBS = 3
SB_GROUP = 2
SB_PAD_SUBS = SB_BAND_SUBS - 1
SB_HP_PER_STEP = 2
SB_F32_EXP2_ZERO = -152.0
GLA_TILE = 128
GLA_STEP_CHUNKS = 2
GLA_LEVELS = 7


def _rms(x, gain):
    ms = jnp.mean(x * x, axis=-1, keepdims=True)
    return x * lax.rsqrt(ms + RMS_EPS) * gain


def _neg_abs(z):
    bits = lax.bitcast_convert_type(z, jnp.uint32) | jnp.uint32(0x80000000)
    return lax.bitcast_convert_type(bits, F32)


def _log_sigmoid(z):
    return jnp.minimum(z, 0.0) - jnp.log(1.0 + jnp.exp(_neg_abs(z)))


def _log2_one_minus_sigmoid(y):
    return jnp.minimum(y, 0.0) - jnp.log(1.0 + jnp.exp2(_neg_abs(y))) * LOG2E


def _split_bf16(x):
    hi = x.astype(BF16)
    lo = (x - hi.astype(F32)).astype(BF16)
    return hi, lo


def _dot(a, b):
    return jnp.dot(a, b, preferred_element_type=F32)


def _resident(shape):
    zeros = (0,) * len(shape)
    return pl.BlockSpec(shape, lambda *_: zeros, pipeline_mode=pl.Buffered(1))


def _params(semantics):
    return pltpu.CompilerParams(dimension_semantics=semantics,
                                vmem_limit_bytes=VMEM_LIMIT_BYTES)


def _sb_suffix_matrix():
    t = SB_SUB
    m = np.arange(t)[:, None]
    j = np.arange(t)[None, :]
    half = np.concatenate([(m >= j).astype(np.float32), np.ones((t, t), np.float32)], axis=1)
    return jnp.asarray(np.concatenate([half, half], axis=0), dtype=BF16)


def _sb_layer_kernel(x0_ref, x1_ref, x2_ref, g_ref, w_ref, u_ref, o_ref,
                     q_scr, kt_scr, v_scr, kt_new, v_new, xn_buf, o_scr, acc_ref, c_ref):
    t = SB_TILE
    sub = SB_SUB
    n_rb = t // sub
    band = SB_BAND_SUBS
    pad = SB_PAD_SUBS
    grp = SB_GROUP
    win = band + grp - 1
    pairs = n_rb // grp
    grp_rows = grp * sub
    n_hp = D_MODEL // LANES
    duo = SB_HP_PER_STEP
    i = pl.program_id(1)
    step = pl.program_id(0) * pl.num_programs(1) + i
    gain = g_ref[...]
    u = u_ref[...]

    def weight_columns(which, j):
        width = duo * LANES
        return w_ref[:, pl.ds(pl.multiple_of(which * D_MODEL + j * width, width), width)]

    def project_q(xn, j, slot):
        y = _dot(xn, weight_columns(0, j)) * SB_Q_SCALE
        for e in range(duo):
            q_scr[slot, j * duo + e] = y[:, e * LANES:(e + 1) * LANES].astype(BF16)

    def project_k(xn, j):
        y = _dot(xn, weight_columns(1, j))
        for e in range(duo):
            for kb in range(n_rb):
                kt_new[j * duo + e, kb] = (
                    y[kb * sub:(kb + 1) * sub, e * LANES:(e + 1) * LANES].T.astype(BF16))

    def project_v(xn, j):
        y = _dot(xn, weight_columns(2, j)).astype(BF16)
        for e in range(duo):
            v_new[j * duo + e] = y[:, e * LANES:(e + 1) * LANES]

    @pl.when(i == 0)
    def _():
        kt_scr[:, 0:pad] = jnp.zeros((n_hp, pad, LANES, sub), BF16)
        v_scr[:, 0:pad * sub, :] = jnp.zeros((n_hp, pad * sub, LANES), BF16)

    @pl.when(step == 0)
    def _():
        xn = _rms(x0_ref[...], gain).astype(BF16)

        def first_tile(j, carry):
            project_q(xn, j, 0)
            project_k(xn, j)
            project_v(xn, j)
            return carry

        lax.fori_loop(0, n_hp // duo, first_tile, 0)
        xn_buf[0] = _rms(x1_ref[...], gain).astype(BF16)

    slot = step % 2
    next_slot = (step + 1) % 2
    piece = t // (n_hp // duo)

    lane = lax.broadcasted_iota(jnp.int32, (t, LANES), 1)
    first = lane < SB_HEAD_DIM
    first_grp = lax.broadcasted_iota(jnp.int32, (grp_rows, LANES), 1) < SB_HEAD_DIM
    row = lax.broadcasted_iota(jnp.int32, (sub, sub), 0)
    col = lax.broadcasted_iota(jnp.int32, (sub, sub), 1)
    diag = col < row
    no_weight = jnp.zeros((sub, sub), BF16)
    row_blk = lax.broadcasted_iota(jnp.int32, (t, sub), 0) // sub
    not_first_tile = i > 0

    def append_projected(hp):
        for kb in range(n_rb):
            kt_scr[hp, pad + i * n_rb + kb] = kt_new[hp, kb]
        v_scr[hp, pl.ds(pl.multiple_of((pad + i * n_rb) * sub, sub), t), :] = v_new[hp]

    def split_heads(hp):
        q = q_scr[slot, hp]
        zero = jnp.zeros_like(q)
        return jnp.where(first, q, zero), jnp.where(first, zero, q)

    def band_logits(hp, q_heads):
        parts = []
        tiles = {}
        for p in range(pairs):
            first_sub = pad + i * n_rb + p * grp - band + 1
            kwin = jnp.concatenate([kt_scr[hp, first_sub + s] for s in range(win)], axis=1)
            q_both = jnp.concatenate(
                [q_heads[hd][p * grp_rows:(p + 1) * grp_rows] for hd in range(2)], axis=0)
            y = _dot(q_both, kwin)
            for hd in range(2):
                for rr in range(grp):
                    top = hd * grp_rows + rr * sub
                    for s in range(band):
                        yt = y[top:top + sub, (rr + s) * sub:(rr + s + 1) * sub]
                        part = _log2_one_minus_sigmoid(yt)
                        mask = diag if s == band - 1 else None
                        if p * grp + rr + s < band - 1:
                            mask = not_first_tile if mask is None else mask & not_first_tile
                        if mask is not None:
                            part = jnp.where(mask, part, 0.0)
                        hi, lo = _split_bf16(part)
                        tiles[p, hd, rr, s] = (len(parts), yt, mask)
                        parts.append(jnp.concatenate([hi, lo], axis=1))
        return jnp.concatenate(parts, axis=0), tiles

    def band_weights(hp, s2, tiles, p):
        first_sub = pad + i * n_rb + p * grp - band + 1
        w_rows = []
        carries = []
        for hd in range(2):
            for rr in range(grp):
                r = p * grp + rr
                c = jnp.zeros((sub, sub), F32)
                ws = [None] * band
                for s in reversed(range(band)):
                    index, yt, mask = tiles[p, hd, rr, s]
                    blk = s2[index * sub:(index + 1) * sub]
                    w = jnp.exp2(blk[:, :sub] + c - yt)
                    if mask is not None:
                        w = jnp.where(mask, w, 0.0)
                    ws[s] = w.astype(BF16)
                    c = c + blk[:, sub:]
                c_ref[hp, hd, r * sub:(r + 1) * sub, :] = c
                carries.append(c)
                w_rows.append(jnp.concatenate(
                    [no_weight] * rr + ws + [no_weight] * (grp - 1 - rr), axis=1))
        vwin = v_scr[hp, pl.ds(pl.multiple_of(first_sub * sub, sub), win * sub), :]
        pv = _dot(jnp.concatenate(w_rows, axis=0), vwin)
        rows = slice(p * grp_rows, (p + 1) * grp_rows)
        for hd in range(2):
            acc_ref[hp, hd, rows, :] = pv[hd * grp_rows:(hd + 1) * grp_rows]
        o_scr[hp, rows, :] = jnp.where(first_grp, pv[:grp_rows], pv[grp_rows:]).astype(BF16)
        return jnp.concatenate(
            [jnp.maximum(carries[rr], carries[grp + rr]) for rr in range(grp)], axis=0)

    def largest_open_carry(c, m):
        has_keys = row_blk + i * n_rb - band + 1 - m >= 0
        return jnp.max(jnp.where(has_keys, c, -jnp.inf))

    def extend_band(hp, q_heads, m):
        parts = []
        tiles = {}
        starts = []
        for p in range(pairs):
            first_sub = i * n_rb + p * grp - band + 1 - m
            start = jnp.maximum(pad + first_sub, 0)
            starts.append(start)
            kwin = jnp.concatenate([kt_scr[hp, start + s] for s in range(grp)], axis=1)
            q_both = jnp.concatenate(
                [q_heads[hd][p * grp_rows:(p + 1) * grp_rows] for hd in range(2)], axis=0)
            y = _dot(q_both, kwin)
            for hd in range(2):
                for rr in range(grp):
                    top = hd * grp_rows + rr * sub
                    yt = y[top:top + sub, rr * sub:(rr + 1) * sub]
                    in_sequence = first_sub + rr >= 0
                    part = jnp.where(in_sequence, _log2_one_minus_sigmoid(yt), 0.0)
                    hi, lo = _split_bf16(part)
                    tiles[p, hd, rr] = (len(parts), yt, in_sequence)
                    parts.append(jnp.concatenate([hi, lo], axis=1))
        s2 = _dot(jnp.concatenate(parts, axis=0), u)
        for p in range(pairs):
            w_rows = []
            for hd in range(2):
                for rr in range(grp):
                    rows = slice((p * grp + rr) * sub, (p * grp + rr + 1) * sub)
                    index, yt, in_sequence = tiles[p, hd, rr]
                    blk = s2[index * sub:(index + 1) * sub]
                    c = c_ref[hp, hd, rows, :]
                    w = jnp.where(in_sequence, jnp.exp2(blk[:, :sub] + c - yt), 0.0).astype(BF16)
                    c_ref[hp, hd, rows, :] = c + blk[:, sub:]
                    w_rows.append(jnp.concatenate(
                        [no_weight] * rr + [w] + [no_weight] * (grp - 1 - rr), axis=1))
            vwin = v_scr[hp, pl.ds(pl.multiple_of(starts[p] * sub, sub), grp * sub), :]
            pv = _dot(jnp.concatenate(w_rows, axis=0), vwin)
            rows = slice(p * grp_rows, (p + 1) * grp_rows)
            for hd in range(2):
                acc_ref[hp, hd, rows, :] += pv[hd * grp_rows:(hd + 1) * grp_rows]

    def further_left(hp):
        q_heads = split_heads(hp)
        m_last = i * n_rb + n_rb - band

        def more_keys_matter(state):
            m, c_max = state
            return jnp.logical_and(m <= m_last, c_max > SB_F32_EXP2_ZERO)

        def one_more_sub_block(state):
            m, _ = state
            extend_band(hp, q_heads, m)
            return m + 1, largest_open_carry(c_ref[hp], m + 1)

        lax.while_loop(more_keys_matter, one_more_sub_block,
                       (1, largest_open_carry(c_ref[hp], 1)))
        o_scr[hp] = jnp.where(first, acc_ref[hp, 0], acc_ref[hp, 1]).astype(BF16)

    def per_head_pair_duo(j, largest_carry):
        hps = [j * duo + e for e in range(duo)]
        xn_next = xn_buf[slot]
        for hp in hps:
            append_projected(hp)
        q_heads = [split_heads(hp) for hp in hps]
        stacked = [band_logits(hp, q_heads[e]) for e, hp in enumerate(hps)]
        project_q(xn_next, j, next_slot)
        project_k(xn_next, j)
        s2 = []
        for e in range(duo):
            s2.append(_dot(stacked[e][0], u))
            if e == 0:
                project_v(xn_next, j)
        for e, hp in enumerate(hps):
            carry = [band_weights(hp, s2[e], stacked[e][1], p) for p in range(pairs)]
            largest_carry = jnp.maximum(largest_carry, jnp.concatenate(carry, axis=0))
        rows = pl.ds(pl.multiple_of(j * piece, piece), piece)
        xn_buf[next_slot, rows, :] = _rms(x2_ref[rows, :], gain).astype(BF16)
        return largest_carry

    largest_carry = lax.fori_loop(0, n_hp // duo, per_head_pair_duo,
                                  jnp.full((t, sub), -jnp.inf, F32))

    @pl.when(largest_open_carry(largest_carry, 1) > SB_F32_EXP2_ZERO)
    def _():
        def visit(hp, carry):
            further_left(hp)
            return carry

        lax.fori_loop(0, n_hp, visit, 0)

    o_ref[...] = jnp.concatenate([o_scr[hp] for hp in range(n_hp)], axis=1)


def _sb_mixer(h, gain, w_qkv, index):
    b, s, d = h.shape
    t = SB_TILE
    n_hp = d // LANES
    duo = SB_HP_PER_STEP
    n_tiles = s // t
    n_subs = SB_PAD_SUBS + s // SB_SUB

    def tile_after_next(bi, i):
        tile = jnp.minimum(bi * n_tiles + i + 2, b * n_tiles - 1)
        return tile // n_tiles, tile % n_tiles, 0

    return pl.pallas_call(
        _sb_layer_kernel,
        grid=(b, n_tiles),
        in_specs=[
            pl.BlockSpec((None, t, d), lambda bi, i: (0, 0, 0), pipeline_mode=pl.Buffered(1)),
            pl.BlockSpec((None, t, d), lambda bi, i: (0, 1, 0), pipeline_mode=pl.Buffered(1)),
            pl.BlockSpec((None, t, d), tile_after_next),
            _resident((1, d)),
            _layer_slice((d, 3 * d), index),
            _resident((2 * SB_SUB, 2 * SB_SUB)),
        ],
        out_specs=pl.BlockSpec((None, t, d), lambda bi, i: (bi, i, 0)),
        out_shape=jax.ShapeDtypeStruct((b, s, d), BF16),
        scratch_shapes=[
            pltpu.VMEM((2, n_hp, t, LANES), BF16),
            pltpu.VMEM((n_hp, n_subs, LANES, SB_SUB), BF16),
            pltpu.VMEM((n_hp, n_subs * SB_SUB, LANES), BF16),
            pltpu.VMEM((n_hp, t // SB_SUB, LANES, SB_SUB), BF16),
            pltpu.VMEM((n_hp, t, LANES), BF16),
            pltpu.VMEM((2, t, d), BF16),
            pltpu.VMEM((n_hp, t, LANES), BF16),
            pltpu.VMEM((n_hp, 2, t, LANES), F32),
            pltpu.VMEM((n_hp, 2, t, SB_SUB), F32),
        ],
        compiler_params=_params(("arbitrary", "arbitrary")),
        name="sb_layer",
    )(h, h, h, gain, w_qkv, _sb_suffix_matrix())


def _conv_kernel(x_ref, xp_ref, g_ref, w_ref, cw_ref, a_ref, hbuf_ref):
    d = D_MODEL
    tm = TOKEN_TILE
    i = pl.program_id(1)
    gain = g_ref[...]
    xn = _rms(x_ref[...], gain).astype(BF16)
    xpn = _rms(xp_ref[...], gain).astype(BF16)
    w_c = w_ref[:, d:2 * d].astype(BF16)
    w_u = w_ref[:, 2 * d:3 * d].astype(BF16)
    hbuf_ref[SUBLANES:, :] = _dot(xn, w_c) * _dot(xn, w_u)
    h_prev = _dot(xpn, w_c) * _dot(xpn, w_u)
    hbuf_ref[0:SUBLANES, :] = jnp.where(i > 0, h_prev, 0.0)
    conv = cw_ref[0:1, :] * hbuf_ref[pl.ds(SUBLANES - 2, tm), :]
    conv = conv + cw_ref[1:2, :] * hbuf_ref[pl.ds(SUBLANES - 1, tm), :]
    conv = conv + cw_ref[2:3, :] * hbuf_ref[pl.ds(SUBLANES, tm), :]
    a_ref[...] = (_dot(xn, w_ref[:, 0:d].astype(BF16)) * conv).astype(BF16)


def _conv_mixer(h, gain, w_in, conv_w, index):
    b, s, d = h.shape
    tm = TOKEN_TILE
    rows_per_tile = tm // SUBLANES
    return pl.pallas_call(
        _conv_kernel,
        grid=(b, s // tm),
        in_specs=[
            pl.BlockSpec((None, tm, d), lambda bi, i: (bi, i, 0)),
            pl.BlockSpec((None, SUBLANES, d),
                         lambda bi, i: (bi, jnp.maximum(i * rows_per_tile - 1, 0), 0)),
            _resident((1, d)),
            _layer_slice((d, 3 * d), index),
            _layer_slice((CONV_WIDTH, d), index),
        ],
        out_specs=pl.BlockSpec((None, tm, d), lambda bi, i: (bi, i, 0)),
        out_shape=jax.ShapeDtypeStruct((b, s, d), BF16),
        scratch_shapes=[pltpu.VMEM((tm + SUBLANES, d), F32)],
        compiler_params=_params(("parallel", "parallel")),
        name="conv_mixer",
    )(h, h, gain, w_in, conv_w)


def _gla_proj_kernel(x_ref, g_ref, w_ref, wa_ref, wgu_ref, bg_ref,
                     q_ref, kt_ref, v_ref, og_ref, lg_ref, lgt_ref):
    dk, dv = GLA_DK, GLA_DV
    xn = _rms(x_ref[...], g_ref[...]).astype(BF16)
    a_low = _dot(xn, wa_ref[...]).astype(BF16)
    pre = _dot(a_low, wgu_ref[...]) + bg_ref[...]
    k = _dot(xn, w_ref[:, dk:2 * dk].astype(BF16))
    lg = _log_sigmoid(pre) * (1.0 / GLA_GATE_NORMALIZER)
    lg_ref[...] = lg
    lgt_ref[...] = lg.T
    kt_ref[...] = k.T.astype(BF16)
    q_ref[...] = _dot(xn, w_ref[:, 0:dk].astype(BF16)).astype(BF16)
    v_ref[...] = _dot(xn, w_ref[:, 2 * dk:2 * dk + dv].astype(BF16)).astype(BF16)
    og_ref[...] = _dot(xn, w_ref[:, 2 * dk + dv:2 * dk + 2 * dv].astype(BF16)).astype(BF16)


def _gla_proj(h, gain, w_in, index, w_a, w_gu, b_gate):
    b, s, d = h.shape
    tm = TOKEN_TILE
    dk, dv = GLA_DK, GLA_DV
    row = lambda bi, i: (bi, i, 0)
    colmajor = lambda bi, i: (bi, 0, i)
    return pl.pallas_call(
        _gla_proj_kernel,
        grid=(b, s // tm),
        in_specs=[
            pl.BlockSpec((None, tm, d), row),
            _resident((1, d)),
            _layer_slice((d, 2 * dk + 2 * dv), index),
            _resident((d, LANES)),
            _resident((LANES, dk)),
            _resident((1, dk)),
        ],
        out_specs=[
            pl.BlockSpec((None, tm, dk), row),
            pl.BlockSpec((None, dk, tm), colmajor),
            pl.BlockSpec((None, tm, dv), row),
            pl.BlockSpec((None, tm, dv), row),
            pl.BlockSpec((None, tm, dk), row),
            pl.BlockSpec((None, dk, tm), colmajor),
        ],
        out_shape=[
            jax.ShapeDtypeStruct((b, s, dk), BF16),
            jax.ShapeDtypeStruct((b, dk, s), BF16),
            jax.ShapeDtypeStruct((b, s, dv), BF16),
            jax.ShapeDtypeStruct((b, s, dv), BF16),
            jax.ShapeDtypeStruct((b, s, dk), F32),
            jax.ShapeDtypeStruct((b, dk, s), F32),
        ],
        compiler_params=_params(("parallel", "parallel")),
        name="gla_proj",
    )(h, gain, w_in, w_a, w_gu, b_gate)


def _gla_constants():
    c = GLA_TILE
    i = np.arange(c)[:, None]
    m = np.arange(c)[None, :]
    fwd, rev, masks = [], [], []
    for level in range(1, GLA_LEVELS + 1):
        n = 1 << level
        same = (i // n) == (m // n)
        fwd.append((same & (m <= i)).astype(np.float32))
        rev.append((same & (m > i)).astype(np.float32).T)
    rev.append(np.ones((c, c), np.float32))
    for level in range(GLA_LEVELS):
        n = 1 << level
        masks.append(((i // (2 * n) == m // (2 * n)) & ((i // n) % 2 == 1)
                      & ((m // n) % 2 == 0)).astype(np.float32))
    masks.append((i == m).astype(np.float32))
    fwd = np.concatenate(fwd, axis=0)
    rev = np.concatenate(rev, axis=1)
    fwd2 = np.concatenate([fwd, fwd], axis=1)
    rev2 = np.concatenate([rev, rev], axis=0)
    return (jnp.asarray(fwd2, dtype=BF16), jnp.asarray(rev2, dtype=BF16),
            jnp.asarray(np.stack(masks), dtype=F32))


def _gla_chunk_kernel(q_ref, kt_ref, v_ref, og_ref, lg_ref, lgt_ref, fwd_ref, rev_ref,
                      mask_ref, hn_ref, a_ref, state_ref):
    c = GLA_TILE
    n_ch = GLA_STEP_CHUNKS
    dk = GLA_DK
    dkh, dvh = GLA_DK_HEAD, GLA_DV_HEAD
    n_lv = GLA_LEVELS

    @pl.when(pl.program_id(1) == 0)
    def _():
        state_ref[...] = jnp.zeros_like(state_ref)

    lg = lg_ref[...]
    hi, lo = _split_bf16(lg)
    g_cat = jnp.concatenate(
        [jnp.concatenate([hi[ch * c:(ch + 1) * c], lo[ch * c:(ch + 1) * c]], axis=0)
         for ch in range(n_ch)], axis=1)
    fwd_all = _dot(fwd_ref[...], g_cat)
    hi_t, lo_t = _split_bf16(lgt_ref[...])
    gt_cat = jnp.concatenate(
        [jnp.concatenate([hi_t[:, ch * c:(ch + 1) * c], lo_t[:, ch * c:(ch + 1) * c]], axis=1)
         for ch in range(n_ch)], axis=0)
    rev_all = _dot(gt_cat, rev_ref[...])

    for hd in range(GLA_HEADS):
        ks = slice(hd * dkh, (hd + 1) * dkh)
        vs = slice(hd * dvh, (hd + 1) * dvh)
        state = state_ref[hd]
        for ch in range(n_ch):
            rows = slice(ch * c, (ch + 1) * c)
            kcols = slice(ch * dk + hd * dkh, ch * dk + (hd + 1) * dkh)
            q = q_ref[rows, ks].astype(F32) * GLA_SCALE
            kt = kt_ref[ks, rows].astype(F32)
            v = v_ref[rows, vs]
            kt_bf = kt.astype(BF16)
            scores = _dot(q.astype(BF16), kt_bf) * mask_ref[n_lv]
            qd = q * jnp.exp(lg[rows, ks])
            scores = scores + _dot(qd.astype(BF16), kt_bf) * mask_ref[0]
            for lv in range(1, n_lv):
                qd = q * jnp.exp(fwd_all[(lv - 1) * c:lv * c, kcols])
                kd = kt * jnp.exp(rev_all[kcols, (lv - 1) * c:lv * c])
                scores = scores + _dot(qd.astype(BF16), kd.astype(BF16)) * mask_ref[lv]
            q_cum = q * jnp.exp(fwd_all[(n_lv - 1) * c:n_lv * c, kcols])
            out = _dot(jnp.concatenate([q_cum.astype(BF16), scores.astype(BF16)], axis=1),
                       jnp.concatenate([state.astype(BF16), v], axis=0))
            k_rest = kt * jnp.exp(rev_all[kcols, (n_lv - 1) * c:n_lv * c])
            keep = jnp.exp(rev_all[kcols, n_lv * c:(n_lv + 1) * c])
            keep = jnp.concatenate([keep] * (dvh // c), axis=1)
            state = keep * state + _dot(k_rest.astype(BF16), v)
            o = _rms(out, hn_ref[:, vs])
            og = og_ref[rows, vs].astype(F32)
            a_ref[rows, vs] = (o * (og * (1.0 / (1.0 + jnp.exp(-og))))).astype(BF16)
        state_ref[hd] = state


def _gla_chunk(q, kt, v, og, lg, lgt, head_norm):
    b, s, dk = q.shape
    dv = v.shape[-1]
    c = GLA_TILE * GLA_STEP_CHUNKS
    fwd, rev, masks = _gla_constants()
    row = lambda bi, i: (bi, i, 0)
    colmajor = lambda bi, i: (bi, 0, i)
    return pl.pallas_call(
        _gla_chunk_kernel,
        grid=(b, s // c),
        in_specs=[
            pl.BlockSpec((None, c, dk), row),
            pl.BlockSpec((None, dk, c), colmajor),
            pl.BlockSpec((None, c, dv), row),
            pl.BlockSpec((None, c, dv), row),
            pl.BlockSpec((None, c, dk), row),
            pl.BlockSpec((None, dk, c), colmajor),
            _resident(fwd.shape),
            _resident(rev.shape),
            _resident(masks.shape),
            _resident((1, dv)),
        ],
        out_specs=pl.BlockSpec((None, c, dv), row),
        out_shape=jax.ShapeDtypeStruct((b, s, dv), BF16),
        scratch_shapes=[pltpu.VMEM((GLA_HEADS, GLA_DK_HEAD, GLA_DV_HEAD), F32)],
        compiler_params=_params(("parallel", "arbitrary")),
        name="gla_chunk",
    )(q, kt, v, og, lg, lgt, fwd, rev, masks, head_norm)


def _post_kernel(a_ref, h_ref, wo_ref, g_ref, wup_ref, wdn_ref, o_ref, acc_ref):
    half = TOKEN_TILE // 2
    halves = (slice(0, half), slice(half, TOKEN_TILE))
    w_o = wo_ref[...].astype(BF16)
    h1 = []
    xn = []
    for rows in halves:
        mixed = _dot(a_ref[rows, :], w_o)
        h1.append(h_ref[rows, :] + _rms(mixed, g_ref[1:2, :]))
        xn.append(_rms(h1[-1], g_ref[2:3, :]).astype(BF16))
    xn_full = jnp.concatenate(xn, axis=0)
    n_chunks = D_FF // FF_CHUNK
    for ci in range(n_chunks):
        cols = slice(ci * FF_CHUNK, (ci + 1) * FF_CHUNK)
        w_up = wup_ref[:, cols].astype(BF16)
        w_down = wdn_ref[cols, :].astype(BF16)
        if ci == 0:
            up = jnp.concatenate([_dot(part, w_up) for part in xn], axis=0)
        else:
            up = _dot(xn_full, w_up)
        up = jnp.maximum(up, 0.0)
        act = (up * up).astype(BF16)
        if ci == 0:
            acc_ref[...] = _dot(act, w_down)
        elif ci < n_chunks - 1:
            acc_ref[...] += _dot(act, w_down)
        else:
            for hi, rows in enumerate(halves):
                ffn = acc_ref[rows, :] + _dot(act[rows, :], w_down)
                o_ref[rows, :] = h1[hi] + _rms(ffn, g_ref[3:4, :])


def _layer_slice(shape, index):
    zeros = (0,) * len(shape)
    return pl.BlockSpec((None,) + tuple(shape), lambda *_: (index,) + zeros,
                        pipeline_mode=pl.Buffered(1))


def _post(a, h, w_o, mixer_index, gains, w_up, w_down, layer):
    b, s, d = h.shape
    tm = TOKEN_TILE
    row = lambda bi, i: (bi, i, 0)
    return pl.pallas_call(
        _post_kernel,
        grid=(b, s // tm),
        in_specs=[
            pl.BlockSpec((None, tm, d), row),
            pl.BlockSpec((None, tm, d), row),
            _layer_slice((d, d), mixer_index),
            _layer_slice((4, d), layer),
            _layer_slice((d, D_FF), layer),
            _layer_slice((D_FF, d), layer),
        ],
        out_specs=pl.BlockSpec((None, tm, d), row),
        out_shape=jax.ShapeDtypeStruct((b, s, d), F32),
        scratch_shapes=[pltpu.VMEM((tm, d), F32)],
        compiler_params=_params(("parallel", "parallel")),
        name="post_ffn",
    )(a, h, w_o, gains, w_up, w_down)


def kernel(x, norm_gains, sb_w_qkv, sb_w_o, conv_w_in, conv_w, conv_w_out, gla_w_in,
           gla_w_gate_up, gla_b_gate, gla_head_norm, gla_w_o, ffn_w_up, ffn_w_down):
    depth = norm_gains.shape[0]
    sb_w_qkv_bf16 = sb_w_qkv.astype(BF16)
    h = x
    for i in range(depth):
        kind, j = i % 3, i // 3
        gains = norm_gains[i]
        pre_gain = gains[0:1]
        if kind == 0:
            a = _sb_mixer(h, pre_gain, sb_w_qkv_bf16, j)
            w_o = sb_w_o
        elif kind == 1:
            a = _conv_mixer(h, pre_gain, conv_w_in, conv_w, j)
            w_o = conv_w_out
        else:
            n_main = 2 * GLA_DK + 2 * GLA_DV
            w_in = gla_w_in[j]
            w_a = jnp.pad(w_in[:, n_main:], ((0, 0), (0, LANES - GLA_GATE_RANK))).astype(BF16)
            w_gu = jnp.pad(gla_w_gate_up[j], ((0, LANES - GLA_GATE_RANK), (0, 0))).astype(BF16)
            q, kt, v, og, lg, lgt = _gla_proj(h, pre_gain, gla_w_in, j, w_a, w_gu,
                                              gla_b_gate[j][None, :])
            a = _gla_chunk(q, kt, v, og, lg, lgt, gla_head_norm[j].reshape(1, GLA_DV))
            w_o = gla_w_o
        h = _post(a, h, w_o, j, norm_gains, ffn_w_up, ffn_w_down, i)
    return h
```

```python
import math

import numpy as np
import jax
import jax.numpy as jnp
from jax import lax
from jax.experimental import pallas as pl
from jax.experimental.pallas import tpu as pltpu

F32 = jnp.float32
BF16 = jnp.bfloat16

D_MODEL = 1024
D_FF = 4 * D_MODEL
RMS_EPS = 1e-6

SB_HEADS = 16
SB_HEAD_DIM = D_MODEL // SB_HEADS
LOG2E = math.log2(math.e)
SB_Q_SCALE = -(SB_HEAD_DIM ** -0.5) * LOG2E
CONV_WIDTH = 3
GLA_HEADS = 4
GLA_DK = D_MODEL // 2
GLA_DV = D_MODEL
GLA_DK_HEAD = GLA_DK // GLA_HEADS
GLA_DV_HEAD = GLA_DV // GLA_HEADS
GLA_GATE_RANK = 16
GLA_GATE_NORMALIZER = 16.0
GLA_SCALE = GLA_DK_HEAD ** -0.5

LANES = 128
SUBLANES = 8
VMEM_LIMIT_BYTES = 58 * 1024 * 1024

TOKEN_TILE = 512
FF_CHUNK = 1024
SB_TILE = 512
SB_SUB = 128
SB_BAND_SUBS = 3
SB_GROUP = 2
SB_PAD_SUBS = SB_BAND_SUBS - 1
SB_HP_PER_STEP = 2
SB_F32_EXP2_ZERO = -152.0
GLA_TILE = 128
GLA_STEP_CHUNKS = 2
GLA_LEVELS = 7


def _rms(x, gain):
    ms = jnp.mean(x * x, axis=-1, keepdims=True)
    return x * lax.rsqrt(ms + RMS_EPS) * gain


def _neg_abs(z):
    bits = lax.bitcast_convert_type(z, jnp.uint32) | jnp.uint32(0x80000000)
    return lax.bitcast_convert_type(bits, F32)


def _log_sigmoid(z):
    return jnp.minimum(z, 0.0) - jnp.log(1.0 + jnp.exp(_neg_abs(z)))


def _log2_one_minus_sigmoid(y):
    return jnp.minimum(y, 0.0) - jnp.log(1.0 + jnp.exp2(_neg_abs(y))) * LOG2E


def _split_bf16(x):
    hi = x.astype(BF16)
    lo = (x - hi.astype(F32)).astype(BF16)
    return hi, lo


def _dot(a, b):
    return jnp.dot(a, b, preferred_element_type=F32)


def _resident(shape):
    zeros = (0,) * len(shape)
    return pl.BlockSpec(shape, lambda *_: zeros, pipeline_mode=pl.Buffered(1))


def _params(semantics):
    return pltpu.CompilerParams(dimension_semantics=semantics,
                                vmem_limit_bytes=VMEM_LIMIT_BYTES)


def _sb_suffix_matrix():
    t = SB_SUB
    m = np.arange(t)[:, None]
    j = np.arange(t)[None, :]
    half = np.concatenate([(m >= j).astype(np.float32), np.ones((t, t), np.float32)], axis=1)
    return jnp.asarray(np.concatenate([half, half], axis=0), dtype=BF16)


def _sb_layer_kernel(x0_ref, x1_ref, x2_ref, g_ref, w_ref, u_ref, o_ref,
                     q_scr, kt_scr, v_scr, kt_new, v_new, xn_buf, o_scr, acc_ref, c_ref):
    t = SB_TILE
    sub = SB_SUB
    n_rb = t // sub
    band = SB_BAND_SUBS
    pad = SB_PAD_SUBS
    grp = SB_GROUP
    win = band + grp - 1
    pairs = n_rb // grp
    grp_rows = grp * sub
    n_hp = D_MODEL // LANES
    duo = SB_HP_PER_STEP
    i = pl.program_id(1)
    step = pl.program_id(0) * pl.num_programs(1) + i
    gain = g_ref[...]
    u = u_ref[...]

    def weight_columns(which, j):
        width = duo * LANES
        return w_ref[:, pl.ds(pl.multiple_of(which * D_MODEL + j * width, width), width)]

    def project_q(xn, j, slot):
        y = _dot(xn, weight_columns(0, j)) * SB_Q_SCALE
        for e in range(duo):
            q_scr[slot, j * duo + e] = y[:, e * LANES:(e + 1) * LANES].astype(BF16)

    def project_k(xn, j):
        y = _dot(xn, weight_columns(1, j))
        for e in range(duo):
            for kb in range(n_rb):
                kt_new[j * duo + e, kb] = (
                    y[kb * sub:(kb + 1) * sub, e * LANES:(e + 1) * LANES].T.astype(BF16))

    def project_v(xn, j):
        y = _dot(xn, weight_columns(2, j)).astype(BF16)
        for e in range(duo):
            v_new[j * duo + e] = y[:, e * LANES:(e + 1) * LANES]

    @pl.when(i == 0)
    def _():
        kt_scr[:, 0:pad] = jnp.zeros((n_hp, pad, LANES, sub), BF16)
        v_scr[:, 0:pad * sub, :] = jnp.zeros((n_hp, pad * sub, LANES), BF16)

    @pl.when(step == 0)
    def _():
        xn = _rms(x0_ref[...], gain).astype(BF16)

        def first_tile(j, carry):
            project_q(xn, j, 0)
            project_k(xn, j)
            project_v(xn, j)
            return carry

        lax.fori_loop(0, n_hp // duo, first_tile, 0)
        xn_buf[0] = _rms(x1_ref[...], gain).astype(BF16)

    slot = step % 2
    next_slot = (step + 1) % 2
    piece = t // (n_hp // duo)

    lane = lax.broadcasted_iota(jnp.int32, (t, LANES), 1)
    first = lane < SB_HEAD_DIM
    first_grp = lax.broadcasted_iota(jnp.int32, (grp_rows, LANES), 1) < SB_HEAD_DIM
    row = lax.broadcasted_iota(jnp.int32, (sub, sub), 0)
    col = lax.broadcasted_iota(jnp.int32, (sub, sub), 1)
    diag = col < row
    no_weight = jnp.zeros((sub, sub), BF16)
    row_blk = lax.broadcasted_iota(jnp.int32, (t, sub), 0) // sub
    not_first_tile = i > 0

    def append_projected(hp):
        for kb in range(n_rb):
            kt_scr[hp, pad + i * n_rb + kb] = kt_new[hp, kb]
        v_scr[hp, pl.ds(pl.multiple_of((pad + i * n_rb) * sub, sub), t), :] = v_new[hp]

    def split_heads(hp):
        q = q_scr[slot, hp]
        zero = jnp.zeros_like(q)
        return jnp.where(first, q, zero), jnp.where(first, zero, q)

    def band_logits(hp, q_heads):
        parts = []
        tiles = {}
        for p in range(pairs):
            first_sub = pad + i * n_rb + p * grp - band + 1
            kwin = jnp.concatenate([kt_scr[hp, first_sub + s] for s in range(win)], axis=1)
            q_both = jnp.concatenate(
                [q_heads[hd][p * grp_rows:(p + 1) * grp_rows] for hd in range(2)], axis=0)
            y = _dot(q_both, kwin)
            for hd in range(2):
                for rr in range(grp):
                    top = hd * grp_rows + rr * sub
                    for s in range(band):
                        yt = y[top:top + sub, (rr + s) * sub:(rr + s + 1) * sub]
                        part = _log2_one_minus_sigmoid(yt)
                        mask = diag if s == band - 1 else None
                        if p * grp + rr + s < band - 1:
                            mask = not_first_tile if mask is None else mask & not_first_tile
                        if mask is not None:
                            part = jnp.where(mask, part, 0.0)
                        hi, lo = _split_bf16(part)
                        tiles[p, hd, rr, s] = (len(parts), yt, mask)
                        parts.append(jnp.concatenate([hi, lo], axis=1))
        return jnp.concatenate(parts, axis=0), tiles

    def band_weights(hp, s2, tiles, p):
        first_sub = pad + i * n_rb + p * grp - band + 1
        w_rows = []
        carries = []
        for hd in range(2):
            for rr in range(grp):
                r = p * grp + rr
                c = jnp.zeros((sub, sub), F32)
                ws = [None] * band
                for s in reversed(range(band)):
                    index, yt, mask = tiles[p, hd, rr, s]
                    blk = s2[index * sub:(index + 1) * sub]
                    w = jnp.exp2(blk[:, :sub] + c - yt)
                    if mask is not None:
                        w = jnp.where(mask, w, 0.0)
                    ws[s] = w.astype(BF16)
                    c = c + blk[:, sub:]
                c_ref[hp, hd, r * sub:(r + 1) * sub, :] = c
                carries.append(c)
                w_rows.append(jnp.concatenate(
                    [no_weight] * rr + ws + [no_weight] * (grp - 1 - rr), axis=1))
        vwin = v_scr[hp, pl.ds(pl.multiple_of(first_sub * sub, sub), win * sub), :]
        pv = _dot(jnp.concatenate(w_rows, axis=0), vwin)
        rows = slice(p * grp_rows, (p + 1) * grp_rows)
        for hd in range(2):
            acc_ref[hp, hd, rows, :] = pv[hd * grp_rows:(hd + 1) * grp_rows]
        o_scr[hp, rows, :] = jnp.where(first_grp, pv[:grp_rows], pv[grp_rows:]).astype(BF16)
        return jnp.concatenate(
            [jnp.maximum(carries[rr], carries[grp + rr]) for rr in range(grp)], axis=0)

    def largest_open_carry(c, m):
        has_keys = row_blk + i * n_rb - band + 1 - m >= 0
        return jnp.max(jnp.where(has_keys, c, -jnp.inf))

    def extend_band(hp, q_heads, m):
        parts = []
        tiles = {}
        starts = []
        for p in range(pairs):
            first_sub = i * n_rb + p * grp - band + 1 - m
            start = jnp.maximum(pad + first_sub, 0)
            starts.append(start)
            kwin = jnp.concatenate([kt_scr[hp, start + s] for s in range(grp)], axis=1)
            q_both = jnp.concatenate(
                [q_heads[hd][p * grp_rows:(p + 1) * grp_rows] for hd in range(2)], axis=0)
            y = _dot(q_both, kwin)
            for hd in range(2):
                for rr in range(grp):
                    top = hd * grp_rows + rr * sub
                    yt = y[top:top + sub, rr * sub:(rr + 1) * sub]
                    in_sequence = first_sub + rr >= 0
                    part = jnp.where(in_sequence, _log2_one_minus_sigmoid(yt), 0.0)
                    hi, lo = _split_bf16(part)
                    tiles[p, hd, rr] = (len(parts), yt, in_sequence)
                    parts.append(jnp.concatenate([hi, lo], axis=1))
        s2 = _dot(jnp.concatenate(parts, axis=0), u)
        for p in range(pairs):
            w_rows = []
            for hd in range(2):
                for rr in range(grp):
                    rows = slice((p * grp + rr) * sub, (p * grp + rr + 1) * sub)
                    index, yt, in_sequence = tiles[p, hd, rr]
                    blk = s2[index * sub:(index + 1) * sub]
                    c = c_ref[hp, hd, rows, :]
                    w = jnp.where(in_sequence, jnp.exp2(blk[:, :sub] + c - yt), 0.0).astype(BF16)
                    c_ref[hp, hd, rows, :] = c + blk[:, sub:]
                    w_rows.append(jnp.concatenate(
                        [no_weight] * rr + [w] + [no_weight] * (grp - 1 - rr), axis=1))
            vwin = v_scr[hp, pl.ds(pl.multiple_of(starts[p] * sub, sub), grp * sub), :]
            pv = _dot(jnp.concatenate(w_rows, axis=0), vwin)
            rows = slice(p * grp_rows, (p + 1) * grp_rows)
            for hd in range(2):
                acc_ref[hp, hd, rows, :] += pv[hd * grp_rows:(hd + 1) * grp_rows]

    def further_left(hp):
        q_heads = split_heads(hp)
        m_last = i * n_rb + n_rb - band

        def more_keys_matter(state):
            m, c_max = state
            return jnp.logical_and(m <= m_last, c_max > SB_F32_EXP2_ZERO)

        def one_more_sub_block(state):
            m, _ = state
            extend_band(hp, q_heads, m)
            return m + 1, largest_open_carry(c_ref[hp], m + 1)

        lax.while_loop(more_keys_matter, one_more_sub_block,
                       (1, largest_open_carry(c_ref[hp], 1)))
        o_scr[hp] = jnp.where(first, acc_ref[hp, 0], acc_ref[hp, 1]).astype(BF16)

    def per_head_pair_duo(j, largest_carry):
        hps = [j * duo + e for e in range(duo)]
        xn_next = xn_buf[slot]
        for hp in hps:
            append_projected(hp)
        q_heads = [split_heads(hp) for hp in hps]
        stacked = [band_logits(hp, q_heads[e]) for e, hp in enumerate(hps)]
        project_q(xn_next, j, next_slot)
        project_k(xn_next, j)
        s2 = []
        for e in range(duo):
            s2.append(_dot(stacked[e][0], u))
            if e == 0:
                project_v(xn_next, j)
        for e, hp in enumerate(hps):
            carry = [band_weights(hp, s2[e], stacked[e][1], p) for p in range(pairs)]
            largest_carry = jnp.maximum(largest_carry, jnp.concatenate(carry, axis=0))
        rows = pl.ds(pl.multiple_of(j * piece, piece), piece)
        xn_buf[next_slot, rows, :] = _rms(x2_ref[rows, :], gain).astype(BF16)
        return largest_carry

    largest_carry = lax.fori_loop(0, n_hp // duo, per_head_pair_duo,
                                  jnp.full((t, sub), -jnp.inf, F32))

    @pl.when(largest_open_carry(largest_carry, 1) > SB_F32_EXP2_ZERO)
    def _():
        def visit(hp, carry):
            further_left(hp)
            return carry

        lax.fori_loop(0, n_hp, visit, 0)

    o_ref[...] = jnp.concatenate([o_scr[hp] for hp in range(n_hp)], axis=1)


def _sb_mixer(h, gain, w_qkv):
    b, s, d = h.shape
    t = SB_TILE
    n_hp = d // LANES
    duo = SB_HP_PER_STEP
    n_tiles = s // t
    n_subs = SB_PAD_SUBS + s // SB_SUB

    def tile_after_next(bi, i):
        tile = jnp.minimum(bi * n_tiles + i + 2, b * n_tiles - 1)
        return tile // n_tiles, tile % n_tiles, 0

    return pl.pallas_call(
        _sb_layer_kernel,
        grid=(b, n_tiles),
        in_specs=[
            pl.BlockSpec((None, t, d), lambda bi, i: (0, 0, 0), pipeline_mode=pl.Buffered(1)),
            pl.BlockSpec((None, t, d), lambda bi, i: (0, 1, 0), pipeline_mode=pl.Buffered(1)),
            pl.BlockSpec((None, t, d), tile_after_next),
            _resident((1, d)),
            _resident((d, 3 * d)),
            _resident((2 * SB_SUB, 2 * SB_SUB)),
        ],
        out_specs=pl.BlockSpec((None, t, d), lambda bi, i: (bi, i, 0)),
        out_shape=jax.ShapeDtypeStruct((b, s, d), BF16),
        scratch_shapes=[
            pltpu.VMEM((2, n_hp, t, LANES), BF16),
            pltpu.VMEM((n_hp, n_subs, LANES, SB_SUB), BF16),
            pltpu.VMEM((n_hp, n_subs * SB_SUB, LANES), BF16),
            pltpu.VMEM((n_hp, t // SB_SUB, LANES, SB_SUB), BF16),
            pltpu.VMEM((n_hp, t, LANES), BF16),
            pltpu.VMEM((2, t, d), BF16),
            pltpu.VMEM((n_hp, t, LANES), BF16),
            pltpu.VMEM((n_hp, 2, t, LANES), F32),
            pltpu.VMEM((n_hp, 2, t, SB_SUB), F32),
        ],
        compiler_params=_params(("arbitrary", "arbitrary")),
        name="sb_layer",
    )(h, h, h, gain, w_qkv, _sb_suffix_matrix())


def _conv_kernel(x_ref, xp_ref, g_ref, w_ref, cw_ref, a_ref, hbuf_ref):
    d = D_MODEL
    tm = TOKEN_TILE
    i = pl.program_id(1)
    gain = g_ref[...]
    xn = _rms(x_ref[...], gain).astype(BF16)
    xpn = _rms(xp_ref[...], gain).astype(BF16)
    w_c = w_ref[:, d:2 * d].astype(BF16)
    w_u = w_ref[:, 2 * d:3 * d].astype(BF16)
    hbuf_ref[SUBLANES:, :] = _dot(xn, w_c) * _dot(xn, w_u)
    h_prev = _dot(xpn, w_c) * _dot(xpn, w_u)
    hbuf_ref[0:SUBLANES, :] = jnp.where(i > 0, h_prev, 0.0)
    conv = cw_ref[0:1, :] * hbuf_ref[pl.ds(SUBLANES - 2, tm), :]
    conv = conv + cw_ref[1:2, :] * hbuf_ref[pl.ds(SUBLANES - 1, tm), :]
    conv = conv + cw_ref[2:3, :] * hbuf_ref[pl.ds(SUBLANES, tm), :]
    a_ref[...] = (_dot(xn, w_ref[:, 0:d].astype(BF16)) * conv).astype(BF16)


def _conv_mixer(h, gain, w_in, conv_w, index):
    b, s, d = h.shape
    tm = TOKEN_TILE
    rows_per_tile = tm // SUBLANES
    return pl.pallas_call(
        _conv_kernel,
        grid=(b, s // tm),
        in_specs=[
            pl.BlockSpec((None, tm, d), lambda bi, i: (bi, i, 0)),
            pl.BlockSpec((None, SUBLANES, d),
                         lambda bi, i: (bi, jnp.maximum(i * rows_per_tile - 1, 0), 0)),
            _resident((1, d)),
            _layer_slice((d, 3 * d), index),
            _layer_slice((CONV_WIDTH, d), index),
        ],
        out_specs=pl.BlockSpec((None, tm, d), lambda bi, i: (bi, i, 0)),
        out_shape=jax.ShapeDtypeStruct((b, s, d), BF16),
        scratch_shapes=[pltpu.VMEM((tm + SUBLANES, d), F32)],
        compiler_params=_params(("parallel", "parallel")),
        name="conv_mixer",
    )(h, h, gain, w_in, conv_w)


def _gla_proj_kernel(x_ref, g_ref, w_ref, wa_ref, wgu_ref, bg_ref,
                     q_ref, kt_ref, v_ref, og_ref, lg_ref, lgt_ref):
    dk, dv = GLA_DK, GLA_DV
    xn = _rms(x_ref[...], g_ref[...]).astype(BF16)
    q_ref[...] = _dot(xn, w_ref[:, 0:dk].astype(BF16)).astype(BF16)
    kt_ref[...] = _dot(xn, w_ref[:, dk:2 * dk].astype(BF16)).T.astype(BF16)
    v_ref[...] = _dot(xn, w_ref[:, 2 * dk:2 * dk + dv].astype(BF16)).astype(BF16)
    og_ref[...] = _dot(xn, w_ref[:, 2 * dk + dv:2 * dk + 2 * dv].astype(BF16)).astype(BF16)
    a_low = _dot(xn, wa_ref[...]).astype(BF16)
    pre = _dot(a_low, wgu_ref[...]) + bg_ref[...]
    lg = _log_sigmoid(pre) * (LOG2E / GLA_GATE_NORMALIZER)
    lg_ref[...] = lg
    lgt_ref[...] = lg.T


def _gla_proj(h, gain, w_in, index, w_a, w_gu, b_gate):
    b, s, d = h.shape
    tm = TOKEN_TILE
    dk, dv = GLA_DK, GLA_DV
    row = lambda bi, i: (bi, i, 0)
    colmajor = lambda bi, i: (bi, 0, i)
    return pl.pallas_call(
        _gla_proj_kernel,
        grid=(b, s // tm),
        in_specs=[
            pl.BlockSpec((None, tm, d), row),
            _resident((1, d)),
            _layer_slice((d, 2 * dk + 2 * dv), index),
            _resident((d, LANES)),
            _resident((LANES, dk)),
            _resident((1, dk)),
        ],
        out_specs=[
            pl.BlockSpec((None, tm, dk), row),
            pl.BlockSpec((None, dk, tm), colmajor),
            pl.BlockSpec((None, tm, dv), row),
            pl.BlockSpec((None, tm, dv), row),
            pl.BlockSpec((None, tm, dk), row),
            pl.BlockSpec((None, dk, tm), colmajor),
        ],
        out_shape=[
            jax.ShapeDtypeStruct((b, s, dk), BF16),
            jax.ShapeDtypeStruct((b, dk, s), BF16),
            jax.ShapeDtypeStruct((b, s, dv), BF16),
            jax.ShapeDtypeStruct((b, s, dv), BF16),
            jax.ShapeDtypeStruct((b, s, dk), F32),
            jax.ShapeDtypeStruct((b, dk, s), F32),
        ],
        compiler_params=_params(("parallel", "parallel")),
        name="gla_proj",
    )(h, gain, w_in, w_a, w_gu, b_gate)


def _gla_constants():
    c = GLA_TILE
    i = np.arange(c)[:, None]
    m = np.arange(c)[None, :]
    fwd, rev, masks = [], [], []
    for level in range(1, GLA_LEVELS + 1):
        n = 1 << level
        same = (i // n) == (m // n)
        fwd.append((same & (m <= i)).astype(np.float32))
        rev.append((same & (m > i)).astype(np.float32).T)
    rev.append(np.ones((c, c), np.float32))
    for level in range(GLA_LEVELS):
        n = 1 << level
        masks.append(((i // (2 * n) == m // (2 * n)) & ((i // n) % 2 == 1)
                      & ((m // n) % 2 == 0)).astype(np.float32))
    masks.append((i == m).astype(np.float32))
    fwd = np.concatenate(fwd, axis=0)
    rev = np.concatenate(rev, axis=1)
    fwd2 = np.concatenate([fwd, fwd], axis=1)
    rev2 = np.concatenate([rev, rev], axis=0)
    return (jnp.asarray(fwd2, dtype=BF16), jnp.asarray(rev2, dtype=BF16),
            jnp.asarray(np.stack(masks), dtype=F32))


def _gla_chunk_kernel(q_ref, kt_ref, v_ref, og_ref, lg_ref, lgt_ref, fwd_ref, rev_ref,
                      mask_ref, hn_ref, a_ref, state_ref):
    c = GLA_TILE
    n_ch = GLA_STEP_CHUNKS
    dk = GLA_DK
    dkh, dvh = GLA_DK_HEAD, GLA_DV_HEAD
    n_lv = GLA_LEVELS

    @pl.when(pl.program_id(1) == 0)
    def _():
        state_ref[...] = jnp.zeros_like(state_ref)

    lg = lg_ref[...]
    hi, lo = _split_bf16(lg)
    g_cat = jnp.concatenate(
        [jnp.concatenate([hi[ch * c:(ch + 1) * c], lo[ch * c:(ch + 1) * c]], axis=0)
         for ch in range(n_ch)], axis=1)
    fwd_all = _dot(fwd_ref[...], g_cat)
    hi_t, lo_t = _split_bf16(lgt_ref[...])
    gt_cat = jnp.concatenate(
        [jnp.concatenate([hi_t[:, ch * c:(ch + 1) * c], lo_t[:, ch * c:(ch + 1) * c]], axis=1)
         for ch in range(n_ch)], axis=0)
    rev_all = _dot(gt_cat, rev_ref[...])

    for hd in range(GLA_HEADS):
        ks = slice(hd * dkh, (hd + 1) * dkh)
        vs = slice(hd * dvh, (hd + 1) * dvh)
        state = state_ref[hd]
        for ch in range(n_ch):
            rows = slice(ch * c, (ch + 1) * c)
            kcols = slice(ch * dk + hd * dkh, ch * dk + (hd + 1) * dkh)
            q = q_ref[rows, ks].astype(F32) * GLA_SCALE
            kt = kt_ref[ks, rows].astype(F32)
            v = v_ref[rows, vs]
            kt_bf = kt.astype(BF16)
            scores = _dot(q.astype(BF16), kt_bf) * mask_ref[n_lv]
            qd = q * jnp.exp2(lg[rows, ks])
            scores = scores + _dot(qd.astype(BF16), kt_bf) * mask_ref[0]
            for lv in range(1, n_lv):
                qd = q * jnp.exp2(fwd_all[(lv - 1) * c:lv * c, kcols])
                kd = kt * jnp.exp2(rev_all[kcols, (lv - 1) * c:lv * c])
                scores = scores + _dot(qd.astype(BF16), kd.astype(BF16)) * mask_ref[lv]
            q_cum = q * jnp.exp2(fwd_all[(n_lv - 1) * c:n_lv * c, kcols])
            out = _dot(jnp.concatenate([q_cum.astype(BF16), scores.astype(BF16)], axis=1),
                       jnp.concatenate([state.astype(BF16), v], axis=0))
            k_rest = kt * jnp.exp2(rev_all[kcols, (n_lv - 1) * c:n_lv * c])
            keep = jnp.exp2(rev_all[kcols, n_lv * c:(n_lv + 1) * c])
            keep = jnp.concatenate([keep] * (dvh // c), axis=1)
            state = keep * state + _dot(k_rest.astype(BF16), v)
            o = _rms(out, hn_ref[:, vs])
            og = og_ref[rows, vs].astype(F32)
            a_ref[rows, vs] = (o * (og * (1.0 / (1.0 + jnp.exp(-og))))).astype(BF16)
        state_ref[hd] = state


def _gla_chunk(q, kt, v, og, lg, lgt, head_norm):
    b, s, dk = q.shape
    dv = v.shape[-1]
    c = GLA_TILE * GLA_STEP_CHUNKS
    fwd, rev, masks = _gla_constants()
    row = lambda bi, i: (bi, i, 0)
    colmajor = lambda bi, i: (bi, 0, i)
    return pl.pallas_call(
        _gla_chunk_kernel,
        grid=(b, s // c),
        in_specs=[
            pl.BlockSpec((None, c, dk), row),
            pl.BlockSpec((None, dk, c), colmajor),
            pl.BlockSpec((None, c, dv), row),
            pl.BlockSpec((None, c, dv), row),
            pl.BlockSpec((None, c, dk), row),
            pl.BlockSpec((None, dk, c), colmajor),
            _resident(fwd.shape),
            _resident(rev.shape),
            _resident(masks.shape),
            _resident((1, dv)),
        ],
        out_specs=pl.BlockSpec((None, c, dv), row),
        out_shape=jax.ShapeDtypeStruct((b, s, dv), BF16),
        scratch_shapes=[pltpu.VMEM((GLA_HEADS, GLA_DK_HEAD, GLA_DV_HEAD), F32)],
        compiler_params=_params(("parallel", "arbitrary")),
        name="gla_chunk",
    )(q, kt, v, og, lg, lgt, fwd, rev, masks, head_norm)


def _post_kernel(a_ref, h_ref, wo_ref, g_ref, wup_ref, wdn_ref, o_ref, acc_ref):
    half = TOKEN_TILE // 2
    halves = (slice(0, half), slice(half, TOKEN_TILE))
    w_o = wo_ref[...].astype(BF16)
    h1 = []
    xn = []
    for rows in halves:
        mixed = _dot(a_ref[rows, :], w_o)
        h1.append(h_ref[rows, :] + _rms(mixed, g_ref[1:2, :]))
        xn.append(_rms(h1[-1], g_ref[2:3, :]).astype(BF16))
    xn_full = jnp.concatenate(xn, axis=0)
    n_chunks = D_FF // FF_CHUNK
    for ci in range(n_chunks):
        cols = slice(ci * FF_CHUNK, (ci + 1) * FF_CHUNK)
        w_up = wup_ref[:, cols].astype(BF16)
        w_down = wdn_ref[cols, :].astype(BF16)
        if ci == 0:
            up = jnp.concatenate([_dot(part, w_up) for part in xn], axis=0)
        else:
            up = _dot(xn_full, w_up)
        up = jnp.maximum(up, 0.0)
        act = (up * up).astype(BF16)
        if ci == 0:
            acc_ref[...] = _dot(act, w_down)
        elif ci < n_chunks - 1:
            acc_ref[...] += _dot(act, w_down)
        else:
            for hi, rows in enumerate(halves):
                ffn = acc_ref[rows, :] + _dot(act[rows, :], w_down)
                o_ref[rows, :] = h1[hi] + _rms(ffn, g_ref[3:4, :])


def _layer_slice(shape, index):
    zeros = (0,) * len(shape)
    return pl.BlockSpec((None,) + tuple(shape), lambda *_: (index,) + zeros,
                        pipeline_mode=pl.Buffered(1))


def _post(a, h, w_o, mixer_index, gains, w_up, w_down, layer):
    b, s, d = h.shape
    tm = TOKEN_TILE
    row = lambda bi, i: (bi, i, 0)
    return pl.pallas_call(
        _post_kernel,
        grid=(b, s // tm),
        in_specs=[
            pl.BlockSpec((None, tm, d), row),
            pl.BlockSpec((None, tm, d), row),
            _layer_slice((d, d), mixer_index),
            _layer_slice((4, d), layer),
            _layer_slice((d, D_FF), layer),
            _layer_slice((D_FF, d), layer),
        ],
        out_specs=pl.BlockSpec((None, tm, d), row),
        out_shape=jax.ShapeDtypeStruct((b, s, d), F32),
        scratch_shapes=[pltpu.VMEM((tm, d), F32)],
        compiler_params=_params(("parallel", "parallel")),
        name="post_ffn",
    )(a, h, w_o, gains, w_up, w_down)


def kernel(x, norm_gains, sb_w_qkv, sb_w_o, conv_w_in, conv_w, conv_w_out, gla_w_in,
           gla_w_gate_up, gla_b_gate, gla_head_norm, gla_w_o, ffn_w_up, ffn_w_down):
    depth = norm_gains.shape[0]
    h = x
    for i in range(depth):
        kind, j = i % 3, i // 3
        gains = norm_gains[i]
        pre_gain = gains[0:1]
        if kind == 0:
            a = _sb_mixer(h, pre_gain, sb_w_qkv[j].astype(BF16))
            w_o = sb_w_o
        elif kind == 1:
            a = _conv_mixer(h, pre_gain, conv_w_in, conv_w, j)
            w_o = conv_w_out
        else:
            n_main = 2 * GLA_DK + 2 * GLA_DV
            w_in = gla_w_in[j]
            w_a = jnp.pad(w_in[:, n_main:], ((0, 0), (0, LANES - GLA_GATE_RANK))).astype(BF16)
            w_gu = jnp.pad(gla_w_gate_up[j], ((0, LANES - GLA_GATE_RANK), (0, 0))).astype(BF16)
            q, kt, v, og, lg, lgt = _gla_proj(h, pre_gain, gla_w_in, j, w_a, w_gu,
                                              gla_b_gate[j][None, :])
            a = _gla_chunk(q, kt, v, og, lg, lgt, gla_head_norm[j].reshape(1, GLA_DV))
            w_o = gla_w_o
        h = _post(a, h, w_o, j, norm_gains, ffn_w_up, ffn_w_down, i)
    return h
```

```python
import math

import numpy as np
import jax
import jax.numpy as jnp
from jax import lax
from jax.experimental import pallas as pl
from jax.experimental.pallas import tpu as pltpu

F32 = jnp.float32
BF16 = jnp.bfloat16

D_MODEL = 1024
D_FF = 4 * D_MODEL
RMS_EPS = 1e-6

SB_HEADS = 16
SB_HEAD_DIM = D_MODEL // SB_HEADS
LOG2E = math.log2(math.e)
SB_Q_SCALE = -(SB_HEAD_DIM ** -0.5) * LOG2E
CONV_WIDTH = 3
GLA_HEADS = 4
GLA_DK = D_MODEL // 2
GLA_DV = D_MODEL
GLA_DK_HEAD = GLA_DK // GLA_HEADS
GLA_DV_HEAD = GLA_DV // GLA_HEADS
GLA_GATE_RANK = 16
GLA_GATE_NORMALIZER = 16.0
GLA_SCALE = GLA_DK_HEAD ** -0.5

LANES = 128
SUBLANES = 8
VMEM_LIMIT_BYTES = 58 * 1024 * 1024

TOKEN_TILE = 512
PROJ_TILE = 1024
FF_CHUNK = 1024
SB_TILE = 512
SB_SUB = 128
SB_BAND_SUBS = 3
SB_GROUP = 2
SB_PAD_SUBS = SB_BAND_SUBS - 1
SB_HP_PER_STEP = 2
SB_F32_EXP2_ZERO = -152.0
GLA_TILE = 128
GLA_STEP_CHUNKS = 2
GLA_LEVELS = 7


def _rms(x, gain):
    ms = jnp.mean(x * x, axis=-1, keepdims=True)
    return x * lax.rsqrt(ms + RMS_EPS) * gain


def _neg_abs(z):
    bits = lax.bitcast_convert_type(z, jnp.uint32) | jnp.uint32(0x80000000)
    return lax.bitcast_convert_type(bits, F32)


def _log_sigmoid(z):
    return jnp.minimum(z, 0.0) - jnp.log(1.0 + jnp.exp(_neg_abs(z)))


def _log2_one_minus_sigmoid(y):
    return jnp.minimum(y, 0.0) - jnp.log(1.0 + jnp.exp2(_neg_abs(y))) * LOG2E


def _split_bf16(x):
    hi = x.astype(BF16)
    lo = (x - hi.astype(F32)).astype(BF16)
    return hi, lo


def _dot(a, b):
    return jnp.dot(a, b, preferred_element_type=F32)


def _resident(shape):
    zeros = (0,) * len(shape)
    return pl.BlockSpec(shape, lambda *_: zeros, pipeline_mode=pl.Buffered(1))


def _params(semantics):
    return pltpu.CompilerParams(dimension_semantics=semantics,
                                vmem_limit_bytes=VMEM_LIMIT_BYTES)


def _sb_suffix_matrix():
    t = SB_SUB
    m = np.arange(t)[:, None]
    j = np.arange(t)[None, :]
    half = np.concatenate([(m >= j).astype(np.float32), np.ones((t, t), np.float32)], axis=1)
    return jnp.asarray(np.concatenate([half, half], axis=0), dtype=BF16)


def _sb_layer_kernel(x0_ref, x1_ref, x2_ref, g_ref, w_ref, u_ref, o_ref,
                     q_scr, kt_scr, v_scr, kt_new, v_new, xn_buf, o_scr, acc_ref, c_ref):
    t = SB_TILE
    sub = SB_SUB
    n_rb = t // sub
    band = SB_BAND_SUBS
    pad = SB_PAD_SUBS
    grp = SB_GROUP
    win = band + grp - 1
    pairs = n_rb // grp
    grp_rows = grp * sub
    n_hp = D_MODEL // LANES
    duo = SB_HP_PER_STEP
    i = pl.program_id(1)
    step = pl.program_id(0) * pl.num_programs(1) + i
    gain = g_ref[...]
    u = u_ref[...]

    def weight_columns(which, j):
        width = duo * LANES
        return w_ref[:, pl.ds(pl.multiple_of(which * D_MODEL + j * width, width), width)]

    def project_q(xn, j, slot):
        y = _dot(xn, weight_columns(0, j)) * SB_Q_SCALE
        for e in range(duo):
            q_scr[slot, j * duo + e] = y[:, e * LANES:(e + 1) * LANES].astype(BF16)

    def project_k(xn, j):
        y = _dot(xn, weight_columns(1, j))
        for e in range(duo):
            for kb in range(n_rb):
                kt_new[j * duo + e, kb] = (
                    y[kb * sub:(kb + 1) * sub, e * LANES:(e + 1) * LANES].T.astype(BF16))

    def project_v(xn, j):
        y = _dot(xn, weight_columns(2, j)).astype(BF16)
        for e in range(duo):
            v_new[j * duo + e] = y[:, e * LANES:(e + 1) * LANES]

    @pl.when(i == 0)
    def _():
        kt_scr[:, 0:pad] = jnp.zeros((n_hp, pad, LANES, sub), BF16)
        v_scr[:, 0:pad * sub, :] = jnp.zeros((n_hp, pad * sub, LANES), BF16)

    @pl.when(step == 0)
    def _():
        xn = _rms(x0_ref[...], gain).astype(BF16)

        def first_tile(j, carry):
            project_q(xn, j, 0)
            project_k(xn, j)
            project_v(xn, j)
            return carry

        lax.fori_loop(0, n_hp // duo, first_tile, 0)
        xn_buf[0] = _rms(x1_ref[...], gain).astype(BF16)

    slot = step % 2
    next_slot = (step + 1) % 2
    piece = t // (n_hp // duo)

    lane = lax.broadcasted_iota(jnp.int32, (t, LANES), 1)
    first = lane < SB_HEAD_DIM
    first_grp = lax.broadcasted_iota(jnp.int32, (grp_rows, LANES), 1) < SB_HEAD_DIM
    row = lax.broadcasted_iota(jnp.int32, (sub, sub), 0)
    col = lax.broadcasted_iota(jnp.int32, (sub, sub), 1)
    diag = col < row
    no_weight = jnp.zeros((sub, sub), BF16)
    row_blk = lax.broadcasted_iota(jnp.int32, (t, sub), 0) // sub
    not_first_tile = i > 0

    def append_projected(hp):
        for kb in range(n_rb):
            kt_scr[hp, pad + i * n_rb + kb] = kt_new[hp, kb]
        v_scr[hp, pl.ds(pl.multiple_of((pad + i * n_rb) * sub, sub), t), :] = v_new[hp]

    def split_heads(hp):
        q = q_scr[slot, hp]
        zero = jnp.zeros_like(q)
        return jnp.where(first, q, zero), jnp.where(first, zero, q)

    def band_logits(hp, q_heads):
        parts = []
        tiles = {}
        for p in range(pairs):
            first_sub = pad + i * n_rb + p * grp - band + 1
            kwin = jnp.concatenate([kt_scr[hp, first_sub + s] for s in range(win)], axis=1)
            q_both = jnp.concatenate(
                [q_heads[hd][p * grp_rows:(p + 1) * grp_rows] for hd in range(2)], axis=0)
            y = _dot(q_both, kwin)
            for hd in range(2):
                for rr in range(grp):
                    top = hd * grp_rows + rr * sub
                    for s in range(band):
                        yt = y[top:top + sub, (rr + s) * sub:(rr + s + 1) * sub]
                        part = _log2_one_minus_sigmoid(yt)
                        mask = diag if s == band - 1 else None
                        if p * grp + rr + s < band - 1:
                            mask = not_first_tile if mask is None else mask & not_first_tile
                        if mask is not None:
                            part = jnp.where(mask, part, 0.0)
                        hi, lo = _split_bf16(part)
                        tiles[p, hd, rr, s] = (len(parts), yt, mask)
                        parts.append(jnp.concatenate([hi, lo], axis=1))
        return jnp.concatenate(parts, axis=0), tiles

    def band_weights(hp, s2, tiles, p):
        first_sub = pad + i * n_rb + p * grp - band + 1
        w_rows = []
        carries = []
        for hd in range(2):
            for rr in range(grp):
                r = p * grp + rr
                c = jnp.zeros((sub, sub), F32)
                ws = [None] * band
                for s in reversed(range(band)):
                    index, yt, mask = tiles[p, hd, rr, s]
                    blk = s2[index * sub:(index + 1) * sub]
                    w = jnp.exp2(blk[:, :sub] + c - yt)
                    if mask is not None:
                        w = jnp.where(mask, w, 0.0)
                    ws[s] = w.astype(BF16)
                    c = c + blk[:, sub:]
                c_ref[hp, hd, r * sub:(r + 1) * sub, :] = c
                carries.append(c)
                w_rows.append(jnp.concatenate(
                    [no_weight] * rr + ws + [no_weight] * (grp - 1 - rr), axis=1))
        vwin = v_scr[hp, pl.ds(pl.multiple_of(first_sub * sub, sub), win * sub), :]
        pv = _dot(jnp.concatenate(w_rows, axis=0), vwin)
        rows = slice(p * grp_rows, (p + 1) * grp_rows)
        for hd in range(2):
            acc_ref[hp, hd, rows, :] = pv[hd * grp_rows:(hd + 1) * grp_rows]
        o_scr[hp, rows, :] = jnp.where(first_grp, pv[:grp_rows], pv[grp_rows:]).astype(BF16)
        return jnp.concatenate(
            [jnp.maximum(carries[rr], carries[grp + rr]) for rr in range(grp)], axis=0)

    def largest_open_carry(c, m):
        has_keys = row_blk + i * n_rb - band + 1 - m >= 0
        return jnp.max(jnp.where(has_keys, c, -jnp.inf))

    def extend_band(hp, q_heads, m):
        parts = []
        tiles = {}
        starts = []
        for p in range(pairs):
            first_sub = i * n_rb + p * grp - band + 1 - m
            start = jnp.maximum(pad + first_sub, 0)
            starts.append(start)
            kwin = jnp.concatenate([kt_scr[hp, start + s] for s in range(grp)], axis=1)
            q_both = jnp.concatenate(
                [q_heads[hd][p * grp_rows:(p + 1) * grp_rows] for hd in range(2)], axis=0)
            y = _dot(q_both, kwin)
            for hd in range(2):
                for rr in range(grp):
                    top = hd * grp_rows + rr * sub
                    yt = y[top:top + sub, rr * sub:(rr + 1) * sub]
                    in_sequence = first_sub + rr >= 0
                    part = jnp.where(in_sequence, _log2_one_minus_sigmoid(yt), 0.0)
                    hi, lo = _split_bf16(part)
                    tiles[p, hd, rr] = (len(parts), yt, in_sequence)
                    parts.append(jnp.concatenate([hi, lo], axis=1))
        s2 = _dot(jnp.concatenate(parts, axis=0), u)
        for p in range(pairs):
            w_rows = []
            for hd in range(2):
                for rr in range(grp):
                    rows = slice((p * grp + rr) * sub, (p * grp + rr + 1) * sub)
                    index, yt, in_sequence = tiles[p, hd, rr]
                    blk = s2[index * sub:(index + 1) * sub]
                    c = c_ref[hp, hd, rows, :]
                    w = jnp.where(in_sequence, jnp.exp2(blk[:, :sub] + c - yt), 0.0).astype(BF16)
                    c_ref[hp, hd, rows, :] = c + blk[:, sub:]
                    w_rows.append(jnp.concatenate(
                        [no_weight] * rr + [w] + [no_weight] * (grp - 1 - rr), axis=1))
            vwin = v_scr[hp, pl.ds(pl.multiple_of(starts[p] * sub, sub), grp * sub), :]
            pv = _dot(jnp.concatenate(w_rows, axis=0), vwin)
            rows = slice(p * grp_rows, (p + 1) * grp_rows)
            for hd in range(2):
                acc_ref[hp, hd, rows, :] += pv[hd * grp_rows:(hd + 1) * grp_rows]

    def further_left(hp):
        q_heads = split_heads(hp)
        m_last = i * n_rb + n_rb - band

        def more_keys_matter(state):
            m, c_max = state
            return jnp.logical_and(m <= m_last, c_max > SB_F32_EXP2_ZERO)

        def one_more_sub_block(state):
            m, _ = state
            extend_band(hp, q_heads, m)
            return m + 1, largest_open_carry(c_ref[hp], m + 1)

        lax.while_loop(more_keys_matter, one_more_sub_block,
                       (1, largest_open_carry(c_ref[hp], 1)))
        o_scr[hp] = jnp.where(first, acc_ref[hp, 0], acc_ref[hp, 1]).astype(BF16)

    def per_head_pair_duo(j, largest_carry):
        hps = [j * duo + e for e in range(duo)]
        xn_next = xn_buf[slot]
        for hp in hps:
            append_projected(hp)
        q_heads = [split_heads(hp) for hp in hps]
        stacked = [band_logits(hp, q_heads[e]) for e, hp in enumerate(hps)]
        project_q(xn_next, j, next_slot)
        project_k(xn_next, j)
        s2 = []
        for e in range(duo):
            s2.append(_dot(stacked[e][0], u))
            if e == 0:
                project_v(xn_next, j)
        for e, hp in enumerate(hps):
            carry = [band_weights(hp, s2[e], stacked[e][1], p) for p in range(pairs)]
            largest_carry = jnp.maximum(largest_carry, jnp.concatenate(carry, axis=0))
        rows = pl.ds(pl.multiple_of(j * piece, piece), piece)
        xn_buf[next_slot, rows, :] = _rms(x2_ref[rows, :], gain).astype(BF16)
        return largest_carry

    largest_carry = lax.fori_loop(0, n_hp // duo, per_head_pair_duo,
                                  jnp.full((t, sub), -jnp.inf, F32))

    @pl.when(largest_open_carry(largest_carry, 1) > SB_F32_EXP2_ZERO)
    def _():
        def visit(hp, carry):
            further_left(hp)
            return carry

        lax.fori_loop(0, n_hp, visit, 0)

    o_ref[...] = jnp.concatenate([o_scr[hp] for hp in range(n_hp)], axis=1)


def _sb_mixer(h, gain, w_qkv):
    b, s, d = h.shape
    t = SB_TILE
    n_hp = d // LANES
    duo = SB_HP_PER_STEP
    n_tiles = s // t
    n_subs = SB_PAD_SUBS + s // SB_SUB

    def tile_after_next(bi, i):
        tile = jnp.minimum(bi * n_tiles + i + 2, b * n_tiles - 1)
        return tile // n_tiles, tile % n_tiles, 0

    return pl.pallas_call(
        _sb_layer_kernel,
        grid=(b, n_tiles),
        in_specs=[
            pl.BlockSpec((None, t, d), lambda bi, i: (0, 0, 0), pipeline_mode=pl.Buffered(1)),
            pl.BlockSpec((None, t, d), lambda bi, i: (0, 1, 0), pipeline_mode=pl.Buffered(1)),
            pl.BlockSpec((None, t, d), tile_after_next),
            _resident((1, d)),
            _resident((d, 3 * d)),
            _resident((2 * SB_SUB, 2 * SB_SUB)),
        ],
        out_specs=pl.BlockSpec((None, t, d), lambda bi, i: (bi, i, 0)),
        out_shape=jax.ShapeDtypeStruct((b, s, d), BF16),
        scratch_shapes=[
            pltpu.VMEM((2, n_hp, t, LANES), BF16),
            pltpu.VMEM((n_hp, n_subs, LANES, SB_SUB), BF16),
            pltpu.VMEM((n_hp, n_subs * SB_SUB, LANES), BF16),
            pltpu.VMEM((n_hp, t // SB_SUB, LANES, SB_SUB), BF16),
            pltpu.VMEM((n_hp, t, LANES), BF16),
            pltpu.VMEM((2, t, d), BF16),
            pltpu.VMEM((n_hp, t, LANES), BF16),
            pltpu.VMEM((n_hp, 2, t, LANES), F32),
            pltpu.VMEM((n_hp, 2, t, SB_SUB), F32),
        ],
        compiler_params=_params(("arbitrary", "arbitrary")),
        name="sb_layer",
    )(h, h, h, gain, w_qkv, _sb_suffix_matrix())


def _conv_kernel(x_ref, xp_ref, g_ref, w_ref, cw_ref, a_ref, hbuf_ref):
    d = D_MODEL
    tm = PROJ_TILE
    i = pl.program_id(1)
    gain = g_ref[...]
    xn = _rms(x_ref[...], gain).astype(BF16)
    xpn = _rms(xp_ref[...], gain).astype(BF16)
    w_c = w_ref[:, d:2 * d].astype(BF16)
    w_u = w_ref[:, 2 * d:3 * d].astype(BF16)
    hbuf_ref[SUBLANES:, :] = _dot(xn, w_c) * _dot(xn, w_u)
    h_prev = _dot(xpn, w_c) * _dot(xpn, w_u)
    hbuf_ref[0:SUBLANES, :] = jnp.where(i > 0, h_prev, 0.0)
    conv = cw_ref[0:1, :] * hbuf_ref[pl.ds(SUBLANES - 2, tm), :]
    conv = conv + cw_ref[1:2, :] * hbuf_ref[pl.ds(SUBLANES - 1, tm), :]
    conv = conv + cw_ref[2:3, :] * hbuf_ref[pl.ds(SUBLANES, tm), :]
    a_ref[...] = (_dot(xn, w_ref[:, 0:d].astype(BF16)) * conv).astype(BF16)


def _conv_mixer(h, gain, w_in, conv_w, index):
    b, s, d = h.shape
    tm = PROJ_TILE
    rows_per_tile = tm // SUBLANES
    return pl.pallas_call(
        _conv_kernel,
        grid=(b, s // tm),
        in_specs=[
            pl.BlockSpec((None, tm, d), lambda bi, i: (bi, i, 0)),
            pl.BlockSpec((None, SUBLANES, d),
                         lambda bi, i: (bi, jnp.maximum(i * rows_per_tile - 1, 0), 0)),
            _resident((1, d)),
            _layer_slice((d, 3 * d), index),
            _layer_slice((CONV_WIDTH, d), index),
        ],
        out_specs=pl.BlockSpec((None, tm, d), lambda bi, i: (bi, i, 0)),
        out_shape=jax.ShapeDtypeStruct((b, s, d), BF16),
        scratch_shapes=[pltpu.VMEM((tm + SUBLANES, d), F32)],
        compiler_params=_params(("parallel", "parallel")),
        name="conv_mixer",
    )(h, h, gain, w_in, conv_w)


def _gla_proj_kernel(x_ref, g_ref, w_ref, wa_ref, wgu_ref, bg_ref,
                     q_ref, kt_ref, v_ref, og_ref, lg_ref, lgt_ref):
    dk, dv = GLA_DK, GLA_DV
    xn = _rms(x_ref[...], g_ref[...]).astype(BF16)
    a_low = _dot(xn, wa_ref[...]).astype(BF16)
    pre = _dot(a_low, wgu_ref[...]) + bg_ref[...]
    k = _dot(xn, w_ref[:, dk:2 * dk].astype(BF16))
    lg = _log_sigmoid(pre) * (LOG2E / GLA_GATE_NORMALIZER)
    lg_ref[...] = lg
    lgt_ref[...] = lg.T
    kt_ref[...] = k.T.astype(BF16)
    q_ref[...] = _dot(xn, w_ref[:, 0:dk].astype(BF16)).astype(BF16)
    v_ref[...] = _dot(xn, w_ref[:, 2 * dk:2 * dk + dv].astype(BF16)).astype(BF16)
    og_ref[...] = _dot(xn, w_ref[:, 2 * dk + dv:2 * dk + 2 * dv].astype(BF16)).astype(BF16)


def _gla_proj(h, gain, w_in, index, w_a, w_gu, b_gate):
    b, s, d = h.shape
    tm = PROJ_TILE
    dk, dv = GLA_DK, GLA_DV
    row = lambda bi, i: (bi, i, 0)
    colmajor = lambda bi, i: (bi, 0, i)
    return pl.pallas_call(
        _gla_proj_kernel,
        grid=(b, s // tm),
        in_specs=[
            pl.BlockSpec((None, tm, d), row),
            _resident((1, d)),
            _layer_slice((d, 2 * dk + 2 * dv), index),
            _resident((d, LANES)),
            _resident((LANES, dk)),
            _resident((1, dk)),
        ],
        out_specs=[
            pl.BlockSpec((None, tm, dk), row),
            pl.BlockSpec((None, dk, tm), colmajor),
            pl.BlockSpec((None, tm, dv), row),
            pl.BlockSpec((None, tm, dv), row),
            pl.BlockSpec((None, tm, dk), row),
            pl.BlockSpec((None, dk, tm), colmajor),
        ],
        out_shape=[
            jax.ShapeDtypeStruct((b, s, dk), BF16),
            jax.ShapeDtypeStruct((b, dk, s), BF16),
            jax.ShapeDtypeStruct((b, s, dv), BF16),
            jax.ShapeDtypeStruct((b, s, dv), BF16),
            jax.ShapeDtypeStruct((b, s, dk), F32),
            jax.ShapeDtypeStruct((b, dk, s), F32),
        ],
        compiler_params=_params(("parallel", "parallel")),
        name="gla_proj",
    )(h, gain, w_in, w_a, w_gu, b_gate)


def _gla_constants():
    c = GLA_TILE
    i = np.arange(c)[:, None]
    m = np.arange(c)[None, :]
    fwd, rev, masks = [], [], []
    for level in range(1, GLA_LEVELS + 1):
        n = 1 << level
        same = (i // n) == (m // n)
        fwd.append((same & (m <= i)).astype(np.float32))
        rev.append((same & (m > i)).astype(np.float32).T)
    rev.append(np.ones((c, c), np.float32))
    for level in range(GLA_LEVELS):
        n = 1 << level
        masks.append(((i // (2 * n) == m // (2 * n)) & ((i // n) % 2 == 1)
                      & ((m // n) % 2 == 0)).astype(np.float32))
    masks.append((i == m).astype(np.float32))
    fwd = np.concatenate(fwd, axis=0)
    rev = np.concatenate(rev, axis=1)
    fwd2 = np.concatenate([fwd, fwd], axis=1)
    rev2 = np.concatenate([rev, rev], axis=0)
    return (jnp.asarray(fwd2, dtype=BF16), jnp.asarray(rev2, dtype=BF16),
            jnp.asarray(np.stack(masks), dtype=F32))


def _gla_chunk_kernel(q_ref, kt_ref, v_ref, og_ref, lg_ref, lgt_ref, fwd_ref, rev_ref,
                      mask_ref, hn_ref, a_ref, state_ref):
    c = GLA_TILE
    n_ch = GLA_STEP_CHUNKS
    dk = GLA_DK
    dkh, dvh = GLA_DK_HEAD, GLA_DV_HEAD
    n_lv = GLA_LEVELS

    @pl.when(pl.program_id(1) == 0)
    def _():
        state_ref[...] = jnp.zeros_like(state_ref)

    lg = lg_ref[...]
    hi, lo = _split_bf16(lg)
    g_cat = jnp.concatenate(
        [jnp.concatenate([hi[ch * c:(ch + 1) * c], lo[ch * c:(ch + 1) * c]], axis=0)
         for ch in range(n_ch)], axis=1)
    fwd_all = _dot(fwd_ref[...], g_cat)
    hi_t, lo_t = _split_bf16(lgt_ref[...])
    gt_cat = jnp.concatenate(
        [jnp.concatenate([hi_t[:, ch * c:(ch + 1) * c], lo_t[:, ch * c:(ch + 1) * c]], axis=1)
         for ch in range(n_ch)], axis=0)
    rev_all = _dot(gt_cat, rev_ref[...])

    for hd in range(GLA_HEADS):
        ks = slice(hd * dkh, (hd + 1) * dkh)
        vs = slice(hd * dvh, (hd + 1) * dvh)
        state = state_ref[hd]
        for ch in range(n_ch):
            rows = slice(ch * c, (ch + 1) * c)
            kcols = slice(ch * dk + hd * dkh, ch * dk + (hd + 1) * dkh)
            q = q_ref[rows, ks].astype(F32) * GLA_SCALE
            kt = kt_ref[ks, rows].astype(F32)
            v = v_ref[rows, vs]
            kt_bf = kt.astype(BF16)
            scores = _dot(q.astype(BF16), kt_bf) * mask_ref[n_lv]
            qd = q * jnp.exp2(lg[rows, ks])
            scores = scores + _dot(qd.astype(BF16), kt_bf) * mask_ref[0]
            for lv in range(1, n_lv):
                qd = q * jnp.exp2(fwd_all[(lv - 1) * c:lv * c, kcols])
                kd = kt * jnp.exp2(rev_all[kcols, (lv - 1) * c:lv * c])
                scores = scores + _dot(qd.astype(BF16), kd.astype(BF16)) * mask_ref[lv]
            q_cum = q * jnp.exp2(fwd_all[(n_lv - 1) * c:n_lv * c, kcols])
            out = _dot(jnp.concatenate([q_cum.astype(BF16), scores.astype(BF16)], axis=1),
                       jnp.concatenate([state.astype(BF16), v], axis=0))
            k_rest = kt * jnp.exp2(rev_all[kcols, (n_lv - 1) * c:n_lv * c])
            keep = jnp.exp2(rev_all[kcols, n_lv * c:(n_lv + 1) * c])
            keep = jnp.concatenate([keep] * (dvh // c), axis=1)
            state = keep * state + _dot(k_rest.astype(BF16), v)
            o = _rms(out, hn_ref[:, vs])
            og = og_ref[rows, vs].astype(F32)
            a_ref[rows, vs] = (o * (og * (1.0 / (1.0 + jnp.exp(-og))))).astype(BF16)
        state_ref[hd] = state


def _gla_chunk(q, kt, v, og, lg, lgt, head_norm):
    b, s, dk = q.shape
    dv = v.shape[-1]
    c = GLA_TILE * GLA_STEP_CHUNKS
    fwd, rev, masks = _gla_constants()
    row = lambda bi, i: (bi, i, 0)
    colmajor = lambda bi, i: (bi, 0, i)
    return pl.pallas_call(
        _gla_chunk_kernel,
        grid=(b, s // c),
        in_specs=[
            pl.BlockSpec((None, c, dk), row),
            pl.BlockSpec((None, dk, c), colmajor),
            pl.BlockSpec((None, c, dv), row),
            pl.BlockSpec((None, c, dv), row),
            pl.BlockSpec((None, c, dk), row),
            pl.BlockSpec((None, dk, c), colmajor),
            _resident(fwd.shape),
            _resident(rev.shape),
            _resident(masks.shape),
            _resident((1, dv)),
        ],
        out_specs=pl.BlockSpec((None, c, dv), row),
        out_shape=jax.ShapeDtypeStruct((b, s, dv), BF16),
        scratch_shapes=[pltpu.VMEM((GLA_HEADS, GLA_DK_HEAD, GLA_DV_HEAD), F32)],
        compiler_params=_params(("parallel", "arbitrary")),
        name="gla_chunk",
    )(q, kt, v, og, lg, lgt, fwd, rev, masks, head_norm)


def _post_kernel(a_ref, h_ref, wo_ref, g_ref, wup_ref, wdn_ref, o_ref, acc_ref):
    half = TOKEN_TILE // 2
    halves = (slice(0, half), slice(half, TOKEN_TILE))
    w_o = wo_ref[...].astype(BF16)
    h1 = []
    xn = []
    for rows in halves:
        mixed = _dot(a_ref[rows, :], w_o)
        h1.append(h_ref[rows, :] + _rms(mixed, g_ref[1:2, :]))
        xn.append(_rms(h1[-1], g_ref[2:3, :]).astype(BF16))
    xn_full = jnp.concatenate(xn, axis=0)
    n_chunks = D_FF // FF_CHUNK
    for ci in range(n_chunks):
        cols = slice(ci * FF_CHUNK, (ci + 1) * FF_CHUNK)
        w_up = wup_ref[:, cols].astype(BF16)
        w_down = wdn_ref[cols, :].astype(BF16)
        if ci == 0:
            up = jnp.concatenate([_dot(part, w_up) for part in xn], axis=0)
        else:
            up = _dot(xn_full, w_up)
        up = jnp.maximum(up, 0.0)
        act = (up * up).astype(BF16)
        if ci == 0:
            acc_ref[...] = _dot(act, w_down)
        elif ci < n_chunks - 1:
            acc_ref[...] += _dot(act, w_down)
        else:
            for hi, rows in enumerate(halves):
                ffn = acc_ref[rows, :] + _dot(act[rows, :], w_down)
                o_ref[rows, :] = h1[hi] + _rms(ffn, g_ref[3:4, :])


def _layer_slice(shape, index):
    zeros = (0,) * len(shape)
    return pl.BlockSpec((None,) + tuple(shape), lambda *_: (index,) + zeros,
                        pipeline_mode=pl.Buffered(1))


def _post(a, h, w_o, mixer_index, gains, w_up, w_down, layer):
    b, s, d = h.shape
    tm = TOKEN_TILE
    row = lambda bi, i: (bi, i, 0)
    return pl.pallas_call(
        _post_kernel,
        grid=(b, s // tm),
        in_specs=[
            pl.BlockSpec((None, tm, d), row),
            pl.BlockSpec((None, tm, d), row),
            _layer_slice((d, d), mixer_index),
            _layer_slice((4, d), layer),
            _layer_slice((d, D_FF), layer),
            _layer_slice((D_FF, d), layer),
        ],
        out_specs=pl.BlockSpec((None, tm, d), row),
        out_shape=jax.ShapeDtypeStruct((b, s, d), F32),
        scratch_shapes=[pltpu.VMEM((tm, d), F32)],
        compiler_params=_params(("parallel", "parallel")),
        name="post_ffn",
    )(a, h, w_o, gains, w_up, w_down)


def kernel(x, norm_gains, sb_w_qkv, sb_w_o, conv_w_in, conv_w, conv_w_out, gla_w_in,
           gla_w_gate_up, gla_b_gate, gla_head_norm, gla_w_o, ffn_w_up, ffn_w_down):
    depth = norm_gains.shape[0]
    h = x
    for i in range(depth):
        kind, j = i % 3, i // 3
        gains = norm_gains[i]
        pre_gain = gains[0:1]
        if kind == 0:
            a = _sb_mixer(h, pre_gain, sb_w_qkv[j].astype(BF16))
            w_o = sb_w_o
        elif kind == 1:
            a = _conv_mixer(h, pre_gain, conv_w_in, conv_w, j)
            w_o = conv_w_out
        else:
            n_main = 2 * GLA_DK + 2 * GLA_DV
            w_in = gla_w_in[j]
            w_a = jnp.pad(w_in[:, n_main:], ((0, 0), (0, LANES - GLA_GATE_RANK))).astype(BF16)
            w_gu = jnp.pad(gla_w_gate_up[j], ((0, LANES - GLA_GATE_RANK), (0, 0))).astype(BF16)
            q, kt, v, og, lg, lgt = _gla_proj(h, pre_gain, gla_w_in, j, w_a, w_gu,
                                              gla_b_gate[j][None, :])
            a = _gla_chunk(q, kt, v, og, lg, lgt, gla_head_norm[j].reshape(1, GLA_DV))
            w_o = gla_w_o
        h = _post(a, h, w_o, j, norm_gains, ffn_w_up, ffn_w_down, i)
    return h
```

```python
import math

import numpy as np
import jax
import jax.numpy as jnp
from jax import lax
from jax.experimental import pallas as pl
from jax.experimental.pallas import tpu as pltpu

F32 = jnp.float32
BF16 = jnp.bfloat16

D_MODEL = 1024
D_FF = 4 * D_MODEL
RMS_EPS = 1e-6

SB_HEADS = 16
SB_HEAD_DIM = D_MODEL // SB_HEADS
LOG2E = math.log2(math.e)
SB_Q_SCALE = -(SB_HEAD_DIM ** -0.5) * LOG2E
CONV_WIDTH = 3
GLA_HEADS = 4
GLA_DK = D_MODEL // 2
GLA_DV = D_MODEL
GLA_DK_HEAD = GLA_DK // GLA_HEADS
GLA_DV_HEAD = GLA_DV // GLA_HEADS
GLA_GATE_RANK = 16
GLA_GATE_NORMALIZER = 16.0
GLA_SCALE = GLA_DK_HEAD ** -0.5

LANES = 128
SUBLANES = 8
VMEM_LIMIT_BYTES = 58 * 1024 * 1024

TOKEN_TILE = 512
PROJ_TILE = 1024
FF_CHUNK = 1024
SB_TILE = 512
SB_SUB = 128
SB_BAND_SUBS = 3
SB_GROUP = 1
SB_PAD_SUBS = SB_BAND_SUBS - 1
SB_HP_PER_STEP = 2
SB_F32_EXP2_ZERO = -152.0
GLA_TILE = 128
GLA_STEP_CHUNKS = 2
GLA_LEVELS = 7


def _rms(x, gain):
    ms = jnp.mean(x * x, axis=-1, keepdims=True)
    return x * lax.rsqrt(ms + RMS_EPS) * gain


def _neg_abs(z):
    bits = lax.bitcast_convert_type(z, jnp.uint32) | jnp.uint32(0x80000000)
    return lax.bitcast_convert_type(bits, F32)


def _log_sigmoid(z):
    return jnp.minimum(z, 0.0) - jnp.log(1.0 + jnp.exp(_neg_abs(z)))


def _log2_one_minus_sigmoid(y):
    return jnp.minimum(y, 0.0) - jnp.log(1.0 + jnp.exp2(_neg_abs(y))) * LOG2E


def _split_bf16(x):
    hi = x.astype(BF16)
    lo = (x - hi.astype(F32)).astype(BF16)
    return hi, lo


def _dot(a, b):
    return jnp.dot(a, b, preferred_element_type=F32)


def _resident(shape):
    zeros = (0,) * len(shape)
    return pl.BlockSpec(shape, lambda *_: zeros, pipeline_mode=pl.Buffered(1))


def _params(semantics):
    return pltpu.CompilerParams(dimension_semantics=semantics,
                                vmem_limit_bytes=VMEM_LIMIT_BYTES)


def _sb_suffix_matrix():
    t = SB_SUB
    m = np.arange(t)[:, None]
    j = np.arange(t)[None, :]
    half = np.concatenate([(m >= j).astype(np.float32), np.ones((t, t), np.float32)], axis=1)
    return jnp.asarray(np.concatenate([half, half], axis=0), dtype=BF16)


def _sb_layer_kernel(x0_ref, x1_ref, x2_ref, g_ref, w_ref, u_ref, o_ref,
                     q_scr, kt_scr, v_scr, kt_new, v_new, xn_buf, o_scr, acc_ref, c_ref):
    t = SB_TILE
    sub = SB_SUB
    n_rb = t // sub
    band = SB_BAND_SUBS
    pad = SB_PAD_SUBS
    grp = SB_GROUP
    win = band + grp - 1
    pairs = n_rb // grp
    grp_rows = grp * sub
    n_hp = D_MODEL // LANES
    duo = SB_HP_PER_STEP
    i = pl.program_id(1)
    step = pl.program_id(0) * pl.num_programs(1) + i
    gain = g_ref[...]
    u = u_ref[...]

    def weight_columns(which, j):
        width = duo * LANES
        return w_ref[:, pl.ds(pl.multiple_of(which * D_MODEL + j * width, width), width)]

    def project_q(xn, j, slot):
        y = _dot(xn, weight_columns(0, j)) * SB_Q_SCALE
        for e in range(duo):
            q_scr[slot, j * duo + e] = y[:, e * LANES:(e + 1) * LANES].astype(BF16)

    def project_k(xn, j):
        y = _dot(xn, weight_columns(1, j))
        for e in range(duo):
            for kb in range(n_rb):
                kt_new[j * duo + e, kb] = (
                    y[kb * sub:(kb + 1) * sub, e * LANES:(e + 1) * LANES].T.astype(BF16))

    def project_v(xn, j):
        y = _dot(xn, weight_columns(2, j)).astype(BF16)
        for e in range(duo):
            v_new[j * duo + e] = y[:, e * LANES:(e + 1) * LANES]

    @pl.when(i == 0)
    def _():
        kt_scr[:, 0:pad] = jnp.zeros((n_hp, pad, LANES, sub), BF16)
        v_scr[:, 0:pad * sub, :] = jnp.zeros((n_hp, pad * sub, LANES), BF16)

    @pl.when(step == 0)
    def _():
        xn = _rms(x0_ref[...], gain).astype(BF16)

        def first_tile(j, carry):
            project_q(xn, j, 0)
            project_k(xn, j)
            project_v(xn, j)
            return carry

        lax.fori_loop(0, n_hp // duo, first_tile, 0)
        xn_buf[0] = _rms(x1_ref[...], gain).astype(BF16)

    slot = step % 2
    next_slot = (step + 1) % 2
    piece = t // (n_hp // duo)

    lane = lax.broadcasted_iota(jnp.int32, (t, LANES), 1)
    first = lane < SB_HEAD_DIM
    first_grp = lax.broadcasted_iota(jnp.int32, (grp_rows, LANES), 1) < SB_HEAD_DIM
    row = lax.broadcasted_iota(jnp.int32, (sub, sub), 0)
    col = lax.broadcasted_iota(jnp.int32, (sub, sub), 1)
    diag = col < row
    no_weight = jnp.zeros((sub, sub), BF16)
    row_blk = lax.broadcasted_iota(jnp.int32, (t, sub), 0) // sub
    not_first_tile = i > 0

    def append_projected(hp):
        for kb in range(n_rb):
            kt_scr[hp, pad + i * n_rb + kb] = kt_new[hp, kb]
        v_scr[hp, pl.ds(pl.multiple_of((pad + i * n_rb) * sub, sub), t), :] = v_new[hp]

    def split_heads(hp):
        q = q_scr[slot, hp]
        zero = jnp.zeros_like(q)
        return jnp.where(first, q, zero), jnp.where(first, zero, q)

    def band_logits(hp, q_heads):
        parts = []
        tiles = {}
        for p in range(pairs):
            first_sub = pad + i * n_rb + p * grp - band + 1
            kwin = jnp.concatenate([kt_scr[hp, first_sub + s] for s in range(win)], axis=1)
            q_both = jnp.concatenate(
                [q_heads[hd][p * grp_rows:(p + 1) * grp_rows] for hd in range(2)], axis=0)
            y = _dot(q_both, kwin)
            for hd in range(2):
                for rr in range(grp):
                    top = hd * grp_rows + rr * sub
                    for s in range(band):
                        yt = y[top:top + sub, (rr + s) * sub:(rr + s + 1) * sub]
                        part = _log2_one_minus_sigmoid(yt)
                        mask = diag if s == band - 1 else None
                        if p * grp + rr + s < band - 1:
                            mask = not_first_tile if mask is None else mask & not_first_tile
                        if mask is not None:
                            part = jnp.where(mask, part, 0.0)
                        hi, lo = _split_bf16(part)
                        tiles[p, hd, rr, s] = (len(parts), yt, mask)
                        parts.append(jnp.concatenate([hi, lo], axis=1))
        return jnp.concatenate(parts, axis=0), tiles

    def band_weights(hp, s2, tiles, p):
        first_sub = pad + i * n_rb + p * grp - band + 1
        w_rows = []
        carries = []
        for hd in range(2):
            for rr in range(grp):
                r = p * grp + rr
                c = jnp.zeros((sub, sub), F32)
                ws = [None] * band
                for s in reversed(range(band)):
                    index, yt, mask = tiles[p, hd, rr, s]
                    blk = s2[index * sub:(index + 1) * sub]
                    w = jnp.exp2(blk[:, :sub] + c - yt)
                    if mask is not None:
                        w = jnp.where(mask, w, 0.0)
                    ws[s] = w.astype(BF16)
                    c = c + blk[:, sub:]
                c_ref[hp, hd, r * sub:(r + 1) * sub, :] = c
                carries.append(c)
                w_rows.append(jnp.concatenate(
                    [no_weight] * rr + ws + [no_weight] * (grp - 1 - rr), axis=1))
        vwin = v_scr[hp, pl.ds(pl.multiple_of(first_sub * sub, sub), win * sub), :]
        pv = _dot(jnp.concatenate(w_rows, axis=0), vwin)
        rows = slice(p * grp_rows, (p + 1) * grp_rows)
        for hd in range(2):
            acc_ref[hp, hd, rows, :] = pv[hd * grp_rows:(hd + 1) * grp_rows]
        o_scr[hp, rows, :] = jnp.where(first_grp, pv[:grp_rows], pv[grp_rows:]).astype(BF16)
        return jnp.concatenate(
            [jnp.maximum(carries[rr], carries[grp + rr]) for rr in range(grp)], axis=0)

    def largest_open_carry(c, m):
        has_keys = row_blk + i * n_rb - band + 1 - m >= 0
        return jnp.max(jnp.where(has_keys, c, -jnp.inf))

    def extend_band(hp, q_heads, m):
        parts = []
        tiles = {}
        starts = []
        for p in range(pairs):
            first_sub = i * n_rb + p * grp - band + 1 - m
            start = jnp.maximum(pad + first_sub, 0)
            starts.append(start)
            kwin = jnp.concatenate([kt_scr[hp, start + s] for s in range(grp)], axis=1)
            q_both = jnp.concatenate(
                [q_heads[hd][p * grp_rows:(p + 1) * grp_rows] for hd in range(2)], axis=0)
            y = _dot(q_both, kwin)
            for hd in range(2):
                for rr in range(grp):
                    top = hd * grp_rows + rr * sub
                    yt = y[top:top + sub, rr * sub:(rr + 1) * sub]
                    in_sequence = first_sub + rr >= 0
                    part = jnp.where(in_sequence, _log2_one_minus_sigmoid(yt), 0.0)
                    hi, lo = _split_bf16(part)
                    tiles[p, hd, rr] = (len(parts), yt, in_sequence)
                    parts.append(jnp.concatenate([hi, lo], axis=1))
        s2 = _dot(jnp.concatenate(parts, axis=0), u)
        for p in range(pairs):
            w_rows = []
            for hd in range(2):
                for rr in range(grp):
                    rows = slice((p * grp + rr) * sub, (p * grp + rr + 1) * sub)
                    index, yt, in_sequence = tiles[p, hd, rr]
                    blk = s2[index * sub:(index + 1) * sub]
                    c = c_ref[hp, hd, rows, :]
                    w = jnp.where(in_sequence, jnp.exp2(blk[:, :sub] + c - yt), 0.0).astype(BF16)
                    c_ref[hp, hd, rows, :] = c + blk[:, sub:]
                    w_rows.append(jnp.concatenate(
                        [no_weight] * rr + [w] + [no_weight] * (grp - 1 - rr), axis=1))
            vwin = v_scr[hp, pl.ds(pl.multiple_of(starts[p] * sub, sub), grp * sub), :]
            pv = _dot(jnp.concatenate(w_rows, axis=0), vwin)
            rows = slice(p * grp_rows, (p + 1) * grp_rows)
            for hd in range(2):
                acc_ref[hp, hd, rows, :] += pv[hd * grp_rows:(hd + 1) * grp_rows]

    def further_left(hp):
        q_heads = split_heads(hp)
        m_last = i * n_rb + n_rb - band

        def more_keys_matter(state):
            m, c_max = state
            return jnp.logical_and(m <= m_last, c_max > SB_F32_EXP2_ZERO)

        def one_more_sub_block(state):
            m, _ = state
            extend_band(hp, q_heads, m)
            return m + 1, largest_open_carry(c_ref[hp], m + 1)

        lax.while_loop(more_keys_matter, one_more_sub_block,
                       (1, largest_open_carry(c_ref[hp], 1)))
        o_scr[hp] = jnp.where(first, acc_ref[hp, 0], acc_ref[hp, 1]).astype(BF16)

    def per_head_pair_duo(j, largest_carry):
        hps = [j * duo + e for e in range(duo)]
        xn_next = xn_buf[slot]
        for hp in hps:
            append_projected(hp)
        q_heads = [split_heads(hp) for hp in hps]
        stacked = [band_logits(hp, q_heads[e]) for e, hp in enumerate(hps)]
        project_q(xn_next, j, next_slot)
        project_k(xn_next, j)
        s2 = []
        for e in range(duo):
            s2.append(_dot(stacked[e][0], u))
            if e == 0:
                project_v(xn_next, j)
        for e, hp in enumerate(hps):
            carry = [band_weights(hp, s2[e], stacked[e][1], p) for p in range(pairs)]
            largest_carry = jnp.maximum(largest_carry, jnp.concatenate(carry, axis=0))
        rows = pl.ds(pl.multiple_of(j * piece, piece), piece)
        xn_buf[next_slot, rows, :] = _rms(x2_ref[rows, :], gain).astype(BF16)
        return largest_carry

    largest_carry = lax.fori_loop(0, n_hp // duo, per_head_pair_duo,
                                  jnp.full((t, sub), -jnp.inf, F32))

    @pl.when(largest_open_carry(largest_carry, 1) > SB_F32_EXP2_ZERO)
    def _():
        def visit(hp, carry):
            further_left(hp)
            return carry

        lax.fori_loop(0, n_hp, visit, 0)

    o_ref[...] = jnp.concatenate([o_scr[hp] for hp in range(n_hp)], axis=1)


def _sb_mixer(h, gain, w_qkv):
    b, s, d = h.shape
    t = SB_TILE
    n_hp = d // LANES
    duo = SB_HP_PER_STEP
    n_tiles = s // t
    n_subs = SB_PAD_SUBS + s // SB_SUB

    def tile_after_next(bi, i):
        tile = jnp.minimum(bi * n_tiles + i + 2, b * n_tiles - 1)
        return tile // n_tiles, tile % n_tiles, 0

    return pl.pallas_call(
        _sb_layer_kernel,
        grid=(b, n_tiles),
        in_specs=[
            pl.BlockSpec((None, t, d), lambda bi, i: (0, 0, 0), pipeline_mode=pl.Buffered(1)),
            pl.BlockSpec((None, t, d), lambda bi, i: (0, 1, 0), pipeline_mode=pl.Buffered(1)),
            pl.BlockSpec((None, t, d), tile_after_next),
            _resident((1, d)),
            _resident((d, 3 * d)),
            _resident((2 * SB_SUB, 2 * SB_SUB)),
        ],
        out_specs=pl.BlockSpec((None, t, d), lambda bi, i: (bi, i, 0)),
        out_shape=jax.ShapeDtypeStruct((b, s, d), BF16),
        scratch_shapes=[
            pltpu.VMEM((2, n_hp, t, LANES), BF16),
            pltpu.VMEM((n_hp, n_subs, LANES, SB_SUB), BF16),
            pltpu.VMEM((n_hp, n_subs * SB_SUB, LANES), BF16),
            pltpu.VMEM((n_hp, t // SB_SUB, LANES, SB_SUB), BF16),
            pltpu.VMEM((n_hp, t, LANES), BF16),
            pltpu.VMEM((2, t, d), BF16),
            pltpu.VMEM((n_hp, t, LANES), BF16),
            pltpu.VMEM((n_hp, 2, t, LANES), F32),
            pltpu.VMEM((n_hp, 2, t, SB_SUB), F32),
        ],
        compiler_params=_params(("arbitrary", "arbitrary")),
        name="sb_layer",
    )(h, h, h, gain, w_qkv, _sb_suffix_matrix())


def _conv_kernel(x_ref, xp_ref, g_ref, w_ref, cw_ref, a_ref, hbuf_ref):
    d = D_MODEL
    tm = PROJ_TILE
    i = pl.program_id(1)
    gain = g_ref[...]
    xn = _rms(x_ref[...], gain).astype(BF16)
    xpn = _rms(xp_ref[...], gain).astype(BF16)
    w_c = w_ref[:, d:2 * d].astype(BF16)
    w_u = w_ref[:, 2 * d:3 * d].astype(BF16)
    hbuf_ref[SUBLANES:, :] = _dot(xn, w_c) * _dot(xn, w_u)
    h_prev = _dot(xpn, w_c) * _dot(xpn, w_u)
    hbuf_ref[0:SUBLANES, :] = jnp.where(i > 0, h_prev, 0.0)
    conv = cw_ref[0:1, :] * hbuf_ref[pl.ds(SUBLANES - 2, tm), :]
    conv = conv + cw_ref[1:2, :] * hbuf_ref[pl.ds(SUBLANES - 1, tm), :]
    conv = conv + cw_ref[2:3, :] * hbuf_ref[pl.ds(SUBLANES, tm), :]
    a_ref[...] = (_dot(xn, w_ref[:, 0:d].astype(BF16)) * conv).astype(BF16)


def _conv_mixer(h, gain, w_in, conv_w, index):
    b, s, d = h.shape
    tm = PROJ_TILE
    rows_per_tile = tm // SUBLANES
    return pl.pallas_call(
        _conv_kernel,
        grid=(b, s // tm),
        in_specs=[
            pl.BlockSpec((None, tm, d), lambda bi, i: (bi, i, 0)),
            pl.BlockSpec((None, SUBLANES, d),
                         lambda bi, i: (bi, jnp.maximum(i * rows_per_tile - 1, 0), 0)),
            _resident((1, d)),
            _layer_slice((d, 3 * d), index),
            _layer_slice((CONV_WIDTH, d), index),
        ],
        out_specs=pl.BlockSpec((None, tm, d), lambda bi, i: (bi, i, 0)),
        out_shape=jax.ShapeDtypeStruct((b, s, d), BF16),
        scratch_shapes=[pltpu.VMEM((tm + SUBLANES, d), F32)],
        compiler_params=_params(("parallel", "parallel")),
        name="conv_mixer",
    )(h, h, gain, w_in, conv_w)


def _gla_proj_kernel(x_ref, g_ref, w_ref, wa_ref, wgu_ref, bg_ref,
                     q_ref, kt_ref, v_ref, og_ref, lg_ref, lgt_ref):
    dk, dv = GLA_DK, GLA_DV
    xn = _rms(x_ref[...], g_ref[...]).astype(BF16)
    a_low = _dot(xn, wa_ref[...]).astype(BF16)
    pre = _dot(a_low, wgu_ref[...]) + bg_ref[...]
    k = _dot(xn, w_ref[:, dk:2 * dk].astype(BF16))
    lg = _log_sigmoid(pre) * (LOG2E / GLA_GATE_NORMALIZER)
    lg_ref[...] = lg
    lgt_ref[...] = lg.T
    kt_ref[...] = k.T.astype(BF16)
    q_ref[...] = _dot(xn, w_ref[:, 0:dk].astype(BF16)).astype(BF16)
    v_ref[...] = _dot(xn, w_ref[:, 2 * dk:2 * dk + dv].astype(BF16)).astype(BF16)
    og_ref[...] = _dot(xn, w_ref[:, 2 * dk + dv:2 * dk + 2 * dv].astype(BF16)).astype(BF16)


def _gla_proj(h, gain, w_in, index, w_a, w_gu, b_gate):
    b, s, d = h.shape
    tm = PROJ_TILE
    dk, dv = GLA_DK, GLA_DV
    row = lambda bi, i: (bi, i, 0)
    colmajor = lambda bi, i: (bi, 0, i)
    return pl.pallas_call(
        _gla_proj_kernel,
        grid=(b, s // tm),
        in_specs=[
            pl.BlockSpec((None, tm, d), row),
            _resident((1, d)),
            _layer_slice((d, 2 * dk + 2 * dv), index),
            _resident((d, LANES)),
            _resident((LANES, dk)),
            _resident((1, dk)),
        ],
        out_specs=[
            pl.BlockSpec((None, tm, dk), row),
            pl.BlockSpec((None, dk, tm), colmajor),
            pl.BlockSpec((None, tm, dv), row),
            pl.BlockSpec((None, tm, dv), row),
            pl.BlockSpec((None, tm, dk), row),
            pl.BlockSpec((None, dk, tm), colmajor),
        ],
        out_shape=[
            jax.ShapeDtypeStruct((b, s, dk), BF16),
            jax.ShapeDtypeStruct((b, dk, s), BF16),
            jax.ShapeDtypeStruct((b, s, dv), BF16),
            jax.ShapeDtypeStruct((b, s, dv), BF16),
            jax.ShapeDtypeStruct((b, s, dk), F32),
            jax.ShapeDtypeStruct((b, dk, s), F32),
        ],
        compiler_params=_params(("parallel", "parallel")),
        name="gla_proj",
    )(h, gain, w_in, w_a, w_gu, b_gate)


def _gla_constants():
    c = GLA_TILE
    i = np.arange(c)[:, None]
    m = np.arange(c)[None, :]
    fwd, rev, masks = [], [], []
    for level in range(1, GLA_LEVELS + 1):
        n = 1 << level
        same = (i // n) == (m // n)
        fwd.append((same & (m <= i)).astype(np.float32))
        rev.append((same & (m > i)).astype(np.float32).T)
    rev.append(np.ones((c, c), np.float32))
    for level in range(GLA_LEVELS):
        n = 1 << level
        masks.append(((i // (2 * n) == m // (2 * n)) & ((i // n) % 2 == 1)
                      & ((m // n) % 2 == 0)).astype(np.float32))
    masks.append((i == m).astype(np.float32))
    fwd = np.concatenate(fwd, axis=0)
    rev = np.concatenate(rev, axis=1)
    fwd2 = np.concatenate([fwd, fwd], axis=1)
    rev2 = np.concatenate([rev, rev], axis=0)
    return (jnp.asarray(fwd2, dtype=BF16), jnp.asarray(rev2, dtype=BF16),
            jnp.asarray(np.stack(masks), dtype=F32))


def _gla_chunk_kernel(q_ref, kt_ref, v_ref, og_ref, lg_ref, lgt_ref, fwd_ref, rev_ref,
                      mask_ref, hn_ref, a_ref, state_ref):
    c = GLA_TILE
    n_ch = GLA_STEP_CHUNKS
    dk = GLA_DK
    dkh, dvh = GLA_DK_HEAD, GLA_DV_HEAD
    n_lv = GLA_LEVELS

    @pl.when(pl.program_id(1) == 0)
    def _():
        state_ref[...] = jnp.zeros_like(state_ref)

    lg = lg_ref[...]
    hi, lo = _split_bf16(lg)
    g_cat = jnp.concatenate(
        [jnp.concatenate([hi[ch * c:(ch + 1) * c], lo[ch * c:(ch + 1) * c]], axis=0)
         for ch in range(n_ch)], axis=1)
    fwd_all = _dot(fwd_ref[...], g_cat)
    hi_t, lo_t = _split_bf16(lgt_ref[...])
    gt_cat = jnp.concatenate(
        [jnp.concatenate([hi_t[:, ch * c:(ch + 1) * c], lo_t[:, ch * c:(ch + 1) * c]], axis=1)
         for ch in range(n_ch)], axis=0)
    rev_all = _dot(gt_cat, rev_ref[...])

    for hd in range(GLA_HEADS):
        ks = slice(hd * dkh, (hd + 1) * dkh)
        vs = slice(hd * dvh, (hd + 1) * dvh)
        state = state_ref[hd]
        for ch in range(n_ch):
            rows = slice(ch * c, (ch + 1) * c)
            kcols = slice(ch * dk + hd * dkh, ch * dk + (hd + 1) * dkh)
            q = q_ref[rows, ks].astype(F32) * GLA_SCALE
            kt = kt_ref[ks, rows].astype(F32)
            v = v_ref[rows, vs]
            kt_bf = kt.astype(BF16)
            scores = _dot(q.astype(BF16), kt_bf) * mask_ref[n_lv]
            qd = q * jnp.exp2(lg[rows, ks])
            scores = scores + _dot(qd.astype(BF16), kt_bf) * mask_ref[0]
            for lv in range(1, n_lv):
                qd = q * jnp.exp2(fwd_all[(lv - 1) * c:lv * c, kcols])
                kd = kt * jnp.exp2(rev_all[kcols, (lv - 1) * c:lv * c])
                scores = scores + _dot(qd.astype(BF16), kd.astype(BF16)) * mask_ref[lv]
            q_cum = q * jnp.exp2(fwd_all[(n_lv - 1) * c:n_lv * c, kcols])
            out = _dot(jnp.concatenate([q_cum.astype(BF16), scores.astype(BF16)], axis=1),
                       jnp.concatenate([state.astype(BF16), v], axis=0))
            k_rest = kt * jnp.exp2(rev_all[kcols, (n_lv - 1) * c:n_lv * c])
            keep = jnp.exp2(rev_all[kcols, n_lv * c:(n_lv + 1) * c])
            keep = jnp.concatenate([keep] * (dvh // c), axis=1)
            state = keep * state + _dot(k_rest.astype(BF16), v)
            o = _rms(out, hn_ref[:, vs])
            og = og_ref[rows, vs].astype(F32)
            a_ref[rows, vs] = (o * (og * (1.0 / (1.0 + jnp.exp(-og))))).astype(BF16)
        state_ref[hd] = state


def _gla_chunk(q, kt, v, og, lg, lgt, head_norm):
    b, s, dk = q.shape
    dv = v.shape[-1]
    c = GLA_TILE * GLA_STEP_CHUNKS
    fwd, rev, masks = _gla_constants()
    row = lambda bi, i: (bi, i, 0)
    colmajor = lambda bi, i: (bi, 0, i)
    return pl.pallas_call(
        _gla_chunk_kernel,
        grid=(b, s // c),
        in_specs=[
            pl.BlockSpec((None, c, dk), row),
            pl.BlockSpec((None, dk, c), colmajor),
            pl.BlockSpec((None, c, dv), row),
            pl.BlockSpec((None, c, dv), row),
            pl.BlockSpec((None, c, dk), row),
            pl.BlockSpec((None, dk, c), colmajor),
            _resident(fwd.shape),
            _resident(rev.shape),
            _resident(masks.shape),
            _resident((1, dv)),
        ],
        out_specs=pl.BlockSpec((None, c, dv), row),
        out_shape=jax.ShapeDtypeStruct((b, s, dv), BF16),
        scratch_shapes=[pltpu.VMEM((GLA_HEADS, GLA_DK_HEAD, GLA_DV_HEAD), F32)],
        compiler_params=_params(("parallel", "arbitrary")),
        name="gla_chunk",
    )(q, kt, v, og, lg, lgt, fwd, rev, masks, head_norm)


def _post_kernel(a_ref, h_ref, wo_ref, g_ref, wup_ref, wdn_ref, o_ref, acc_ref):
    half = TOKEN_TILE // 2
    halves = (slice(0, half), slice(half, TOKEN_TILE))
    w_o = wo_ref[...].astype(BF16)
    h1 = []
    xn = []
    for rows in halves:
        mixed = _dot(a_ref[rows, :], w_o)
        h1.append(h_ref[rows, :] + _rms(mixed, g_ref[1:2, :]))
        xn.append(_rms(h1[-1], g_ref[2:3, :]).astype(BF16))
    xn_full = jnp.concatenate(xn, axis=0)
    n_chunks = D_FF // FF_CHUNK
    for ci in range(n_chunks):
        cols = slice(ci * FF_CHUNK, (ci + 1) * FF_CHUNK)
        w_up = wup_ref[:, cols].astype(BF16)
        w_down = wdn_ref[cols, :].astype(BF16)
        if ci == 0:
            up = jnp.concatenate([_dot(part, w_up) for part in xn], axis=0)
        else:
            up = _dot(xn_full, w_up)
        up = jnp.maximum(up, 0.0)
        act = (up * up).astype(BF16)
        if ci == 0:
            acc_ref[...] = _dot(act, w_down)
        elif ci < n_chunks - 1:
            acc_ref[...] += _dot(act, w_down)
        else:
            for hi, rows in enumerate(halves):
                ffn = acc_ref[rows, :] + _dot(act[rows, :], w_down)
                o_ref[rows, :] = h1[hi] + _rms(ffn, g_ref[3:4, :])


def _layer_slice(shape, index):
    zeros = (0,) * len(shape)
    return pl.BlockSpec((None,) + tuple(shape), lambda *_: (index,) + zeros,
                        pipeline_mode=pl.Buffered(1))


def _post(a, h, w_o, mixer_index, gains, w_up, w_down, layer):
    b, s, d = h.shape
    tm = TOKEN_TILE
    row = lambda bi, i: (bi, i, 0)
    return pl.pallas_call(
        _post_kernel,
        grid=(b, s // tm),
        in_specs=[
            pl.BlockSpec((None, tm, d), row),
            pl.BlockSpec((None, tm, d), row),
            _layer_slice((d, d), mixer_index),
            _layer_slice((4, d), layer),
            _layer_slice((d, D_FF), layer),
            _layer_slice((D_FF, d), layer),
        ],
        out_specs=pl.BlockSpec((None, tm, d), row),
        out_shape=jax.ShapeDtypeStruct((b, s, d), F32),
        scratch_shapes=[pltpu.VMEM((tm, d), F32)],
        compiler_params=_params(("parallel", "parallel")),
        name="post_ffn",
    )(a, h, w_o, gains, w_up, w_down)


def kernel(x, norm_gains, sb_w_qkv, sb_w_o, conv_w_in, conv_w, conv_w_out, gla_w_in,
           gla_w_gate_up, gla_b_gate, gla_head_norm, gla_w_o, ffn_w_up, ffn_w_down):
    depth = norm_gains.shape[0]
    h = x
    for i in range(depth):
        kind, j = i % 3, i // 3
        gains = norm_gains[i]
        pre_gain = gains[0:1]
        if kind == 0:
            a = _sb_mixer(h, pre_gain, sb_w_qkv[j].astype(BF16))
            w_o = sb_w_o
        elif kind == 1:
            a = _conv_mixer(h, pre_gain, conv_w_in, conv_w, j)
            w_o = conv_w_out
        else:
            n_main = 2 * GLA_DK + 2 * GLA_DV
            w_in = gla_w_in[j]
            w_a = jnp.pad(w_in[:, n_main:], ((0, 0), (0, LANES - GLA_GATE_RANK))).astype(BF16)
            w_gu = jnp.pad(gla_w_gate_up[j], ((0, LANES - GLA_GATE_RANK), (0, 0))).astype(BF16)
            q, kt, v, og, lg, lgt = _gla_proj(h, pre_gain, gla_w_in, j, w_a, w_gu,
                                              gla_b_gate[j][None, :])
            a = _gla_chunk(q, kt, v, og, lg, lgt, gla_head_norm[j].reshape(1, GLA_DV))
            w_o = gla_w_o
        h = _post(a, h, w_o, j, norm_gains, ffn_w_up, ffn_w_down, i)
    return h
```

```python
import math

import numpy as np
import jax
import jax.numpy as jnp
from jax import lax
from jax.experimental import pallas as pl
from jax.experimental.pallas import tpu as pltpu

F32 = jnp.float32
BF16 = jnp.bfloat16

D_MODEL = 1024
D_FF = 4 * D_MODEL
RMS_EPS = 1e-6

SB_HEADS = 16
SB_HEAD_DIM = D_MODEL // SB_HEADS
LOG2E = math.log2(math.e)
SB_Q_SCALE = -(SB_HEAD_DIM ** -0.5) * LOG2E
CONV_WIDTH = 3
GLA_HEADS = 4
GLA_DK = D_MODEL // 2
GLA_DV = D_MODEL
GLA_DK_HEAD = GLA_DK // GLA_HEADS
GLA_DV_HEAD = GLA_DV // GLA_HEADS
GLA_GATE_RANK = 16
GLA_GATE_NORMALIZER = 16.0
GLA_SCALE = GLA_DK_HEAD ** -0.5

LANES = 128
SUBLANES = 8
VMEM_LIMIT_BYTES = 58 * 1024 * 1024

TOKEN_TILE = 512
PROJ_TILE = 1024
FF_CHUNK = 1024
SB_TILE = 512
SB_SUB = 128
SB_BAND_SUBS = 3
SB_GROUP = 1
SB_PAD_SUBS = SB_BAND_SUBS - 1
SB_HP_PER_STEP = 2
SB_F32_EXP2_ZERO = -152.0
GLA_TILE = 128
GLA_STEP_CHUNKS = 2
GLA_LEVELS = 7


def _rms(x, gain):
    ms = jnp.mean(x * x, axis=-1, keepdims=True)
    return x * lax.rsqrt(ms + RMS_EPS) * gain


def _neg_abs(z):
    bits = lax.bitcast_convert_type(z, jnp.uint32) | jnp.uint32(0x80000000)
    return lax.bitcast_convert_type(bits, F32)


def _log_sigmoid(z):
    return jnp.minimum(z, 0.0) - jnp.log(1.0 + jnp.exp(_neg_abs(z)))


def _log2_one_minus_sigmoid(y):
    return jnp.minimum(y, 0.0) - jnp.log(1.0 + jnp.exp2(_neg_abs(y))) * LOG2E


def _split_bf16(x):
    hi = x.astype(BF16)
    lo = (x - hi.astype(F32)).astype(BF16)
    return hi, lo


def _dot(a, b):
    return jnp.dot(a, b, preferred_element_type=F32)


def _resident(shape):
    zeros = (0,) * len(shape)
    return pl.BlockSpec(shape, lambda *_: zeros, pipeline_mode=pl.Buffered(1))


def _params(semantics):
    return pltpu.CompilerParams(dimension_semantics=semantics,
                                vmem_limit_bytes=VMEM_LIMIT_BYTES)


def _sb_suffix_matrix():
    t = SB_SUB
    m = np.arange(t)[:, None]
    j = np.arange(t)[None, :]
    half = np.concatenate([(m >= j).astype(np.float32), np.ones((t, t), np.float32)], axis=1)
    return jnp.asarray(np.concatenate([half, half], axis=0), dtype=BF16)


def _sb_layer_kernel(x0_ref, x1_ref, x2_ref, g_ref, w_ref, u_ref, o_ref,
                     q_scr, kt_scr, v_scr, kt_new, v_new, xn_buf, o_scr, acc_ref, c_ref, open_ref):
    t = SB_TILE
    sub = SB_SUB
    n_rb = t // sub
    band = SB_BAND_SUBS
    pad = SB_PAD_SUBS
    grp = SB_GROUP
    win = band + grp - 1
    pairs = n_rb // grp
    grp_rows = grp * sub
    n_hp = D_MODEL // LANES
    duo = SB_HP_PER_STEP
    i = pl.program_id(1)
    step = pl.program_id(0) * pl.num_programs(1) + i
    gain = g_ref[...]
    u = u_ref[...]

    def weight_columns(which, j):
        width = duo * LANES
        return w_ref[:, pl.ds(pl.multiple_of(which * D_MODEL + j * width, width), width)]

    def project_q(xn, j, slot):
        y = _dot(xn, weight_columns(0, j)) * SB_Q_SCALE
        for e in range(duo):
            q_scr[slot, j * duo + e] = y[:, e * LANES:(e + 1) * LANES].astype(BF16)

    def project_k(xn, j):
        y = _dot(xn, weight_columns(1, j))
        for e in range(duo):
            for kb in range(n_rb):
                kt_new[j * duo + e, kb] = (
                    y[kb * sub:(kb + 1) * sub, e * LANES:(e + 1) * LANES].T.astype(BF16))

    def project_v(xn, j):
        y = _dot(xn, weight_columns(2, j)).astype(BF16)
        for e in range(duo):
            v_new[j * duo + e] = y[:, e * LANES:(e + 1) * LANES]

    @pl.when(i == 0)
    def _():
        kt_scr[:, 0:pad] = jnp.zeros((n_hp, pad, LANES, sub), BF16)
        v_scr[:, 0:pad * sub, :] = jnp.zeros((n_hp, pad * sub, LANES), BF16)

    @pl.when(step == 0)
    def _():
        xn = _rms(x0_ref[...], gain).astype(BF16)

        def first_tile(j, carry):
            project_q(xn, j, 0)
            project_k(xn, j)
            project_v(xn, j)
            return carry

        lax.fori_loop(0, n_hp // duo, first_tile, 0)
        xn_buf[0] = _rms(x1_ref[...], gain).astype(BF16)

    slot = step % 2
    next_slot = (step + 1) % 2
    piece = t // (n_hp // duo)

    lane = lax.broadcasted_iota(jnp.int32, (t, LANES), 1)
    first = lane < SB_HEAD_DIM
    first_grp = lax.broadcasted_iota(jnp.int32, (grp_rows, LANES), 1) < SB_HEAD_DIM
    row = lax.broadcasted_iota(jnp.int32, (sub, sub), 0)
    col = lax.broadcasted_iota(jnp.int32, (sub, sub), 1)
    diag = col < row
    no_weight = jnp.zeros((sub, sub), BF16)
    row_blk = lax.broadcasted_iota(jnp.int32, (t, sub), 0) // sub
    not_first_tile = i > 0

    def append_projected(hp):
        for kb in range(n_rb):
            kt_scr[hp, pad + i * n_rb + kb] = kt_new[hp, kb]
        v_scr[hp, pl.ds(pl.multiple_of((pad + i * n_rb) * sub, sub), t), :] = v_new[hp]

    def split_heads(hp):
        q = q_scr[slot, hp]
        zero = jnp.zeros_like(q)
        return jnp.where(first, q, zero), jnp.where(first, zero, q)

    def band_logits(hp, q_heads):
        parts = []
        tiles = {}
        for p in range(pairs):
            first_sub = pad + i * n_rb + p * grp - band + 1
            kwin = jnp.concatenate([kt_scr[hp, first_sub + s] for s in range(win)], axis=1)
            q_both = jnp.concatenate(
                [q_heads[hd][p * grp_rows:(p + 1) * grp_rows] for hd in range(2)], axis=0)
            y = _dot(q_both, kwin)
            for hd in range(2):
                for rr in range(grp):
                    top = hd * grp_rows + rr * sub
                    for s in range(band):
                        yt = y[top:top + sub, (rr + s) * sub:(rr + s + 1) * sub]
                        part = _log2_one_minus_sigmoid(yt)
                        mask = diag if s == band - 1 else None
                        if p * grp + rr + s < band - 1:
                            mask = not_first_tile if mask is None else mask & not_first_tile
                        if mask is not None:
                            part = jnp.where(mask, part, 0.0)
                        hi, lo = _split_bf16(part)
                        tiles[p, hd, rr, s] = (len(parts), yt, mask)
                        parts.append(jnp.concatenate([hi, lo], axis=1))
        return jnp.concatenate(parts, axis=0), tiles

    def band_weights(hp, s2, tiles, p):
        first_sub = pad + i * n_rb + p * grp - band + 1
        w_rows = []
        carries = []
        for hd in range(2):
            for rr in range(grp):
                r = p * grp + rr
                c = jnp.zeros((sub, sub), F32)
                ws = [None] * band
                for s in reversed(range(band)):
                    index, yt, mask = tiles[p, hd, rr, s]
                    blk = s2[index * sub:(index + 1) * sub]
                    w = jnp.exp2(blk[:, :sub] + c - yt)
                    if mask is not None:
                        w = jnp.where(mask, w, 0.0)
                    ws[s] = w.astype(BF16)
                    c = c + blk[:, sub:]
                c_ref[hp, hd, r * sub:(r + 1) * sub, :] = c
                carries.append(c)
                w_rows.append(jnp.concatenate(
                    [no_weight] * rr + ws + [no_weight] * (grp - 1 - rr), axis=1))
        vwin = v_scr[hp, pl.ds(pl.multiple_of(first_sub * sub, sub), win * sub), :]
        pv = _dot(jnp.concatenate(w_rows, axis=0), vwin)
        rows = slice(p * grp_rows, (p + 1) * grp_rows)
        for hd in range(2):
            acc_ref[hp, hd, rows, :] = pv[hd * grp_rows:(hd + 1) * grp_rows]
        o_scr[hp, rows, :] = jnp.where(first_grp, pv[:grp_rows], pv[grp_rows:]).astype(BF16)
        return jnp.concatenate(
            [jnp.maximum(carries[rr], carries[grp + rr]) for rr in range(grp)], axis=0)

    def largest_open_carry(c, m):
        has_keys = row_blk + i * n_rb - band + 1 - m >= 0
        return jnp.max(jnp.where(has_keys, c, -jnp.inf))

    def extend_band(hp, q_heads, m):
        parts = []
        tiles = {}
        starts = []
        for p in range(pairs):
            first_sub = i * n_rb + p * grp - band + 1 - m
            start = jnp.maximum(pad + first_sub, 0)
            starts.append(start)
            kwin = jnp.concatenate([kt_scr[hp, start + s] for s in range(grp)], axis=1)
            q_both = jnp.concatenate(
                [q_heads[hd][p * grp_rows:(p + 1) * grp_rows] for hd in range(2)], axis=0)
            y = _dot(q_both, kwin)
            for hd in range(2):
                for rr in range(grp):
                    top = hd * grp_rows + rr * sub
                    yt = y[top:top + sub, rr * sub:(rr + 1) * sub]
                    in_sequence = first_sub + rr >= 0
                    part = jnp.where(in_sequence, _log2_one_minus_sigmoid(yt), 0.0)
                    hi, lo = _split_bf16(part)
                    tiles[p, hd, rr] = (len(parts), yt, in_sequence)
                    parts.append(jnp.concatenate([hi, lo], axis=1))
        s2 = _dot(jnp.concatenate(parts, axis=0), u)
        for p in range(pairs):
            w_rows = []
            for hd in range(2):
                for rr in range(grp):
                    rows = slice((p * grp + rr) * sub, (p * grp + rr + 1) * sub)
                    index, yt, in_sequence = tiles[p, hd, rr]
                    blk = s2[index * sub:(index + 1) * sub]
                    c = c_ref[hp, hd, rows, :]
                    w = jnp.where(in_sequence, jnp.exp2(blk[:, :sub] + c - yt), 0.0).astype(BF16)
                    c_ref[hp, hd, rows, :] = c + blk[:, sub:]
                    w_rows.append(jnp.concatenate(
                        [no_weight] * rr + [w] + [no_weight] * (grp - 1 - rr), axis=1))
            vwin = v_scr[hp, pl.ds(pl.multiple_of(starts[p] * sub, sub), grp * sub), :]
            pv = _dot(jnp.concatenate(w_rows, axis=0), vwin)
            rows = slice(p * grp_rows, (p + 1) * grp_rows)
            for hd in range(2):
                acc_ref[hp, hd, rows, :] += pv[hd * grp_rows:(hd + 1) * grp_rows]

    def further_left(hp):
        q_heads = split_heads(hp)
        m_last = i * n_rb + n_rb - band

        def more_keys_matter(state):
            m, c_max = state
            return jnp.logical_and(m <= m_last, c_max > SB_F32_EXP2_ZERO)

        def one_more_sub_block(state):
            m, _ = state
            extend_band(hp, q_heads, m)
            return m + 1, largest_open_carry(c_ref[hp], m + 1)

        lax.while_loop(more_keys_matter, one_more_sub_block, (1, open_ref[hp]))
        o_scr[hp] = jnp.where(first, acc_ref[hp, 0], acc_ref[hp, 1]).astype(BF16)

    def per_head_pair_duo(j, largest_carry):
        hps = [j * duo + e for e in range(duo)]
        xn_next = xn_buf[slot]
        for hp in hps:
            append_projected(hp)
        q_heads = [split_heads(hp) for hp in hps]
        stacked = [band_logits(hp, q_heads[e]) for e, hp in enumerate(hps)]
        project_q(xn_next, j, next_slot)
        project_k(xn_next, j)
        s2 = []
        for e in range(duo):
            s2.append(_dot(stacked[e][0], u))
            if e == 0:
                project_v(xn_next, j)
        for e, hp in enumerate(hps):
            carry = [band_weights(hp, s2[e], stacked[e][1], p) for p in range(pairs)]
            largest_carry = jnp.maximum(largest_carry, jnp.concatenate(carry, axis=0))
        rows = pl.ds(pl.multiple_of(j * piece, piece), piece)
        xn_buf[next_slot, rows, :] = _rms(x2_ref[rows, :], gain).astype(BF16)
        return largest_carry

    largest_carry = lax.fori_loop(0, n_hp // duo, per_head_pair_duo,
                                  jnp.full((t, sub), -jnp.inf, F32))

    @pl.when(largest_open_carry(largest_carry, 1) > SB_F32_EXP2_ZERO)
    def _():
        for hp in range(n_hp):
            open_ref[hp] = largest_open_carry(c_ref[hp], 1)

        def visit(hp, carry):
            further_left(hp)
            return carry

        lax.fori_loop(0, n_hp, visit, 0)

    o_ref[...] = jnp.concatenate([o_scr[hp] for hp in range(n_hp)], axis=1)


def _sb_mixer(h, gain, w_qkv):
    b, s, d = h.shape
    t = SB_TILE
    n_hp = d // LANES
    duo = SB_HP_PER_STEP
    n_tiles = s // t
    n_subs = SB_PAD_SUBS + s // SB_SUB

    def tile_after_next(bi, i):
        tile = jnp.minimum(bi * n_tiles + i + 2, b * n_tiles - 1)
        return tile // n_tiles, tile % n_tiles, 0

    return pl.pallas_call(
        _sb_layer_kernel,
        grid=(b, n_tiles),
        in_specs=[
            pl.BlockSpec((None, t, d), lambda bi, i: (0, 0, 0), pipeline_mode=pl.Buffered(1)),
            pl.BlockSpec((None, t, d), lambda bi, i: (0, 1, 0), pipeline_mode=pl.Buffered(1)),
            pl.BlockSpec((None, t, d), tile_after_next),
            _resident((1, d)),
            _resident((d, 3 * d)),
            _resident((2 * SB_SUB, 2 * SB_SUB)),
        ],
        out_specs=pl.BlockSpec((None, t, d), lambda bi, i: (bi, i, 0)),
        out_shape=jax.ShapeDtypeStruct((b, s, d), BF16),
        scratch_shapes=[
            pltpu.VMEM((2, n_hp, t, LANES), BF16),
            pltpu.VMEM((n_hp, n_subs, LANES, SB_SUB), BF16),
            pltpu.VMEM((n_hp, n_subs * SB_SUB, LANES), BF16),
            pltpu.VMEM((n_hp, t // SB_SUB, LANES, SB_SUB), BF16),
            pltpu.VMEM((n_hp, t, LANES), BF16),
            pltpu.VMEM((2, t, d), BF16),
            pltpu.VMEM((n_hp, t, LANES), BF16),
            pltpu.VMEM((n_hp, 2, t, LANES), F32),
            pltpu.VMEM((n_hp, 2, t, SB_SUB), F32),
            pltpu.SMEM((n_hp,), F32),
        ],
        compiler_params=_params(("arbitrary", "arbitrary")),
        name="sb_layer",
    )(h, h, h, gain, w_qkv, _sb_suffix_matrix())


def _conv_kernel(x_ref, xp_ref, g_ref, w_ref, cw_ref, a_ref, hbuf_ref):
    d = D_MODEL
    tm = PROJ_TILE
    i = pl.program_id(1)
    gain = g_ref[...]
    xn = _rms(x_ref[...], gain).astype(BF16)
    xpn = _rms(xp_ref[...], gain).astype(BF16)
    w_c = w_ref[:, d:2 * d].astype(BF16)
    w_u = w_ref[:, 2 * d:3 * d].astype(BF16)
    hbuf_ref[SUBLANES:, :] = _dot(xn, w_c) * _dot(xn, w_u)
    h_prev = _dot(xpn, w_c) * _dot(xpn, w_u)
    hbuf_ref[0:SUBLANES, :] = jnp.where(i > 0, h_prev, 0.0)
    conv = cw_ref[0:1, :] * hbuf_ref[pl.ds(SUBLANES - 2, tm), :]
    conv = conv + cw_ref[1:2, :] * hbuf_ref[pl.ds(SUBLANES - 1, tm), :]
    conv = conv + cw_ref[2:3, :] * hbuf_ref[pl.ds(SUBLANES, tm), :]
    a_ref[...] = (_dot(xn, w_ref[:, 0:d].astype(BF16)) * conv).astype(BF16)


def _conv_mixer(h, gain, w_in, conv_w, index):
    b, s, d = h.shape
    tm = PROJ_TILE
    rows_per_tile = tm // SUBLANES
    return pl.pallas_call(
        _conv_kernel,
        grid=(b, s // tm),
        in_specs=[
            pl.BlockSpec((None, tm, d), lambda bi, i: (bi, i, 0)),
            pl.BlockSpec((None, SUBLANES, d),
                         lambda bi, i: (bi, jnp.maximum(i * rows_per_tile - 1, 0), 0)),
            _resident((1, d)),
            _layer_slice((d, 3 * d), index),
            _layer_slice((CONV_WIDTH, d), index),
        ],
        out_specs=pl.BlockSpec((None, tm, d), lambda bi, i: (bi, i, 0)),
        out_shape=jax.ShapeDtypeStruct((b, s, d), BF16),
        scratch_shapes=[pltpu.VMEM((tm + SUBLANES, d), F32)],
        compiler_params=_params(("parallel", "parallel")),
        name="conv_mixer",
    )(h, h, gain, w_in, conv_w)


def _gla_proj_kernel(x_ref, g_ref, w_ref, wa_ref, wgu_ref, bg_ref,
                     q_ref, kt_ref, v_ref, og_ref, lg_ref, lgt_ref):
    dk, dv = GLA_DK, GLA_DV
    xn = _rms(x_ref[...], g_ref[...]).astype(BF16)
    a_low = _dot(xn, wa_ref[...]).astype(BF16)
    pre = _dot(a_low, wgu_ref[...]) + bg_ref[...]
    k = _dot(xn, w_ref[:, dk:2 * dk].astype(BF16))
    lg = _log_sigmoid(pre) * (LOG2E / GLA_GATE_NORMALIZER)
    lg_ref[...] = lg
    lgt_ref[...] = lg.T
    kt_ref[...] = k.T.astype(BF16)
    q_ref[...] = _dot(xn, w_ref[:, 0:dk].astype(BF16)).astype(BF16)
    v_ref[...] = _dot(xn, w_ref[:, 2 * dk:2 * dk + dv].astype(BF16)).astype(BF16)
    og_ref[...] = _dot(xn, w_ref[:, 2 * dk + dv:2 * dk + 2 * dv].astype(BF16)).astype(BF16)


def _gla_proj(h, gain, w_in, index, w_a, w_gu, b_gate):
    b, s, d = h.shape
    tm = PROJ_TILE
    dk, dv = GLA_DK, GLA_DV
    row = lambda bi, i: (bi, i, 0)
    colmajor = lambda bi, i: (bi, 0, i)
    return pl.pallas_call(
        _gla_proj_kernel,
        grid=(b, s // tm),
        in_specs=[
            pl.BlockSpec((None, tm, d), row),
            _resident((1, d)),
            _layer_slice((d, 2 * dk + 2 * dv), index),
            _resident((d, LANES)),
            _resident((LANES, dk)),
            _resident((1, dk)),
        ],
        out_specs=[
            pl.BlockSpec((None, tm, dk), row),
            pl.BlockSpec((None, dk, tm), colmajor),
            pl.BlockSpec((None, tm, dv), row),
            pl.BlockSpec((None, tm, dv), row),
            pl.BlockSpec((None, tm, dk), row),
            pl.BlockSpec((None, dk, tm), colmajor),
        ],
        out_shape=[
            jax.ShapeDtypeStruct((b, s, dk), BF16),
            jax.ShapeDtypeStruct((b, dk, s), BF16),
            jax.ShapeDtypeStruct((b, s, dv), BF16),
            jax.ShapeDtypeStruct((b, s, dv), BF16),
            jax.ShapeDtypeStruct((b, s, dk), F32),
            jax.ShapeDtypeStruct((b, dk, s), F32),
        ],
        compiler_params=_params(("parallel", "parallel")),
        name="gla_proj",
    )(h, gain, w_in, w_a, w_gu, b_gate)


def _gla_constants():
    c = GLA_TILE
    i = np.arange(c)[:, None]
    m = np.arange(c)[None, :]
    fwd, rev, masks = [], [], []
    for level in range(1, GLA_LEVELS + 1):
        n = 1 << level
        same = (i // n) == (m // n)
        fwd.append((same & (m <= i)).astype(np.float32))
        rev.append((same & (m > i)).astype(np.float32).T)
    rev.append(np.ones((c, c), np.float32))
    for level in range(GLA_LEVELS):
        n = 1 << level
        masks.append(((i // (2 * n) == m // (2 * n)) & ((i // n) % 2 == 1)
                      & ((m // n) % 2 == 0)).astype(np.float32))
    masks.append((i == m).astype(np.float32))
    fwd = np.concatenate(fwd, axis=0)
    rev = np.concatenate(rev, axis=1)
    fwd2 = np.concatenate([fwd, fwd], axis=1)
    rev2 = np.concatenate([rev, rev], axis=0)
    return (jnp.asarray(fwd2, dtype=BF16), jnp.asarray(rev2, dtype=BF16),
            jnp.asarray(np.stack(masks), dtype=F32))


def _gla_chunk_kernel(q_ref, kt_ref, v_ref, og_ref, lg_ref, lgt_ref, fwd_ref, rev_ref,
                      mask_ref, hn_ref, a_ref, state_ref):
    c = GLA_TILE
    n_ch = GLA_STEP_CHUNKS
    dk = GLA_DK
    dkh, dvh = GLA_DK_HEAD, GLA_DV_HEAD
    n_lv = GLA_LEVELS

    @pl.when(pl.program_id(1) == 0)
    def _():
        state_ref[...] = jnp.zeros_like(state_ref)

    lg = lg_ref[...]
    hi, lo = _split_bf16(lg)
    g_cat = jnp.concatenate(
        [jnp.concatenate([hi[ch * c:(ch + 1) * c], lo[ch * c:(ch + 1) * c]], axis=0)
         for ch in range(n_ch)], axis=1)
    fwd_all = _dot(fwd_ref[...], g_cat)
    hi_t, lo_t = _split_bf16(lgt_ref[...])
    gt_cat = jnp.concatenate(
        [jnp.concatenate([hi_t[:, ch * c:(ch + 1) * c], lo_t[:, ch * c:(ch + 1) * c]], axis=1)
         for ch in range(n_ch)], axis=0)
    rev_all = _dot(gt_cat, rev_ref[...])

    for hd in range(GLA_HEADS):
        ks = slice(hd * dkh, (hd + 1) * dkh)
        vs = slice(hd * dvh, (hd + 1) * dvh)
        state = state_ref[hd]
        for ch in range(n_ch):
            rows = slice(ch * c, (ch + 1) * c)
            kcols = slice(ch * dk + hd * dkh, ch * dk + (hd + 1) * dkh)
            q = q_ref[rows, ks].astype(F32) * GLA_SCALE
            kt = kt_ref[ks, rows].astype(F32)
            v = v_ref[rows, vs]
            kt_bf = kt.astype(BF16)
            scores = _dot(q.astype(BF16), kt_bf) * mask_ref[n_lv]
            qd = q * jnp.exp2(lg[rows, ks])
            scores = scores + _dot(qd.astype(BF16), kt_bf) * mask_ref[0]
            for lv in range(1, n_lv):
                qd = q * jnp.exp2(fwd_all[(lv - 1) * c:lv * c, kcols])
                kd = kt * jnp.exp2(rev_all[kcols, (lv - 1) * c:lv * c])
                scores = scores + _dot(qd.astype(BF16), kd.astype(BF16)) * mask_ref[lv]
            q_cum = q * jnp.exp2(fwd_all[(n_lv - 1) * c:n_lv * c, kcols])
            out = _dot(jnp.concatenate([q_cum.astype(BF16), scores.astype(BF16)], axis=1),
                       jnp.concatenate([state.astype(BF16), v], axis=0))
            k_rest = kt * jnp.exp2(rev_all[kcols, (n_lv - 1) * c:n_lv * c])
            keep = jnp.exp2(rev_all[kcols, n_lv * c:(n_lv + 1) * c])
            keep = jnp.concatenate([keep] * (dvh // c), axis=1)
            state = keep * state + _dot(k_rest.astype(BF16), v)
            o = _rms(out, hn_ref[:, vs])
            og = og_ref[rows, vs].astype(F32)
            a_ref[rows, vs] = (o * (og * (1.0 / (1.0 + jnp.exp(-og))))).astype(BF16)
        state_ref[hd] = state


def _gla_chunk(q, kt, v, og, lg, lgt, head_norm):
    b, s, dk = q.shape
    dv = v.shape[-1]
    c = GLA_TILE * GLA_STEP_CHUNKS
    fwd, rev, masks = _gla_constants()
    row = lambda bi, i: (bi, i, 0)
    colmajor = lambda bi, i: (bi, 0, i)
    return pl.pallas_call(
        _gla_chunk_kernel,
        grid=(b, s // c),
        in_specs=[
            pl.BlockSpec((None, c, dk), row),
            pl.BlockSpec((None, dk, c), colmajor),
            pl.BlockSpec((None, c, dv), row),
            pl.BlockSpec((None, c, dv), row),
            pl.BlockSpec((None, c, dk), row),
            pl.BlockSpec((None, dk, c), colmajor),
            _resident(fwd.shape),
            _resident(rev.shape),
            _resident(masks.shape),
            _resident((1, dv)),
        ],
        out_specs=pl.BlockSpec((None, c, dv), row),
        out_shape=jax.ShapeDtypeStruct((b, s, dv), BF16),
        scratch_shapes=[pltpu.VMEM((GLA_HEADS, GLA_DK_HEAD, GLA_DV_HEAD), F32)],
        compiler_params=_params(("parallel", "arbitrary")),
        name="gla_chunk",
    )(q, kt, v, og, lg, lgt, fwd, rev, masks, head_norm)


def _post_kernel(a_ref, h_ref, wo_ref, g_ref, wup_ref, wdn_ref, o_ref, acc_ref):
    half = TOKEN_TILE // 2
    halves = (slice(0, half), slice(half, TOKEN_TILE))
    w_o = wo_ref[...].astype(BF16)
    h1 = []
    xn = []
    for rows in halves:
        mixed = _dot(a_ref[rows, :], w_o)
        h1.append(h_ref[rows, :] + _rms(mixed, g_ref[1:2, :]))
        xn.append(_rms(h1[-1], g_ref[2:3, :]).astype(BF16))
    xn_full = jnp.concatenate(xn, axis=0)
    n_chunks = D_FF // FF_CHUNK
    for ci in range(n_chunks):
        cols = slice(ci * FF_CHUNK, (ci + 1) * FF_CHUNK)
        w_up = wup_ref[:, cols].astype(BF16)
        w_down = wdn_ref[cols, :].astype(BF16)
        if ci == 0:
            up = jnp.concatenate([_dot(part, w_up) for part in xn], axis=0)
        else:
            up = _dot(xn_full, w_up)
        up = jnp.maximum(up, 0.0)
        act = (up * up).astype(BF16)
        if ci == 0:
            acc_ref[...] = _dot(act, w_down)
        elif ci < n_chunks - 1:
            acc_ref[...] += _dot(act, w_down)
        else:
            for hi, rows in enumerate(halves):
                ffn = acc_ref[rows, :] + _dot(act[rows, :], w_down)
                o_ref[rows, :] = h1[hi] + _rms(ffn, g_ref[3:4, :])


def _layer_slice(shape, index):
    zeros = (0,) * len(shape)
    return pl.BlockSpec((None,) + tuple(shape), lambda *_: (index,) + zeros,
                        pipeline_mode=pl.Buffered(1))


def _post(a, h, w_o, mixer_index, gains, w_up, w_down, layer):
    b, s, d = h.shape
    tm = TOKEN_TILE
    row = lambda bi, i: (bi, i, 0)
    return pl.pallas_call(
        _post_kernel,
        grid=(b, s // tm),
        in_specs=[
            pl.BlockSpec((None, tm, d), row),
            pl.BlockSpec((None, tm, d), row),
            _layer_slice((d, d), mixer_index),
            _layer_slice((4, d), layer),
            _layer_slice((d, D_FF), layer),
            _layer_slice((D_FF, d), layer),
        ],
        out_specs=pl.BlockSpec((None, tm, d), row),
        out_shape=jax.ShapeDtypeStruct((b, s, d), F32),
        scratch_shapes=[pltpu.VMEM((tm, d), F32)],
        compiler_params=_params(("parallel", "parallel")),
        name="post_ffn",
    )(a, h, w_o, gains, w_up, w_down)


def kernel(x, norm_gains, sb_w_qkv, sb_w_o, conv_w_in, conv_w, conv_w_out, gla_w_in,
           gla_w_gate_up, gla_b_gate, gla_head_norm, gla_w_o, ffn_w_up, ffn_w_down):
    depth = norm_gains.shape[0]
    h = x
    for i in range(depth):
        kind, j = i % 3, i // 3
        gains = norm_gains[i]
        pre_gain = gains[0:1]
        if kind == 0:
            a = _sb_mixer(h, pre_gain, sb_w_qkv[j].astype(BF16))
            w_o = sb_w_o
        elif kind == 1:
            a = _conv_mixer(h, pre_gain, conv_w_in, conv_w, j)
            w_o = conv_w_out
        else:
            n_main = 2 * GLA_DK + 2 * GLA_DV
            w_in = gla_w_in[j]
            w_a = jnp.pad(w_in[:, n_main:], ((0, 0), (0, LANES - GLA_GATE_RANK))).astype(BF16)
            w_gu = jnp.pad(gla_w_gate_up[j], ((0, LANES - GLA_GATE_RANK), (0, 0))).astype(BF16)
            q, kt, v, og, lg, lgt = _gla_proj(h, pre_gain, gla_w_in, j, w_a, w_gu,
                                              gla_b_gate[j][None, :])
            a = _gla_chunk(q, kt, v, og, lg, lgt, gla_head_norm[j].reshape(1, GLA_DV))
            w_o = gla_w_o
        h = _post(a, h, w_o, j, norm_gains, ffn_w_up, ffn_w_down, i)
    return h
```

```python
import math

import numpy as np
import jax
import jax.numpy as jnp
from jax import lax
from jax.experimental import pallas as pl
from jax.experimental.pallas import tpu as pltpu

F32 = jnp.float32
BF16 = jnp.bfloat16

D_MODEL = 1024
D_FF = 4 * D_MODEL
RMS_EPS = 1e-6

SB_HEADS = 16
SB_HEAD_DIM = D_MODEL // SB_HEADS
LOG2E = math.log2(math.e)
SB_Q_SCALE = -(SB_HEAD_DIM ** -0.5) * LOG2E
CONV_WIDTH = 3
GLA_HEADS = 4
GLA_DK = D_MODEL // 2
GLA_DV = D_MODEL
GLA_DK_HEAD = GLA_DK // GLA_HEADS
GLA_DV_HEAD = GLA_DV // GLA_HEADS
GLA_GATE_RANK = 16
GLA_GATE_NORMALIZER = 16.0
GLA_SCALE = GLA_DK_HEAD ** -0.5

LANES = 128
SUBLANES = 8
VMEM_LIMIT_BYTES = 58 * 1024 * 1024

TOKEN_TILE = 512
PROJ_TILE = 1024
FF_CHUNK = 1024
SB_TILE = 512
SB_SUB = 128
SB_BAND_SUBS = 3
SB_GROUP = 1
SB_PAD_SUBS = SB_BAND_SUBS - 1
SB_HP_PER_STEP = 2
SB_F32_EXP2_ZERO = -152.0
GLA_TILE = 128
GLA_STEP_CHUNKS = 4
GLA_LEVELS = 7


def _rms(x, gain):
    ms = jnp.mean(x * x, axis=-1, keepdims=True)
    return x * lax.rsqrt(ms + RMS_EPS) * gain


def _neg_abs(z):
    bits = lax.bitcast_convert_type(z, jnp.uint32) | jnp.uint32(0x80000000)
    return lax.bitcast_convert_type(bits, F32)


def _log_sigmoid(z):
    return jnp.minimum(z, 0.0) - jnp.log(1.0 + jnp.exp(_neg_abs(z)))


def _log2_one_minus_sigmoid(y):
    return jnp.minimum(y, 0.0) - jnp.log(1.0 + jnp.exp2(_neg_abs(y))) * LOG2E


def _split_bf16(x):
    hi = x.astype(BF16)
    lo = (x - hi.astype(F32)).astype(BF16)
    return hi, lo


def _dot(a, b):
    return jnp.dot(a, b, preferred_element_type=F32)


def _resident(shape):
    zeros = (0,) * len(shape)
    return pl.BlockSpec(shape, lambda *_: zeros, pipeline_mode=pl.Buffered(1))


def _params(semantics):
    return pltpu.CompilerParams(dimension_semantics=semantics,
                                vmem_limit_bytes=VMEM_LIMIT_BYTES)


def _sb_suffix_matrix():
    t = SB_SUB
    m = np.arange(t)[:, None]
    j = np.arange(t)[None, :]
    half = np.concatenate([(m >= j).astype(np.float32), np.ones((t, t), np.float32)], axis=1)
    return jnp.asarray(np.concatenate([half, half], axis=0), dtype=BF16)


def _sb_layer_kernel(x0_ref, x1_ref, x2_ref, g_ref, w_ref, u_ref, o_ref,
                     q_scr, kt_scr, v_scr, kt_new, v_new, xn_buf, o_scr, acc_ref, c_ref, open_ref):
    t = SB_TILE
    sub = SB_SUB
    n_rb = t // sub
    band = SB_BAND_SUBS
    pad = SB_PAD_SUBS
    grp = SB_GROUP
    win = band + grp - 1
    pairs = n_rb // grp
    grp_rows = grp * sub
    n_hp = D_MODEL // LANES
    duo = SB_HP_PER_STEP
    i = pl.program_id(1)
    step = pl.program_id(0) * pl.num_programs(1) + i
    gain = g_ref[...]
    u = u_ref[...]

    def weight_columns(which, j):
        width = duo * LANES
        return w_ref[:, pl.ds(pl.multiple_of(which * D_MODEL + j * width, width), width)]

    def project_q(xn, j, slot):
        y = _dot(xn, weight_columns(0, j)) * SB_Q_SCALE
        for e in range(duo):
            q_scr[slot, j * duo + e] = y[:, e * LANES:(e + 1) * LANES].astype(BF16)

    def project_k(xn, j):
        y = _dot(xn, weight_columns(1, j))
        for e in range(duo):
            for kb in range(n_rb):
                kt_new[j * duo + e, kb] = (
                    y[kb * sub:(kb + 1) * sub, e * LANES:(e + 1) * LANES].T.astype(BF16))

    def project_v(xn, j):
        y = _dot(xn, weight_columns(2, j)).astype(BF16)
        for e in range(duo):
            v_new[j * duo + e] = y[:, e * LANES:(e + 1) * LANES]

    @pl.when(i == 0)
    def _():
        kt_scr[:, 0:pad] = jnp.zeros((n_hp, pad, LANES, sub), BF16)
        v_scr[:, 0:pad * sub, :] = jnp.zeros((n_hp, pad * sub, LANES), BF16)

    @pl.when(step == 0)
    def _():
        xn = _rms(x0_ref[...], gain).astype(BF16)

        def first_tile(j, carry):
            project_q(xn, j, 0)
            project_k(xn, j)
            project_v(xn, j)
            return carry

        lax.fori_loop(0, n_hp // duo, first_tile, 0)
        xn_buf[0] = _rms(x1_ref[...], gain).astype(BF16)

    slot = step % 2
    next_slot = (step + 1) % 2
    piece = t // (n_hp // duo)

    lane = lax.broadcasted_iota(jnp.int32, (t, LANES), 1)
    first = lane < SB_HEAD_DIM
    first_grp = lax.broadcasted_iota(jnp.int32, (grp_rows, LANES), 1) < SB_HEAD_DIM
    row = lax.broadcasted_iota(jnp.int32, (sub, sub), 0)
    col = lax.broadcasted_iota(jnp.int32, (sub, sub), 1)
    diag = col < row
    no_weight = jnp.zeros((sub, sub), BF16)
    row_blk = lax.broadcasted_iota(jnp.int32, (t, sub), 0) // sub
    not_first_tile = i > 0

    def append_projected(hp):
        for kb in range(n_rb):
            kt_scr[hp, pad + i * n_rb + kb] = kt_new[hp, kb]
        v_scr[hp, pl.ds(pl.multiple_of((pad + i * n_rb) * sub, sub), t), :] = v_new[hp]

    def split_heads(hp):
        q = q_scr[slot, hp]
        zero = jnp.zeros_like(q)
        return jnp.where(first, q, zero), jnp.where(first, zero, q)

    def band_logits(hp, q_heads):
        parts = []
        tiles = {}
        for p in range(pairs):
            first_sub = pad + i * n_rb + p * grp - band + 1
            kwin = jnp.concatenate([kt_scr[hp, first_sub + s] for s in range(win)], axis=1)
            q_both = jnp.concatenate(
                [q_heads[hd][p * grp_rows:(p + 1) * grp_rows] for hd in range(2)], axis=0)
            y = _dot(q_both, kwin)
            for hd in range(2):
                for rr in range(grp):
                    top = hd * grp_rows + rr * sub
                    for s in range(band):
                        yt = y[top:top + sub, (rr + s) * sub:(rr + s + 1) * sub]
                        part = _log2_one_minus_sigmoid(yt)
                        mask = diag if s == band - 1 else None
                        if p * grp + rr + s < band - 1:
                            mask = not_first_tile if mask is None else mask & not_first_tile
                        if mask is not None:
                            part = jnp.where(mask, part, 0.0)
                        hi, lo = _split_bf16(part)
                        tiles[p, hd, rr, s] = (len(parts), yt, mask)
                        parts.append(jnp.concatenate([hi, lo], axis=1))
        return jnp.concatenate(parts, axis=0), tiles

    def band_weights(hp, s2, tiles, p):
        first_sub = pad + i * n_rb + p * grp - band + 1
        w_rows = []
        carries = []
        for hd in range(2):
            for rr in range(grp):
                r = p * grp + rr
                c = jnp.zeros((sub, sub), F32)
                ws = [None] * band
                for s in reversed(range(band)):
                    index, yt, mask = tiles[p, hd, rr, s]
                    blk = s2[index * sub:(index + 1) * sub]
                    w = jnp.exp2(blk[:, :sub] + c - yt)
                    if mask is not None:
                        w = jnp.where(mask, w, 0.0)
                    ws[s] = w.astype(BF16)
                    c = c + blk[:, sub:]
                c_ref[hp, hd, r * sub:(r + 1) * sub, :] = c
                carries.append(c)
                w_rows.append(jnp.concatenate(
                    [no_weight] * rr + ws + [no_weight] * (grp - 1 - rr), axis=1))
        vwin = v_scr[hp, pl.ds(pl.multiple_of(first_sub * sub, sub), win * sub), :]
        pv = _dot(jnp.concatenate(w_rows, axis=0), vwin)
        rows = slice(p * grp_rows, (p + 1) * grp_rows)
        for hd in range(2):
            acc_ref[hp, hd, rows, :] = pv[hd * grp_rows:(hd + 1) * grp_rows]
        o_scr[hp, rows, :] = jnp.where(first_grp, pv[:grp_rows], pv[grp_rows:]).astype(BF16)
        return jnp.concatenate(
            [jnp.maximum(carries[rr], carries[grp + rr]) for rr in range(grp)], axis=0)

    def largest_open_carry(c, m):
        has_keys = row_blk + i * n_rb - band + 1 - m >= 0
        return jnp.max(jnp.where(has_keys, c, -jnp.inf))

    def extend_band(hp, q_heads, m):
        parts = []
        tiles = {}
        starts = []
        for p in range(pairs):
            first_sub = i * n_rb + p * grp - band + 1 - m
            start = jnp.maximum(pad + first_sub, 0)
            starts.append(start)
            kwin = jnp.concatenate([kt_scr[hp, start + s] for s in range(grp)], axis=1)
            q_both = jnp.concatenate(
                [q_heads[hd][p * grp_rows:(p + 1) * grp_rows] for hd in range(2)], axis=0)
            y = _dot(q_both, kwin)
            for hd in range(2):
                for rr in range(grp):
                    top = hd * grp_rows + rr * sub
                    yt = y[top:top + sub, rr * sub:(rr + 1) * sub]
                    in_sequence = first_sub + rr >= 0
                    part = jnp.where(in_sequence, _log2_one_minus_sigmoid(yt), 0.0)
                    hi, lo = _split_bf16(part)
                    tiles[p, hd, rr] = (len(parts), yt, in_sequence)
                    parts.append(jnp.concatenate([hi, lo], axis=1))
        s2 = _dot(jnp.concatenate(parts, axis=0), u)
        for p in range(pairs):
            w_rows = []
            for hd in range(2):
                for rr in range(grp):
                    rows = slice((p * grp + rr) * sub, (p * grp + rr + 1) * sub)
                    index, yt, in_sequence = tiles[p, hd, rr]
                    blk = s2[index * sub:(index + 1) * sub]
                    c = c_ref[hp, hd, rows, :]
                    w = jnp.where(in_sequence, jnp.exp2(blk[:, :sub] + c - yt), 0.0).astype(BF16)
                    c_ref[hp, hd, rows, :] = c + blk[:, sub:]
                    w_rows.append(jnp.concatenate(
                        [no_weight] * rr + [w] + [no_weight] * (grp - 1 - rr), axis=1))
            vwin = v_scr[hp, pl.ds(pl.multiple_of(starts[p] * sub, sub), grp * sub), :]
            pv = _dot(jnp.concatenate(w_rows, axis=0), vwin)
            rows = slice(p * grp_rows, (p + 1) * grp_rows)
            for hd in range(2):
                acc_ref[hp, hd, rows, :] += pv[hd * grp_rows:(hd + 1) * grp_rows]

    def further_left(hp):
        q_heads = split_heads(hp)
        m_last = i * n_rb + n_rb - band

        def more_keys_matter(state):
            m, c_max = state
            return jnp.logical_and(m <= m_last, c_max > SB_F32_EXP2_ZERO)

        def one_more_sub_block(state):
            m, _ = state
            extend_band(hp, q_heads, m)
            return m + 1, largest_open_carry(c_ref[hp], m + 1)

        lax.while_loop(more_keys_matter, one_more_sub_block, (1, open_ref[hp]))
        o_scr[hp] = jnp.where(first, acc_ref[hp, 0], acc_ref[hp, 1]).astype(BF16)

    def per_head_pair_duo(j, largest_carry):
        hps = [j * duo + e for e in range(duo)]
        xn_next = xn_buf[slot]
        for hp in hps:
            append_projected(hp)
        q_heads = [split_heads(hp) for hp in hps]
        stacked = [band_logits(hp, q_heads[e]) for e, hp in enumerate(hps)]
        project_q(xn_next, j, next_slot)
        project_k(xn_next, j)
        s2 = []
        for e in range(duo):
            s2.append(_dot(stacked[e][0], u))
            if e == 0:
                project_v(xn_next, j)
        for e, hp in enumerate(hps):
            carry = [band_weights(hp, s2[e], stacked[e][1], p) for p in range(pairs)]
            largest_carry = jnp.maximum(largest_carry, jnp.concatenate(carry, axis=0))
        rows = pl.ds(pl.multiple_of(j * piece, piece), piece)
        xn_buf[next_slot, rows, :] = _rms(x2_ref[rows, :], gain).astype(BF16)
        return largest_carry

    largest_carry = lax.fori_loop(0, n_hp // duo, per_head_pair_duo,
                                  jnp.full((t, sub), -jnp.inf, F32))

    @pl.when(largest_open_carry(largest_carry, 1) > SB_F32_EXP2_ZERO)
    def _():
        for hp in range(n_hp):
            open_ref[hp] = largest_open_carry(c_ref[hp], 1)

        def visit(hp, carry):
            further_left(hp)
            return carry

        lax.fori_loop(0, n_hp, visit, 0)

    o_ref[...] = jnp.concatenate([o_scr[hp] for hp in range(n_hp)], axis=1)


def _sb_mixer(h, gain, w_qkv):
    b, s, d = h.shape
    t = SB_TILE
    n_hp = d // LANES
    duo = SB_HP_PER_STEP
    n_tiles = s // t
    n_subs = SB_PAD_SUBS + s // SB_SUB

    def tile_after_next(bi, i):
        tile = jnp.minimum(bi * n_tiles + i + 2, b * n_tiles - 1)
        return tile // n_tiles, tile % n_tiles, 0

    return pl.pallas_call(
        _sb_layer_kernel,
        grid=(b, n_tiles),
        in_specs=[
            pl.BlockSpec((None, t, d), lambda bi, i: (0, 0, 0), pipeline_mode=pl.Buffered(1)),
            pl.BlockSpec((None, t, d), lambda bi, i: (0, 1, 0), pipeline_mode=pl.Buffered(1)),
            pl.BlockSpec((None, t, d), tile_after_next),
            _resident((1, d)),
            _resident((d, 3 * d)),
            _resident((2 * SB_SUB, 2 * SB_SUB)),
        ],
        out_specs=pl.BlockSpec((None, t, d), lambda bi, i: (bi, i, 0)),
        out_shape=jax.ShapeDtypeStruct((b, s, d), BF16),
        scratch_shapes=[
            pltpu.VMEM((2, n_hp, t, LANES), BF16),
            pltpu.VMEM((n_hp, n_subs, LANES, SB_SUB), BF16),
            pltpu.VMEM((n_hp, n_subs * SB_SUB, LANES), BF16),
            pltpu.VMEM((n_hp, t // SB_SUB, LANES, SB_SUB), BF16),
            pltpu.VMEM((n_hp, t, LANES), BF16),
            pltpu.VMEM((2, t, d), BF16),
            pltpu.VMEM((n_hp, t, LANES), BF16),
            pltpu.VMEM((n_hp, 2, t, LANES), F32),
            pltpu.VMEM((n_hp, 2, t, SB_SUB), F32),
            pltpu.SMEM((n_hp,), F32),
        ],
        compiler_params=_params(("arbitrary", "arbitrary")),
        name="sb_layer",
    )(h, h, h, gain, w_qkv, _sb_suffix_matrix())


def _conv_kernel(x_ref, xp_ref, g_ref, w_ref, cw_ref, a_ref, hbuf_ref):
    d = D_MODEL
    tm = PROJ_TILE
    i = pl.program_id(1)
    gain = g_ref[...]
    xn = _rms(x_ref[...], gain).astype(BF16)
    xpn = _rms(xp_ref[...], gain).astype(BF16)
    w_c = w_ref[:, d:2 * d].astype(BF16)
    w_u = w_ref[:, 2 * d:3 * d].astype(BF16)
    hbuf_ref[SUBLANES:, :] = _dot(xn, w_c) * _dot(xn, w_u)
    h_prev = _dot(xpn, w_c) * _dot(xpn, w_u)
    hbuf_ref[0:SUBLANES, :] = jnp.where(i > 0, h_prev, 0.0)
    conv = cw_ref[0:1, :] * hbuf_ref[pl.ds(SUBLANES - 2, tm), :]
    conv = conv + cw_ref[1:2, :] * hbuf_ref[pl.ds(SUBLANES - 1, tm), :]
    conv = conv + cw_ref[2:3, :] * hbuf_ref[pl.ds(SUBLANES, tm), :]
    a_ref[...] = (_dot(xn, w_ref[:, 0:d].astype(BF16)) * conv).astype(BF16)


def _conv_mixer(h, gain, w_in, conv_w, index):
    b, s, d = h.shape
    tm = PROJ_TILE
    rows_per_tile = tm // SUBLANES
    return pl.pallas_call(
        _conv_kernel,
        grid=(b, s // tm),
        in_specs=[
            pl.BlockSpec((None, tm, d), lambda bi, i: (bi, i, 0)),
            pl.BlockSpec((None, SUBLANES, d),
                         lambda bi, i: (bi, jnp.maximum(i * rows_per_tile - 1, 0), 0)),
            _resident((1, d)),
            _layer_slice((d, 3 * d), index),
            _layer_slice((CONV_WIDTH, d), index),
        ],
        out_specs=pl.BlockSpec((None, tm, d), lambda bi, i: (bi, i, 0)),
        out_shape=jax.ShapeDtypeStruct((b, s, d), BF16),
        scratch_shapes=[pltpu.VMEM((tm + SUBLANES, d), F32)],
        compiler_params=_params(("parallel", "parallel")),
        name="conv_mixer",
    )(h, h, gain, w_in, conv_w)


def _gla_proj_kernel(x_ref, g_ref, w_ref, wa_ref, wgu_ref, bg_ref,
                     q_ref, kt_ref, v_ref, og_ref, lg_ref, lgt_ref):
    dk, dv = GLA_DK, GLA_DV
    xn = _rms(x_ref[...], g_ref[...]).astype(BF16)
    a_low = _dot(xn, wa_ref[...]).astype(BF16)
    pre = _dot(a_low, wgu_ref[...]) + bg_ref[...]
    k = _dot(xn, w_ref[:, dk:2 * dk].astype(BF16))
    lg = _log_sigmoid(pre) * (LOG2E / GLA_GATE_NORMALIZER)
    lg_ref[...] = lg
    lgt_ref[...] = lg.T
    kt_ref[...] = k.T.astype(BF16)
    q_ref[...] = _dot(xn, w_ref[:, 0:dk].astype(BF16)).astype(BF16)
    v_ref[...] = _dot(xn, w_ref[:, 2 * dk:2 * dk + dv].astype(BF16)).astype(BF16)
    og_ref[...] = _dot(xn, w_ref[:, 2 * dk + dv:2 * dk + 2 * dv].astype(BF16)).astype(BF16)


def _gla_proj(h, gain, w_in, index, w_a, w_gu, b_gate):
    b, s, d = h.shape
    tm = PROJ_TILE
    dk, dv = GLA_DK, GLA_DV
    row = lambda bi, i: (bi, i, 0)
    colmajor = lambda bi, i: (bi, 0, i)
    return pl.pallas_call(
        _gla_proj_kernel,
        grid=(b, s // tm),
        in_specs=[
            pl.BlockSpec((None, tm, d), row),
            _resident((1, d)),
            _layer_slice((d, 2 * dk + 2 * dv), index),
            _resident((d, LANES)),
            _resident((LANES, dk)),
            _resident((1, dk)),
        ],
        out_specs=[
            pl.BlockSpec((None, tm, dk), row),
            pl.BlockSpec((None, dk, tm), colmajor),
            pl.BlockSpec((None, tm, dv), row),
            pl.BlockSpec((None, tm, dv), row),
            pl.BlockSpec((None, tm, dk), row),
            pl.BlockSpec((None, dk, tm), colmajor),
        ],
        out_shape=[
            jax.ShapeDtypeStruct((b, s, dk), BF16),
            jax.ShapeDtypeStruct((b, dk, s), BF16),
            jax.ShapeDtypeStruct((b, s, dv), BF16),
            jax.ShapeDtypeStruct((b, s, dv), BF16),
            jax.ShapeDtypeStruct((b, s, dk), F32),
            jax.ShapeDtypeStruct((b, dk, s), F32),
        ],
        compiler_params=_params(("parallel", "parallel")),
        name="gla_proj",
    )(h, gain, w_in, w_a, w_gu, b_gate)


def _gla_constants():
    c = GLA_TILE
    i = np.arange(c)[:, None]
    m = np.arange(c)[None, :]
    fwd, rev, masks = [], [], []
    for level in range(1, GLA_LEVELS + 1):
        n = 1 << level
        same = (i // n) == (m // n)
        fwd.append((same & (m <= i)).astype(np.float32))
        rev.append((same & (m > i)).astype(np.float32).T)
    rev.append(np.ones((c, c), np.float32))
    for level in range(GLA_LEVELS):
        n = 1 << level
        masks.append(((i // (2 * n) == m // (2 * n)) & ((i // n) % 2 == 1)
                      & ((m // n) % 2 == 0)).astype(np.float32))
    masks.append((i == m).astype(np.float32))
    fwd = np.concatenate(fwd, axis=0)
    rev = np.concatenate(rev, axis=1)
    fwd2 = np.concatenate([fwd, fwd], axis=1)
    rev2 = np.concatenate([rev, rev], axis=0)
    return (jnp.asarray(fwd2, dtype=BF16), jnp.asarray(rev2, dtype=BF16),
            jnp.asarray(np.stack(masks), dtype=F32))


def _gla_chunk_kernel(q_ref, kt_ref, v_ref, og_ref, lg_ref, lgt_ref, fwd_ref, rev_ref,
                      mask_ref, hn_ref, a_ref, state_ref):
    c = GLA_TILE
    n_ch = GLA_STEP_CHUNKS
    dk = GLA_DK
    dkh, dvh = GLA_DK_HEAD, GLA_DV_HEAD
    n_lv = GLA_LEVELS

    @pl.when(pl.program_id(1) == 0)
    def _():
        state_ref[...] = jnp.zeros_like(state_ref)

    lg = lg_ref[...]
    hi, lo = _split_bf16(lg)
    g_cat = jnp.concatenate(
        [jnp.concatenate([hi[ch * c:(ch + 1) * c], lo[ch * c:(ch + 1) * c]], axis=0)
         for ch in range(n_ch)], axis=1)
    fwd_all = _dot(fwd_ref[...], g_cat)
    hi_t, lo_t = _split_bf16(lgt_ref[...])
    gt_cat = jnp.concatenate(
        [jnp.concatenate([hi_t[:, ch * c:(ch + 1) * c], lo_t[:, ch * c:(ch + 1) * c]], axis=1)
         for ch in range(n_ch)], axis=0)
    rev_all = _dot(gt_cat, rev_ref[...])

    for hd in range(GLA_HEADS):
        ks = slice(hd * dkh, (hd + 1) * dkh)
        vs = slice(hd * dvh, (hd + 1) * dvh)
        state = state_ref[hd]
        for ch in range(n_ch):
            rows = slice(ch * c, (ch + 1) * c)
            kcols = slice(ch * dk + hd * dkh, ch * dk + (hd + 1) * dkh)
            q = q_ref[rows, ks].astype(F32) * GLA_SCALE
            kt = kt_ref[ks, rows].astype(F32)
            v = v_ref[rows, vs]
            kt_bf = kt.astype(BF16)
            scores = _dot(q.astype(BF16), kt_bf) * mask_ref[n_lv]
            qd = q * jnp.exp2(lg[rows, ks])
            scores = scores + _dot(qd.astype(BF16), kt_bf) * mask_ref[0]
            for lv in range(1, n_lv):
                qd = q * jnp.exp2(fwd_all[(lv - 1) * c:lv * c, kcols])
                kd = kt * jnp.exp2(rev_all[kcols, (lv - 1) * c:lv * c])
                scores = scores + _dot(qd.astype(BF16), kd.astype(BF16)) * mask_ref[lv]
            q_cum = q * jnp.exp2(fwd_all[(n_lv - 1) * c:n_lv * c, kcols])
            out = _dot(jnp.concatenate([q_cum.astype(BF16), scores.astype(BF16)], axis=1),
                       jnp.concatenate([state.astype(BF16), v], axis=0))
            k_rest = kt * jnp.exp2(rev_all[kcols, (n_lv - 1) * c:n_lv * c])
            keep = jnp.exp2(rev_all[kcols, n_lv * c:(n_lv + 1) * c])
            keep = jnp.concatenate([keep] * (dvh // c), axis=1)
            state = keep * state + _dot(k_rest.astype(BF16), v)
            o = _rms(out, hn_ref[:, vs])
            og = og_ref[rows, vs].astype(F32)
            a_ref[rows, vs] = (o * (og * (1.0 / (1.0 + jnp.exp(-og))))).astype(BF16)
        state_ref[hd] = state


def _gla_chunk(q, kt, v, og, lg, lgt, head_norm):
    b, s, dk = q.shape
    dv = v.shape[-1]
    c = GLA_TILE * GLA_STEP_CHUNKS
    fwd, rev, masks = _gla_constants()
    row = lambda bi, i: (bi, i, 0)
    colmajor = lambda bi, i: (bi, 0, i)
    return pl.pallas_call(
        _gla_chunk_kernel,
        grid=(b, s // c),
        in_specs=[
            pl.BlockSpec((None, c, dk), row),
            pl.BlockSpec((None, dk, c), colmajor),
            pl.BlockSpec((None, c, dv), row),
            pl.BlockSpec((None, c, dv), row),
            pl.BlockSpec((None, c, dk), row),
            pl.BlockSpec((None, dk, c), colmajor),
            _resident(fwd.shape),
            _resident(rev.shape),
            _resident(masks.shape),
            _resident((1, dv)),
        ],
        out_specs=pl.BlockSpec((None, c, dv), row),
        out_shape=jax.ShapeDtypeStruct((b, s, dv), BF16),
        scratch_shapes=[pltpu.VMEM((GLA_HEADS, GLA_DK_HEAD, GLA_DV_HEAD), F32)],
        compiler_params=_params(("parallel", "arbitrary")),
        name="gla_chunk",
    )(q, kt, v, og, lg, lgt, fwd, rev, masks, head_norm)


def _post_kernel(a_ref, h_ref, wo_ref, g_ref, wup_ref, wdn_ref, o_ref, acc_ref):
    half = TOKEN_TILE // 2
    halves = (slice(0, half), slice(half, TOKEN_TILE))
    w_o = wo_ref[...].astype(BF16)
    h1 = []
    xn = []
    for rows in halves:
        mixed = _dot(a_ref[rows, :], w_o)
        h1.append(h_ref[rows, :] + _rms(mixed, g_ref[1:2, :]))
        xn.append(_rms(h1[-1], g_ref[2:3, :]).astype(BF16))
    xn_full = jnp.concatenate(xn, axis=0)
    n_chunks = D_FF // FF_CHUNK
    for ci in range(n_chunks):
        cols = slice(ci * FF_CHUNK, (ci + 1) * FF_CHUNK)
        w_up = wup_ref[:, cols].astype(BF16)
        w_down = wdn_ref[cols, :].astype(BF16)
        if ci == 0:
            up = jnp.concatenate([_dot(part, w_up) for part in xn], axis=0)
        else:
            up = _dot(xn_full, w_up)
        up = jnp.maximum(up, 0.0)
        act = (up * up).astype(BF16)
        if ci == 0:
            acc_ref[...] = _dot(act, w_down)
        elif ci < n_chunks - 1:
            acc_ref[...] += _dot(act, w_down)
        else:
            for hi, rows in enumerate(halves):
                ffn = acc_ref[rows, :] + _dot(act[rows, :], w_down)
                o_ref[rows, :] = h1[hi] + _rms(ffn, g_ref[3:4, :])


def _layer_slice(shape, index):
    zeros = (0,) * len(shape)
    return pl.BlockSpec((None,) + tuple(shape), lambda *_: (index,) + zeros,
                        pipeline_mode=pl.Buffered(1))


def _post(a, h, w_o, mixer_index, gains, w_up, w_down, layer):
    b, s, d = h.shape
    tm = TOKEN_TILE
    row = lambda bi, i: (bi, i, 0)
    return pl.pallas_call(
        _post_kernel,
        grid=(b, s // tm),
        in_specs=[
            pl.BlockSpec((None, tm, d), row),
            pl.BlockSpec((None, tm, d), row),
            _layer_slice((d, d), mixer_index),
            _layer_slice((4, d), layer),
            _layer_slice((d, D_FF), layer),
            _layer_slice((D_FF, d), layer),
        ],
        out_specs=pl.BlockSpec((None, tm, d), row),
        out_shape=jax.ShapeDtypeStruct((b, s, d), F32),
        scratch_shapes=[pltpu.VMEM((tm, d), F32)],
        compiler_params=_params(("parallel", "parallel")),
        name="post_ffn",
    )(a, h, w_o, gains, w_up, w_down)


def kernel(x, norm_gains, sb_w_qkv, sb_w_o, conv_w_in, conv_w, conv_w_out, gla_w_in,
           gla_w_gate_up, gla_b_gate, gla_head_norm, gla_w_o, ffn_w_up, ffn_w_down):
    depth = norm_gains.shape[0]
    h = x
    for i in range(depth):
        kind, j = i % 3, i // 3
        gains = norm_gains[i]
        pre_gain = gains[0:1]
        if kind == 0:
            a = _sb_mixer(h, pre_gain, sb_w_qkv[j].astype(BF16))
            w_o = sb_w_o
        elif kind == 1:
            a = _conv_mixer(h, pre_gain, conv_w_in, conv_w, j)
            w_o = conv_w_out
        else:
            n_main = 2 * GLA_DK + 2 * GLA_DV
            w_in = gla_w_in[j]
            w_a = jnp.pad(w_in[:, n_main:], ((0, 0), (0, LANES - GLA_GATE_RANK))).astype(BF16)
            w_gu = jnp.pad(gla_w_gate_up[j], ((0, LANES - GLA_GATE_RANK), (0, 0))).astype(BF16)
            w_main = gla_w_in[:, :, :n_main].astype(BF16)
            q, kt, v, og, lg, lgt = _gla_proj(h, pre_gain, w_main, j, w_a, w_gu,
                                              gla_b_gate[j][None, :])
            a = _gla_chunk(q, kt, v, og, lg, lgt, gla_head_norm[j].reshape(1, GLA_DV))
            w_o = gla_w_o
        h = _post(a, h, w_o, j, norm_gains, ffn_w_up, ffn_w_down, i)
    return h
```
